```python
import jax, jax.numpy as jnp
from jax import lax
import numpy as np

D_MODEL = 1024
BATCH = 16
SEQ = 2048
DEPTH = 1

D_MIX = D_MODEL
MLA_HEADS = 8
MLA_V_DIM = (D_MIX // 2) // MLA_HEADS
MLA_NOPE_DIM = 64
MLA_ROPE_DIM = 32
MLA_QK_DIM = MLA_NOPE_DIM + MLA_ROPE_DIM
MLA_Q_LORA = 384
MLA_KV_LORA = 256
ROPE_THETA = 10000.0
Q_BLOCK = 128
HG_HEADS = 4
HG_HEAD_DIM = (D_MIX // 2) // HG_HEADS
HG_WIDTH = HG_HEADS * HG_HEAD_DIM
HG_CHUNK = 64
N_GROUPS = 4
EXPERTS_PER_GROUP = 8
N_EXPERTS = N_GROUPS * EXPERTS_PER_GROUP
TOP_K = 2
D_EXPERT = D_MODEL // 2
MOE_BLOCK = 128
EPS = 1e-6
IN_SPLITS = (MLA_Q_LORA, MLA_KV_LORA, MLA_ROPE_DIM, HG_WIDTH, HG_WIDTH, HG_WIDTH, HG_WIDTH, HG_WIDTH)
IN_COLS = MLA_Q_LORA + MLA_KV_LORA + MLA_ROPE_DIM + 5 * HG_WIDTH

kernel_name = 'hymba_mla_hgrn2_hmoe_adaln_encoder'


def rms_norm(x, g):
    xf = x.astype(jnp.float32)
    y = xf * lax.rsqrt(jnp.mean(xf * xf, axis=-1, keepdims=True) + EPS)
    return (y * g.astype(jnp.float32)).astype(x.dtype)


def modulate(h, shift, scale):
    return h * (1 + scale[:, None, :]) + shift[:, None, :]


def apply_rope(x, positions):
    half = MLA_ROPE_DIM // 2
    inv_freq = ROPE_THETA ** (-jnp.arange(half, dtype=jnp.float32) / half)
    ang = positions.astype(jnp.float32)[:, :, None, None] * inv_freq
    cos, sin = jnp.cos(ang), jnp.sin(ang)
    xf = x.astype(jnp.float32)
    x1, x2 = xf[..., :half], xf[..., half:]
    return jnp.concatenate([x1 * cos - x2 * sin, x2 * cos + x1 * sin], axis=-1).astype(x.dtype)


def mla_mixer(q_lat, kv_lat, k_rope, positions, qa_g, wq_up, kva_g, wkv_up, qn_g, kn_g):
    B, S, _ = q_lat.shape
    q = (rms_norm(q_lat, qa_g) @ wq_up).reshape(B, S, MLA_HEADS, MLA_QK_DIM)
    kv = (rms_norm(kv_lat, kva_g) @ wkv_up).reshape(B, S, MLA_HEADS, MLA_NOPE_DIM + MLA_V_DIM)
    k_nope, v = kv[..., :MLA_NOPE_DIM], kv[..., MLA_NOPE_DIM:]
    k_shared = jnp.broadcast_to(k_rope[:, :, None, :], (B, S, MLA_HEADS, MLA_ROPE_DIM))
    k = jnp.concatenate([k_nope, k_shared], axis=-1)
    q = rms_norm(q, qn_g)
    k = rms_norm(k, kn_g)
    q = jnp.concatenate([q[..., :MLA_NOPE_DIM], apply_rope(q[..., MLA_NOPE_DIM:], positions)], axis=-1)
    k = jnp.concatenate([k[..., :MLA_NOPE_DIM], apply_rope(k[..., MLA_NOPE_DIM:], positions)], axis=-1)
    scale = MLA_QK_DIM ** -0.5
    n_blocks = S // Q_BLOCK
    q_blocks = q.reshape(B, n_blocks, Q_BLOCK, MLA_HEADS, MLA_QK_DIM).transpose(1, 0, 2, 3, 4)

    def attend(qb):
        s = jnp.einsum('bqhd,bkhd->bhqk', qb, k).astype(jnp.float32) * scale
        p = jax.nn.softmax(s, axis=-1).astype(v.dtype)
        return jnp.einsum('bhqk,bkhv->bqhv', p, v)

    o = lax.map(attend, q_blocks)
    return o.transpose(1, 0, 2, 3, 4).reshape(B, S, MLA_HEADS * MLA_V_DIM)


def gla_chunk_scan(q, k, v, log_f):
    B, S, H, K = q.shape
    V = v.shape[-1]
    N = S // HG_CHUNK

    def to_chunks(t):
        return t.reshape(B, N, HG_CHUNK, H, t.shape[-1]).transpose(0, 3, 1, 2, 4)

    q, k, v, log_f = to_chunks(q), to_chunks(k), to_chunks(v), to_chunks(log_f)
    b = jnp.cumsum(log_f, axis=3)
    b_last = b[:, :, :, -1:, :]
    q_dec = q * jnp.exp(b)
    att = jnp.einsum('bhnck,bhnsk->bhncs', q_dec, k * jnp.exp(-b))
    mask = jnp.tril(jnp.ones((HG_CHUNK, HG_CHUNK), dtype=bool))
    att = jnp.where(mask, att, 0.0)
    o_intra = jnp.einsum('bhncs,bhnsv->bhncv', att, v)
    kv = jnp.einsum('bhnck,bhncv->bhnkv', k * jnp.exp(b_last - b), v)
    decay = jnp.exp(b_last[:, :, :, 0, :])

    def step(state, inp):
        d, u = inp
        return d[..., None] * state + u, state

    init = jnp.zeros((B, H, K, V), q.dtype)
    _, s_prev = lax.scan(step, init, (jnp.moveaxis(decay, 2, 0), jnp.moveaxis(kv, 2, 0)))
    s_prev = jnp.moveaxis(s_prev, 0, 2)
    o = o_intra + jnp.einsum('bhnck,bhnkv->bhncv', q_dec, s_prev)
    return o.transpose(0, 2, 3, 1, 4).reshape(B, S, H, V)


def hgrn2_mixer(hq, hf_fwd, hf_bwd, hi, hg, lower, norm_g):
    B, S, _ = hq.shape

    def heads(t):
        return t.reshape(B, S, HG_HEADS, HG_HEAD_DIM).astype(jnp.float32)

    q = jax.nn.silu(heads(hq))
    v = heads(hi)

    def gates(f_logits, lb):
        lb = lb.reshape(HG_HEADS, HG_HEAD_DIM)
        f = lb + (1.0 - lb) * jax.nn.sigmoid(heads(f_logits))
        return 1.0 - f, jnp.log(f)

    k_f, lf_f = gates(hf_fwd, lower[0])
    k_b, lf_b = gates(hf_bwd, lower[1])
    o_fwd = gla_chunk_scan(q, k_f, v, lf_f)

    def flip(t):
        return jnp.flip(t, axis=1)

    o_bwd = flip(gla_chunk_scan(flip(q), flip(k_b), flip(v), flip(lf_b)))
    o = rms_norm(o_fwd + o_bwd, norm_g) * jax.nn.silu(heads(hg))
    return o.reshape(B, S, HG_WIDTH).astype(hq.dtype)


def hierarchical_moe(h, wg, bg, we, be, w_gate, w_up, w_down):
    B, S, D = h.shape
    T = B * S
    A = T * TOP_K
    ht = h.reshape(T, D)
    group_prob = jax.nn.softmax((ht @ wg).astype(jnp.float32) + bg.astype(jnp.float32), axis=-1)
    group_p, group_idx = lax.top_k(group_prob, 1)
    expert_logits = ((ht @ we).astype(jnp.float32) + be.astype(jnp.float32)).reshape(T, N_GROUPS, EXPERTS_PER_GROUP)
    in_group = jnp.take_along_axis(expert_logits, group_idx[:, :, None], axis=1)[:, 0]
    top_val, top_idx = lax.top_k(in_group, TOP_K)
    weights = group_p * jax.nn.softmax(top_val, axis=-1)
    expert_id = group_idx * EXPERTS_PER_GROUP + top_idx
    e_flat = expert_id.reshape(A)
    tok_flat = jnp.repeat(jnp.arange(T, dtype=jnp.int32), TOP_K)
    w_flat = weights.reshape(A)
    order = jnp.argsort(e_flat)
    e_sorted = e_flat[order]
    counts = jnp.bincount(e_flat, length=N_EXPERTS)
    padded = (counts + MOE_BLOCK - 1) // MOE_BLOCK * MOE_BLOCK
    starts = jnp.cumsum(counts) - counts
    padded_ends = jnp.cumsum(padded)
    padded_starts = padded_ends - padded
    dest = padded_starts[e_sorted] + jnp.arange(A, dtype=jnp.int32) - starts[e_sorted]
    cap = A + N_EXPERTS * MOE_BLOCK
    n_blocks = cap // MOE_BLOCK
    buf_tok = jnp.zeros((cap,), jnp.int32).at[dest].set(tok_flat[order])
    buf_w = jnp.zeros((cap,), h.dtype).at[dest].set(w_flat[order].astype(h.dtype))
    block_expert = jnp.clip(jnp.searchsorted(padded_ends, jnp.arange(n_blocks, dtype=jnp.int32) * MOE_BLOCK, side='right'), 0, N_EXPERTS - 1)
    xb = ht[buf_tok].reshape(n_blocks, MOE_BLOCK, D)

    def expert_block(args):
        xe, e = args
        hid = jax.nn.silu(xe @ w_gate[e]) * (xe @ w_up[e])
        return hid @ w_down[e]

    yb = lax.map(expert_block, (xb, block_expert)).reshape(cap, D)
    out = jnp.zeros((T, D), h.dtype).at[buf_tok].add(yb * buf_w[:, None])
    return out.reshape(B, S, D)


def setup_inputs(seed: int = 0) -> dict:
    key = jax.random.key(seed)
    ks = jax.random.split(key, 24)
    f32 = jnp.float32
    L = DEPTH

    def nrm(k, shape, fan_in):
        return jax.random.normal(k, shape, f32) * fan_in ** -0.5

    def gain(k, shape):
        return 1.0 + 0.05 * jax.random.normal(k, shape, f32)

    return {
        'x': jax.random.normal(ks[0], (BATCH, SEQ, D_MODEL), f32),
        'c': jax.random.normal(ks[1], (BATCH, D_MODEL), f32),
        'positions': jnp.arange(SEQ, dtype=jnp.int32)[None, :] + jax.random.randint(ks[2], (BATCH, 1), 0, 4096, dtype=jnp.int32),
        'ada_w': nrm(ks[3], (L, D_MODEL, 6 * D_MODEL), D_MODEL),
        'ada_b': 0.02 * jax.random.normal(ks[4], (L, 6 * D_MODEL), f32),
        'norm1_g': gain(ks[5], (L, D_MODEL)),
        'w_in': nrm(ks[6], (L, D_MODEL, IN_COLS), D_MODEL),
        'mla_qa_g': gain(ks[7], (L, MLA_Q_LORA)),
        'mla_wq_up': nrm(ks[8], (L, MLA_Q_LORA, MLA_HEADS * MLA_QK_DIM), MLA_Q_LORA),
        'mla_kva_g': gain(ks[9], (L, MLA_KV_LORA)),
        'mla_wkv_up': nrm(ks[10], (L, MLA_KV_LORA, MLA_HEADS * (MLA_NOPE_DIM + MLA_V_DIM)), MLA_KV_LORA),
        'mla_qn_g': gain(ks[11], (L, MLA_QK_DIM)),
        'mla_kn_g': gain(ks[12], (L, MLA_QK_DIM)),
        'hg_lb_logits': 0.1 * jax.random.normal(ks[13], (2, L + 1, HG_WIDTH), f32),
        'hg_norm_g': gain(ks[14], (L, HG_HEAD_DIM)),
        'w_out': nrm(ks[15], (L, D_MIX, D_MODEL), D_MIX),
        'norm2_g': gain(ks[16], (L, D_MODEL)),
        'router_group_w': nrm(ks[17], (L, D_MODEL, N_GROUPS), D_MODEL),
        'router_group_b': 0.01 * jax.random.normal(ks[18], (L, N_GROUPS), f32),
        'router_expert_w': nrm(ks[19], (L, D_MODEL, N_EXPERTS), D_MODEL),
        'router_expert_b': 0.01 * jax.random.normal(ks[20], (L, N_EXPERTS), f32),
        'w_gate': nrm(ks[21], (L, N_EXPERTS, D_MODEL, D_EXPERT), D_MODEL),
        'w_up': nrm(ks[22], (L, N_EXPERTS, D_MODEL, D_EXPERT), D_MODEL),
        'w_down': nrm(ks[23], (L, N_EXPERTS, D_EXPERT, D_MODEL), D_EXPERT),
    }


def reference(x, c, positions, ada_w, ada_b, norm1_g, w_in, mla_qa_g, mla_wq_up, mla_kva_g, mla_wkv_up, mla_qn_g, mla_kn_g, hg_lb_logits, hg_norm_g, w_out, norm2_g, router_group_w, router_group_b, router_expert_w, router_expert_b, w_gate, w_up, w_down):
    lower_bounds = jnp.cumsum(jax.nn.softmax(hg_lb_logits.astype(jnp.float32), axis=1), axis=1)
    split_points = [int(p) for p in np.cumsum(IN_SPLITS)[:-1]]
    c_act = jax.nn.silu(c)
    for layer in range(DEPTH):
        mod = c_act @ ada_w[layer] + ada_b[layer]
        shift1, scale1, gate1, shift2, scale2, gate2 = jnp.split(mod, 6, axis=-1)
        h = modulate(rms_norm(x, norm1_g[layer]), shift1, scale1)
        proj = h @ w_in[layer]
        q_lat, kv_lat, k_rope, hg_q, hg_f_fwd, hg_f_bwd, hg_i, hg_g = jnp.split(proj, split_points, axis=-1)
        attn_out = mla_mixer(q_lat, kv_lat, k_rope, positions, mla_qa_g[layer], mla_wq_up[layer], mla_kva_g[layer], mla_wkv_up[layer], mla_qn_g[layer], mla_kn_g[layer])
        rec_out = hgrn2_mixer(hg_q, hg_f_fwd, hg_f_bwd, hg_i, hg_g, lower_bounds[:, layer], hg_norm_g[layer])
        mixed = jnp.concatenate([attn_out, rec_out], axis=-1) @ w_out[layer]
        x = x + gate1[:, None, :] * mixed
        h2 = modulate(rms_norm(x, norm2_g[layer]), shift2, scale2)
        moe_out = hierarchical_moe(h2, router_group_w[layer], router_group_b[layer], router_expert_w[layer], router_expert_b[layer], w_gate[layer], w_up[layer], w_down[layer])
        x = x + gate2[:, None, :] * moe_out
    return x
```

```python
import functools
import math

import numpy as np
import jax
import jax.numpy as jnp
from jax import lax
from jax.experimental import pallas as pl
from jax.experimental.pallas import tpu as pltpu

F32 = jnp.float32
BF16 = jnp.bfloat16
I32 = jnp.int32

D_MODEL = 1024
MLA_HEADS = 8
MLA_NOPE = 64
MLA_ROPE = 32
MLA_QK = MLA_NOPE + MLA_ROPE
MLA_V = 64
MLA_Q_LORA = 384
MLA_KV_LORA = 256
ROPE_THETA = 10000.0
HG_HEADS = 4
HG_DIM = 128
HG_WIDTH = HG_HEADS * HG_DIM
HG_CHUNK = 64
N_GROUPS = 4
EPG = 8
N_EXPERTS = N_GROUPS * EPG
TOP_K = 2
D_EXPERT = 512
EPS = 1e-6

LANES = 128
VMEM_LIMIT = 56 * 1024 * 1024

SLOT_W = MLA_HEADS * LANES
V_W = MLA_HEADS * MLA_V
HG_SEGS = 5
W_IN_COLS = MLA_Q_LORA + MLA_KV_LORA + LANES + HG_SEGS * HG_WIDTH
ROUTE_ROWS = 128
EXPERT_TILE = 256


def _cparams(sem, vmem=VMEM_LIMIT):
    return pltpu.CompilerParams(dimension_semantics=sem, vmem_limit_bytes=vmem)


def _rms(x):
    return x * lax.rsqrt(jnp.mean(x * x, axis=-1, keepdims=True) + EPS)


def _silu(x):
    return x * jax.nn.sigmoid(x)


def _dot(a, b):
    return jnp.dot(a, b, preferred_element_type=F32)


def _dot_nt(a, b):
    return lax.dot_general(a, b, (((1,), (1,)), ((), ())), preferred_element_type=F32)


def _dot_tn(a, b):
    return lax.dot_general(a, b, (((0,), (0,)), ((), ())), preferred_element_type=F32)


def _mod_kernel(c_ref, w_ref, b_ref, o_ref):
    a = _silu(c_ref[...]).astype(BF16)
    o_ref[...] = _dot(a, w_ref[...].astype(BF16)) + b_ref[...]


def _mod_call(c, ada_w, ada_b):
    B = c.shape[0]
    n = ada_w.shape[1] // D_MODEL
    return pl.pallas_call(
        _mod_kernel,
        grid=(n,),
        in_specs=[
            pl.BlockSpec((B, D_MODEL), lambda j: (0, 0)),
            pl.BlockSpec((D_MODEL, D_MODEL), lambda j: (0, j)),
            pl.BlockSpec((1, D_MODEL), lambda j: (0, j)),
        ],
        out_specs=pl.BlockSpec((B, D_MODEL), lambda j: (0, j)),
        out_shape=jax.ShapeDtypeStruct((B, n * D_MODEL), F32),
        compiler_params=_cparams(("arbitrary",)),
        name="mod",
    )(c, ada_w, ada_b)


def _rope_kernel(pos_ref, c_ref, s_ref):
    half = MLA_ROPE // 2
    tm = pos_ref.shape[2]
    idx = lax.broadcasted_iota(I32, (half, tm), 0).astype(F32)
    inv_freq = ROPE_THETA ** (-idx / half)
    ang = pos_ref[0].astype(F32) * inv_freq
    c = jnp.cos(ang)
    s = jnp.sin(ang)
    ct = jnp.concatenate([jnp.ones((MLA_NOPE, tm), F32), c, c, jnp.zeros((MLA_ROPE, tm), F32)], axis=0)
    st = jnp.concatenate([jnp.zeros((MLA_QK, tm), F32), s, s], axis=0)
    c_ref[0] = ct.T
    s_ref[0] = st.T


def _rope_call(positions, tm):
    B, S = positions.shape
    pos3 = positions.reshape(B, 1, S)
    out = jax.ShapeDtypeStruct((B, S, LANES), F32)
    return pl.pallas_call(
        _rope_kernel,
        grid=(B, S // tm),
        in_specs=[pl.BlockSpec((1, 1, tm), lambda b, i: (b, 0, i))],
        out_specs=[pl.BlockSpec((1, tm, LANES), lambda b, i: (b, i, 0))] * 2,
        out_shape=[out, out],
        compiler_params=_cparams(("arbitrary", "arbitrary")),
        name="rope",
    )(pos3)


def _head_finish(t, g_row, ct, st, lane_valid):
    ss = jnp.sum(jnp.where(lane_valid, t * t, 0.0), axis=-1, keepdims=True) * (1.0 / MLA_QK)
    tg = t * lax.rsqrt(ss + EPS) * g_row
    return tg * ct + pltpu.roll(tg * st, LANES - MLA_ROPE, 1)


def _proj_kernel(x_ref, mod_ref, g1_ref, win_ref, qag_ref, wq_ref, kvag_ref, wkv_ref, gq_ref, gk_ref,
                 ct_ref, st_ref, q_out, k_out, vt_out, hg_out):
    x = x_ref[0]
    mod = mod_ref[0]
    h = _rms(x) * g1_ref[...]
    h = h * (1.0 + mod[1:2]) + mod[0:1]
    hb = h.astype(BF16)

    ct = ct_ref[0]
    st = st_ref[0]
    tm = x.shape[0]
    lane_valid = lax.broadcasted_iota(I32, (tm, LANES), 1) < MLA_QK

    c0 = MLA_Q_LORA
    c1 = c0 + MLA_KV_LORA
    c2 = c1 + LANES

    q_lat = _dot(hb, win_ref[:, 0:c0])
    qn = (_rms(q_lat) * qag_ref[...]).astype(BF16)
    qe = _dot(qn, wq_ref[...])
    for hd in range(MLA_HEADS):
        sl = slice(hd * LANES, (hd + 1) * LANES)
        q_out[0, :, sl] = _head_finish(qe[:, sl], gq_ref[:, sl], ct, st, lane_valid).astype(BF16)

    kv_lat = _dot(hb, win_ref[:, c0:c1])
    kvn = (_rms(kv_lat) * kvag_ref[...]).astype(BF16)
    ke = _dot(kvn, wkv_ref[:, 0:SLOT_W])
    kr = _dot(hb, win_ref[:, c1:c2])
    for hd in range(MLA_HEADS):
        sl = slice(hd * LANES, (hd + 1) * LANES)
        k_out[0, :, sl] = _head_finish(ke[:, sl] + kr, gk_ref[:, sl], ct, st, lane_valid).astype(BF16)

    v = _dot(kvn, wkv_ref[:, SLOT_W:SLOT_W + V_W])
    vt_out[0] = v.T.astype(BF16)

    for sg in range(HG_SEGS):
        lo = c2 + sg * HG_WIDTH
        hg_out[0, :, sg * HG_WIDTH:(sg + 1) * HG_WIDTH] = _dot(hb, win_ref[:, lo:lo + HG_WIDTH]).astype(BF16)


def _proj_call(x, mod3, g1, win_p, qag, wq_ext, kvag, wkv_ext, gq, gk, ctab, stab, tm):
    B, S, D = x.shape
    nt = S // tm
    full = lambda shape: pl.BlockSpec(shape, lambda b, i: (0,) * len(shape))
    return pl.pallas_call(
        _proj_kernel,
        grid=(B, nt),
        in_specs=[
            pl.BlockSpec((1, tm, D), lambda b, i: (b, i, 0)),
            pl.BlockSpec((1, 6, D), lambda b, i: (b, 0, 0)),
            full((1, D)),
            full((D, W_IN_COLS)),
            full((1, MLA_Q_LORA)),
            full((MLA_Q_LORA, SLOT_W)),
            full((1, MLA_KV_LORA)),
            full((MLA_KV_LORA, SLOT_W + V_W)),
            full((1, SLOT_W)),
            full((1, SLOT_W)),
            pl.BlockSpec((1, tm, LANES), lambda b, i: (b, i, 0)),
            pl.BlockSpec((1, tm, LANES), lambda b, i: (b, i, 0)),
        ],
        out_specs=[
            pl.BlockSpec((1, tm, SLOT_W), lambda b, i: (b, i, 0)),
            pl.BlockSpec((1, tm, SLOT_W), lambda b, i: (b, i, 0)),
            pl.BlockSpec((1, V_W, tm), lambda b, i: (b, 0, i)),
            pl.BlockSpec((1, tm, HG_SEGS * HG_WIDTH), lambda b, i: (b, i, 0)),
        ],
        out_shape=[
            jax.ShapeDtypeStruct((B, S, SLOT_W), BF16),
            jax.ShapeDtypeStruct((B, S, SLOT_W), BF16),
            jax.ShapeDtypeStruct((B, V_W, S), BF16),
            jax.ShapeDtypeStruct((B, S, HG_SEGS * HG_WIDTH), BF16),
        ],
        compiler_params=_cparams(("arbitrary", "arbitrary")),
        name="proj",
    )(x, mod3, g1, win_p, qag, wq_ext, kvag, wkv_ext, gq, gk, ctab, stab)


ONES_ROWS = 16


def _attn_kernel(q_ref, k_ref, vt_ref, o_ref):
    S = k_ref.shape[1]
    ones = jnp.ones((ONES_ROWS, S), BF16)
    outs = []
    for hd in range(MLA_HEADS):
        sl = slice(hd * LANES, (hd + 1) * LANES)
        s_t = _dot_nt(k_ref[0, :, sl], q_ref[0, :, sl])
        m = jnp.max(s_t, axis=0, keepdims=True)
        p = jnp.exp2(s_t - m).astype(BF16)
        vt = jnp.concatenate([vt_ref[0, hd * MLA_V:(hd + 1) * MLA_V, :], ones], axis=0)
        o_t = _dot(vt, p)
        outs.append(o_t[:MLA_V] / o_t[MLA_V:MLA_V + 1])
    o_ref[0] = jnp.concatenate(outs, axis=0).T.astype(BF16)


def _attn_call(q_slot, k_slot, vt, tq):
    B, S, _ = q_slot.shape
    return pl.pallas_call(
        _attn_kernel,
        grid=(B, S // tq),
        in_specs=[
            pl.BlockSpec((1, tq, SLOT_W), lambda b, i: (b, i, 0)),
            pl.BlockSpec((1, S, SLOT_W), lambda b, i: (b, 0, 0)),
            pl.BlockSpec((1, V_W, S), lambda b, i: (b, 0, 0)),
        ],
        out_specs=pl.BlockSpec((1, tq, V_W), lambda b, i: (b, i, 0)),
        out_shape=jax.ShapeDtypeStruct((B, S, V_W), BF16),
        compiler_params=_cparams(("arbitrary", "arbitrary")),
        name="attn",
    )(q_slot, k_slot, vt)


def _hgrn_kernel(hq_ref, ff_ref, fb_ref, hi_ref, hgate_ref, lbl_ref, ng_ref, o_ref, of_scr, ob_scr, st_scr, *, hps):
    S = hq_ref.shape[1]
    C = HG_CHUNK
    n_chunks = S // C

    def lower_bound(l0, l1):
        m = jnp.maximum(l0, l1)
        e0 = jnp.exp(l0 - m)
        return e0 / (e0 + jnp.exp(l1 - m))

    lb_f = lower_bound(lbl_ref[0:1, :], lbl_ref[1:2, :])
    lb_b = lower_bound(lbl_ref[2:3, :], lbl_ref[3:4, :])

    rr = lax.broadcasted_iota(I32, (C, C), 0)
    cc = lax.broadcasted_iota(I32, (C, C), 1)
    mask_f = cc <= rr
    mask_b = cc >= rr
    tri_f = jnp.where(mask_f, 1.0, 0.0).astype(BF16)
    tri_b = jnp.where(mask_b, 1.0, 0.0).astype(BF16)

    st_scr[...] = jnp.zeros_like(st_scr)

    def chunk_step(d, row0, j, lb, f_ref, o_scr, mask, tri, last):
        ls = slice(j * HG_DIM, (j + 1) * HG_DIM)
        rows = pl.ds(row0, C)
        lbj = lb[:, ls]
        f = lbj + (1.0 - lbj) * jax.nn.sigmoid(f_ref[0, rows, ls].astype(F32))
        k = 1.0 - f
        lf = jnp.log(f)
        hi = lf.astype(BF16)
        lo = (lf - hi.astype(F32)).astype(BF16)
        bc = _dot(tri, hi) + _dot(tri, lo)
        eb = jnp.exp(bc)
        q = _silu(hq_ref[0, rows, ls].astype(F32))
        v = hi_ref[0, rows, ls]
        qd = (q * eb).astype(BF16)
        kd = k * jnp.exp(-bc)
        att = jnp.where(mask, _dot_nt(qd, kd.astype(BF16)), 0.0)
        st_t = st_scr[d * hps + j]
        o_scr[rows, ls] = _dot(att.astype(BF16), v) + _dot_nt(qd, st_t.astype(BF16))
        e_last = eb[last:last + 1, :]
        k_rem = (kd * e_last).astype(BF16)
        st_scr[d * hps + j] = st_t * e_last + _dot_tn(v, k_rem)

    def body(n, carry):
        row_f = pl.multiple_of(n * C, C)
        row_b = pl.multiple_of((n_chunks - 1 - n) * C, C)
        for j in range(hps):
            chunk_step(0, row_f, j, lb_f, ff_ref, of_scr, mask_f, tri_f, C - 1)
            chunk_step(1, row_b, j, lb_b, fb_ref, ob_scr, mask_b, tri_b, 0)
        return carry

    lax.fori_loop(0, n_chunks, body, 0)

    for j in range(hps):
        ls = slice(j * HG_DIM, (j + 1) * HG_DIM)
        o = of_scr[:, ls] + ob_scr[:, ls]
        gate = _silu(hgate_ref[0, :, ls].astype(F32))
        o_ref[0, :, ls] = (_rms(o) * ng_ref[...] * gate).astype(BF16)


def _hgrn_call(hg, lb_logits, norm_g, hps):
    B, S, _ = hg.shape
    W = hps * HG_DIM
    nh = HG_HEADS // hps
    seg = lambda sg: pl.BlockSpec((1, S, W), lambda b, j, sg=sg: (b, 0, sg * nh + j))
    return pl.pallas_call(
        functools.partial(_hgrn_kernel, hps=hps),
        grid=(B, nh),
        in_specs=[
            seg(0), seg(1), seg(2), seg(3), seg(4),
            pl.BlockSpec((4, W), lambda b, j: (0, j)),
            pl.BlockSpec((1, HG_DIM), lambda b, j: (0, 0)),
        ],
        out_specs=pl.BlockSpec((1, S, W), lambda b, j: (b, 0, j)),
        out_shape=jax.ShapeDtypeStruct((B, S, HG_WIDTH), BF16),
        scratch_shapes=[
            pltpu.VMEM((S, W), F32),
            pltpu.VMEM((S, W), F32),
            pltpu.VMEM((2 * hps, HG_DIM, HG_DIM), F32),
        ],
        compiler_params=_cparams(("arbitrary", "arbitrary")),
        name="hgrn",
    )(hg, hg, hg, hg, hg, lb_logits, norm_g)


def _router_kernel(attn_ref, rec_ref, x_ref, mod_ref, woa_ref, wor_ref, g2_ref, wr_ref, br_ref,
                   x1_ref, h2_ref, info_ref, wts_ref, cnt_ref, run_scr):
    i = pl.program_id(0)

    @pl.when(i == 0)
    def _():
        run_scr[...] = jnp.zeros_like(run_scr)

    mod = mod_ref[0]
    mixed = _dot(attn_ref[...], woa_ref[...]) + _dot(rec_ref[...], wor_ref[...])
    x1 = x_ref[...] + mod[2:3] * mixed
    x1_ref[...] = x1
    h2 = _rms(x1) * g2_ref[...]
    h2 = h2 * (1.0 + mod[4:5]) + mod[3:4]
    h2_ref[...] = h2

    logits = _dot(h2.astype(BF16), wr_ref[...]) + br_ref[...]
    lt = logits.T
    tm = lt.shape[1]
    r = lax.broadcasted_iota(I32, (ROUTE_ROWS, tm), 0)
    neg = -jnp.inf
    big = ROUTE_ROWS

    isg = r < N_GROUPS
    gmax = jnp.max(jnp.where(isg, lt, neg), axis=0, keepdims=True)
    gidx = jnp.min(jnp.where(isg & (lt == gmax), r, big), axis=0, keepdims=True)
    gp = 1.0 / jnp.sum(jnp.where(isg, jnp.exp(lt - gmax), 0.0), axis=0, keepdims=True)

    ing = (r >= N_GROUPS) & (r < N_GROUPS + N_EXPERTS) & (((r - N_GROUPS) >> 3) == gidx)
    v1 = jnp.max(jnp.where(ing, lt, neg), axis=0, keepdims=True)
    i1 = jnp.min(jnp.where(ing & (lt == v1), r, big), axis=0, keepdims=True)
    ing2 = ing & (r != i1)
    v2 = jnp.max(jnp.where(ing2, lt, neg), axis=0, keepdims=True)
    i2 = jnp.min(jnp.where(ing2 & (lt == v2), r, big), axis=0, keepdims=True)
    e21 = jnp.exp(v2 - v1)
    w1 = gp / (1.0 + e21)
    w2 = gp * e21 / (1.0 + e21)

    oh1 = r == i1
    oh2 = r == i2
    oh = jnp.where(oh1 | oh2, 1.0, 0.0)
    tt = lax.broadcasted_iota(I32, (tm, tm), 0)
    tc = lax.broadcasted_iota(I32, (tm, tm), 1)
    upper = jnp.where(tt < tc, 1.0, 0.0).astype(BF16)
    before = _dot(oh.astype(BF16), upper) + run_scr[...]
    rank1 = jnp.sum(jnp.where(oh1, before, 0.0), axis=0, keepdims=True)
    rank2 = jnp.sum(jnp.where(oh2, before, 0.0), axis=0, keepdims=True)
    run_scr[...] = run_scr[...] + jnp.sum(oh, axis=1, keepdims=True)
    cnt_ref[...] = run_scr[...]

    zero = jnp.zeros((1, tm), I32)
    info_ref[0] = jnp.concatenate(
        [i1 - N_GROUPS, i2 - N_GROUPS, rank1.astype(I32), rank2.astype(I32), zero, zero, zero, zero], axis=0)
    wfull = jnp.concatenate([jnp.broadcast_to(w1, (LANES, tm)), jnp.broadcast_to(w2, (LANES, tm))], axis=0)
    wts_ref[...] = wfull.T


def _router_call(attn, rec, x2, mod3, wo_a, wo_r, g2, wr, br, tm, tiles_per_batch):
    T, D = x2.shape
    nt = T // tm
    full = lambda shape: pl.BlockSpec(shape, lambda i: (0,) * len(shape))
    return pl.pallas_call(
        _router_kernel,
        grid=(nt,),
        in_specs=[
            pl.BlockSpec((tm, V_W), lambda i: (i, 0)),
            pl.BlockSpec((tm, HG_WIDTH), lambda i: (i, 0)),
            pl.BlockSpec((tm, D), lambda i: (i, 0)),
            pl.BlockSpec((1, 6, D), lambda i: (i // tiles_per_batch, 0, 0)),
            full((V_W, D)),
            full((HG_WIDTH, D)),
            full((1, D)),
            full((D, ROUTE_ROWS)),
            full((1, ROUTE_ROWS)),
        ],
        out_specs=[
            pl.BlockSpec((tm, D), lambda i: (i, 0)),
            pl.BlockSpec((tm, D), lambda i: (i, 0)),
            pl.BlockSpec((1, 8, tm), lambda i: (i, 0, 0)),
            pl.BlockSpec((tm, 2 * LANES), lambda i: (i, 0)),
            pl.BlockSpec((ROUTE_ROWS, 1), lambda i: (0, 0)),
        ],
        out_shape=[
            jax.ShapeDtypeStruct((T, D), F32),
            jax.ShapeDtypeStruct((T, D), F32),
            jax.ShapeDtypeStruct((nt, 8, tm), I32),
            jax.ShapeDtypeStruct((T, 2 * LANES), F32),
            jax.ShapeDtypeStruct((ROUTE_ROWS, 1), F32),
        ],
        scratch_shapes=[pltpu.VMEM((ROUTE_ROWS, 1), F32)],
        compiler_params=_cparams(("arbitrary",)),
        name="router",
    )(attn, rec, x2, mod3, wo_a, wo_r, g2, wr, br)


def _dispatch_kernel(pstart_ref, info_ref, h2_ref, xb_in_ref, xb_ref, sem):
    del xb_in_ref
    tm = h2_ref.shape[0]

    def row_copy(t, d):
        return pltpu.make_async_copy(h2_ref.at[pl.ds(t, 1)], xb_ref.at[pl.ds(d, 1)], sem)

    def issue(t, carry):
        d1 = pstart_ref[info_ref[0, 0, t]] + info_ref[0, 2, t]
        d2 = pstart_ref[info_ref[0, 1, t]] + info_ref[0, 3, t]
        row_copy(t, d1).start()
        row_copy(t, d2).start()
        return carry

    lax.fori_loop(0, tm, issue, 0)

    def drain(t, carry):
        row_copy(0, 0).wait()
        row_copy(0, 0).wait()
        return carry

    lax.fori_loop(0, tm, drain, 0)


def _dispatch_call(pstart, info, h2, xb_zero, tm):
    T, D = h2.shape
    nt = T // tm
    cap = xb_zero.shape[0]
    grid_spec = pltpu.PrefetchScalarGridSpec(
        num_scalar_prefetch=1,
        grid=(nt,),
        in_specs=[
            pl.BlockSpec((1, 8, tm), lambda i, ps: (i, 0, 0), memory_space=pltpu.SMEM),
            pl.BlockSpec((tm, D), lambda i, ps: (i, 0)),
            pl.BlockSpec(memory_space=pl.ANY),
        ],
        out_specs=pl.BlockSpec(memory_space=pl.ANY),
        scratch_shapes=[pltpu.SemaphoreType.DMA],
    )
    return pl.pallas_call(
        _dispatch_kernel,
        grid_spec=grid_spec,
        out_shape=jax.ShapeDtypeStruct((cap, D), F32),
        input_output_aliases={3: 0},
        compiler_params=_cparams(("arbitrary",)),
        name="dispatch",
    )(pstart, info, h2, xb_zero)


def _expert_kernel(te_ref, tv_ref, xb_ref, wg_ref, wu_ref, wd_ref, y_ref, wg_b, wu_b, wd_b):
    j = pl.program_id(0)
    prev = te_ref[jnp.maximum(j - 1, 0)]

    @pl.when((j == 0) | (te_ref[j] != prev))
    def _():
        wg_b[...] = wg_ref[0].astype(BF16)
        wu_b[...] = wu_ref[0].astype(BF16)
        wd_b[...] = wd_ref[0].astype(BF16)

    @pl.when(tv_ref[j] > 0)
    def _():
        x = xb_ref[...].astype(BF16)
        hid = (_silu(_dot(x, wg_b[...])) * _dot(x, wu_b[...])).astype(BF16)
        y_ref[...] = _dot(hid, wd_b[...])

    @pl.when(tv_ref[j] == 0)
    def _():
        y_ref[...] = jnp.zeros_like(y_ref)


def _expert_call(tile_expert, tile_valid, xb, w_gate, w_up, w_down):
    cap, D = xb.shape
    nt = cap // EXPERT_TILE
    grid_spec = pltpu.PrefetchScalarGridSpec(
        num_scalar_prefetch=2,
        grid=(nt,),
        in_specs=[
            pl.BlockSpec((EXPERT_TILE, D), lambda j, te, tv: (j, 0)),
            pl.BlockSpec((1, D, D_EXPERT), lambda j, te, tv: (te[j], 0, 0)),
            pl.BlockSpec((1, D, D_EXPERT), lambda j, te, tv: (te[j], 0, 0)),
            pl.BlockSpec((1, D_EXPERT, D), lambda j, te, tv: (te[j], 0, 0)),
        ],
        out_specs=pl.BlockSpec((EXPERT_TILE, D), lambda j, te, tv: (j, 0)),
        scratch_shapes=[
            pltpu.VMEM((D, D_EXPERT), BF16),
            pltpu.VMEM((D, D_EXPERT), BF16),
            pltpu.VMEM((D_EXPERT, D), BF16),
        ],
    )
    return pl.pallas_call(
        _expert_kernel,
        grid_spec=grid_spec,
        out_shape=jax.ShapeDtypeStruct((cap, D), F32),
        compiler_params=_cparams(("arbitrary",)),
        name="expert",
    )(tile_expert, tile_valid, xb, w_gate, w_up, w_down)


def _combine_kernel(pstart_ref, info_ref, x1_ref, mod_ref, wts_ref, yb_ref, o_ref, g1_scr, g2_scr, sem):
    tm = x1_ref.shape[0]

    def row_copy(d, dst, t):
        return pltpu.make_async_copy(yb_ref.at[pl.ds(d, 1)], dst.at[pl.ds(t, 1)], sem)

    def issue(t, carry):
        d1 = pstart_ref[info_ref[0, 0, t]] + info_ref[0, 2, t]
        d2 = pstart_ref[info_ref[0, 1, t]] + info_ref[0, 3, t]
        row_copy(d1, g1_scr, t).start()
        row_copy(d2, g2_scr, t).start()
        return carry

    lax.fori_loop(0, tm, issue, 0)

    def drain(t, carry):
        row_copy(0, g1_scr, 0).wait()
        row_copy(0, g2_scr, 0).wait()
        return carry

    lax.fori_loop(0, tm, drain, 0)

    w1 = wts_ref[:, 0:1]
    w2 = wts_ref[:, LANES:LANES + 1]
    moe = w1 * g1_scr[...] + w2 * g2_scr[...]
    o_ref[...] = x1_ref[...] + mod_ref[0][5:6] * moe


def _combine_call(pstart, info, x1, mod3, wts, yb, tm, tiles_per_batch):
    T, D = x1.shape
    nt = T // tm
    grid_spec = pltpu.PrefetchScalarGridSpec(
        num_scalar_prefetch=1,
        grid=(nt,),
        in_specs=[
            pl.BlockSpec((1, 8, tm), lambda i, ps: (i, 0, 0), memory_space=pltpu.SMEM),
            pl.BlockSpec((tm, D), lambda i, ps: (i, 0)),
            pl.BlockSpec((1, 6, D), lambda i, ps: (i // tiles_per_batch, 0, 0)),
            pl.BlockSpec((tm, 2 * LANES), lambda i, ps: (i, 0)),
            pl.BlockSpec(memory_space=pl.ANY),
        ],
        out_specs=pl.BlockSpec((tm, D), lambda i, ps: (i, 0)),
        scratch_shapes=[
            pltpu.VMEM((tm, D), F32),
            pltpu.VMEM((tm, D), F32),
            pltpu.SemaphoreType.DMA,
        ],
    )
    return pl.pallas_call(
        _combine_kernel,
        grid_spec=grid_spec,
        out_shape=jax.ShapeDtypeStruct((T, D), F32),
        compiler_params=_cparams(("arbitrary",)),
        name="combine",
    )(pstart, info, x1, mod3, wts, yb)


def _rotate_half_matrix():
    half = MLA_ROPE // 2
    p = np.zeros((MLA_ROPE, MLA_ROPE), np.float32)
    for j in range(half):
        p[j + half, j] = -1.0
        p[j, j + half] = 1.0
    return jnp.asarray(p)


def _slot_gain(g, factor):
    half = MLA_ROPE // 2
    g_n, g_r = g[:MLA_NOPE], g[MLA_NOPE:]
    slot = jnp.concatenate([g_n, g_r, g_r[half:], g_r[:half]]) * factor
    return jnp.tile(slot, MLA_HEADS)[None, :].astype(F32)


def _prepare_weights(w_in, wq_up, wkv_up, qn_g, kn_g):
    rot = _rotate_half_matrix()
    s0 = MLA_Q_LORA
    s1 = s0 + MLA_KV_LORA
    s2 = s1 + MLA_ROPE
    w_kr = w_in[:, s1:s2]
    kr_slot = jnp.concatenate([jnp.zeros((D_MODEL, MLA_NOPE), F32), w_kr, w_kr @ rot], axis=1)
    win_p = jnp.concatenate([w_in[:, :s1], kr_slot, w_in[:, s2:]], axis=1).astype(BF16)

    wq = wq_up.reshape(MLA_Q_LORA, MLA_HEADS, MLA_QK)
    wq_r = wq[:, :, MLA_NOPE:]
    wq_ext = jnp.concatenate([wq, jnp.einsum('lhr,rs->lhs', wq_r, rot)], axis=2)
    wq_ext = wq_ext.reshape(MLA_Q_LORA, SLOT_W).astype(BF16)

    wkv = wkv_up.reshape(MLA_KV_LORA, MLA_HEADS, MLA_NOPE + MLA_V)
    wk_slot = jnp.concatenate([wkv[:, :, :MLA_NOPE], jnp.zeros((MLA_KV_LORA, MLA_HEADS, LANES - MLA_NOPE), F32)],
                              axis=2).reshape(MLA_KV_LORA, SLOT_W)
    wv = wkv[:, :, MLA_NOPE:].reshape(MLA_KV_LORA, V_W)
    wkv_ext = jnp.concatenate([wk_slot, wv], axis=1).astype(BF16)

    gq = _slot_gain(qn_g, MLA_QK ** -0.5 * math.log2(math.e))
    gk = _slot_gain(kn_g, 1.0)
    return win_p, wq_ext, wkv_ext, gq, gk


def _pick(n, pref):
    return pref if n % pref == 0 else n


def kernel(x, c, positions, ada_w, ada_b, norm1_g, w_in, mla_qa_g, mla_wq_up, mla_kva_g, mla_wkv_up, mla_qn_g, mla_kn_g, hg_lb_logits, hg_norm_g, w_out, norm2_g, router_group_w, router_group_b, router_expert_w, router_expert_b, w_gate, w_up, w_down):
    B, S, D = x.shape
    assert D == D_MODEL and ada_w.shape[0] == 1, "one layer of width 1024"
    T = B * S
    tm = _pick(S, 512)
    tq = _pick(S, 256)
    tiles_per_batch = S // tm

    mod3 = _mod_call(c, ada_w[0], ada_b).reshape(B, 6, D)
    ctab, stab = _rope_call(positions, tm)

    win_p, wq_ext, wkv_ext, gq, gk = _prepare_weights(w_in[0], mla_wq_up[0], mla_wkv_up[0], mla_qn_g[0], mla_kn_g[0])
    q_slot, k_slot, vt, hg = _proj_call(x, mod3, norm1_g, win_p, mla_qa_g, wq_ext, mla_kva_g, wkv_ext, gq, gk,
                                        ctab, stab, tm)
    attn = _attn_call(q_slot, k_slot, vt, tq)
    rec = _hgrn_call(hg, hg_lb_logits.reshape(2 * hg_lb_logits.shape[1], HG_WIDTH), hg_norm_g, hps=2)

    wo = w_out[0].astype(BF16)
    wr = jnp.concatenate([router_group_w[0], router_expert_w[0],
                          jnp.zeros((D, ROUTE_ROWS - N_GROUPS - N_EXPERTS), F32)], axis=1).astype(BF16)
    br = jnp.concatenate([router_group_b[0], router_expert_b[0],
                          jnp.zeros((ROUTE_ROWS - N_GROUPS - N_EXPERTS,), F32)])[None, :]
    x1, h2, info, wts, cnt = _router_call(attn.reshape(T, V_W), rec.reshape(T, HG_WIDTH), x.reshape(T, D), mod3,
                                          wo[:V_W], wo[V_W:], norm2_g, wr, br, tm, tiles_per_batch)

    counts = cnt[N_GROUPS:N_GROUPS + N_EXPERTS, 0].astype(I32)
    padded = (counts + EXPERT_TILE - 1) // EXPERT_TILE * EXPERT_TILE
    pend = jnp.cumsum(padded)
    pstart = (pend - padded).astype(I32)
    n_tiles = (T * TOP_K) // EXPERT_TILE + N_EXPERTS
    tile_start = jnp.arange(n_tiles, dtype=I32) * EXPERT_TILE
    tile_expert = jnp.clip(jnp.searchsorted(pend, tile_start, side='right'), 0, N_EXPERTS - 1).astype(I32)
    tile_valid = (tile_start < pend[-1]).astype(I32)

    xb = _dispatch_call(pstart, info, h2, jnp.zeros((n_tiles * EXPERT_TILE, D), F32), tm)
    yb = _expert_call(tile_expert, tile_valid, xb, w_gate[0], w_up[0], w_down[0])
    out = _combine_call(pstart, info, x1, mod3, wts, yb, tm, tiles_per_batch)
    return out.reshape(B, S, D)
```

```python
import functools
import math

import numpy as np
import jax
import jax.numpy as jnp
from jax import lax
from jax.experimental import pallas as pl
from jax.experimental.pallas import tpu as pltpu

F32 = jnp.float32
BF16 = jnp.bfloat16
I32 = jnp.int32

D_MODEL = 1024
MLA_HEADS = 8
MLA_NOPE = 64
MLA_ROPE = 32
MLA_QK = MLA_NOPE + MLA_ROPE
MLA_V = 64
MLA_Q_LORA = 384
MLA_KV_LORA = 256
ROPE_THETA = 10000.0
HG_HEADS = 4
HG_DIM = 128
HG_WIDTH = HG_HEADS * HG_DIM
HG_CHUNK = 64
N_GROUPS = 4
EPG = 8
N_EXPERTS = N_GROUPS * EPG
TOP_K = 2
D_EXPERT = 512
EPS = 1e-6

LANES = 128
VMEM_LIMIT = 56 * 1024 * 1024

SLOT_W = MLA_HEADS * LANES
V_W = MLA_HEADS * MLA_V
HG_SEGS = 5
W_IN_COLS = MLA_Q_LORA + MLA_KV_LORA + LANES + HG_SEGS * HG_WIDTH
HG_GROUP_ROWS = 4 * HG_CHUNK
ONES_ROWS = 16
VT_ROWS = MLA_V + ONES_ROWS
ROUTE_ROWS = 128
EXPERT_TILE = 256


def _cparams(sem, vmem=VMEM_LIMIT):
    return pltpu.CompilerParams(dimension_semantics=sem, vmem_limit_bytes=vmem)


def _rms(x):
    return x * lax.rsqrt(jnp.mean(x * x, axis=-1, keepdims=True) + EPS)


def _silu(x):
    return x * jax.nn.sigmoid(x)


def _dot(a, b):
    return jnp.dot(a, b, preferred_element_type=F32)


def _dot_nt(a, b):
    return lax.dot_general(a, b, (((1,), (1,)), ((), ())), preferred_element_type=F32)


def _dot_tn(a, b):
    return lax.dot_general(a, b, (((0,), (0,)), ((), ())), preferred_element_type=F32)


def _mod_kernel(c_ref, w_ref, b_ref, o_ref):
    a = _silu(c_ref[...]).astype(BF16)
    o_ref[...] = _dot(a, w_ref[...].astype(BF16)) + b_ref[...]


def _mod_call(c, ada_w, ada_b):
    B = c.shape[0]
    n = ada_w.shape[1] // D_MODEL
    return pl.pallas_call(
        _mod_kernel,
        grid=(n,),
        in_specs=[
            pl.BlockSpec((B, D_MODEL), lambda j: (0, 0)),
            pl.BlockSpec((D_MODEL, D_MODEL), lambda j: (0, j)),
            pl.BlockSpec((1, D_MODEL), lambda j: (0, j)),
        ],
        out_specs=pl.BlockSpec((B, D_MODEL), lambda j: (0, j)),
        out_shape=jax.ShapeDtypeStruct((B, n * D_MODEL), F32),
        compiler_params=_cparams(("arbitrary",)),
        name="mod",
    )(c, ada_w, ada_b)


def _rope_kernel(pos_ref, c_ref, s_ref):
    half = MLA_ROPE // 2
    tm = pos_ref.shape[2]
    idx = lax.broadcasted_iota(I32, (half, tm), 0).astype(F32)
    inv_freq = ROPE_THETA ** (-idx / half)
    ang = pos_ref[0].astype(F32) * inv_freq
    c = jnp.cos(ang)
    s = jnp.sin(ang)
    ct = jnp.concatenate([jnp.ones((MLA_NOPE, tm), F32), c, c, jnp.zeros((MLA_ROPE, tm), F32)], axis=0)
    st = jnp.concatenate([jnp.zeros((MLA_QK, tm), F32), s, s], axis=0)
    c_ref[0] = ct.T
    s_ref[0] = st.T


def _rope_call(positions, tm):
    B, S = positions.shape
    pos3 = positions.reshape(B, 1, S)
    out = jax.ShapeDtypeStruct((B, S, LANES), F32)
    return pl.pallas_call(
        _rope_kernel,
        grid=(B, S // tm),
        in_specs=[pl.BlockSpec((1, 1, tm), lambda b, i: (b, 0, i))],
        out_specs=[pl.BlockSpec((1, tm, LANES), lambda b, i: (b, i, 0))] * 2,
        out_shape=[out, out],
        compiler_params=_cparams(("arbitrary", "arbitrary")),
        name="rope",
    )(pos3)


def _head_finish(t, g_row, ct, st, lane_valid):
    ss = jnp.sum(jnp.where(lane_valid, t * t, 0.0), axis=-1, keepdims=True) * (1.0 / MLA_QK)
    tg = t * lax.rsqrt(ss + EPS) * g_row
    return tg * ct + pltpu.roll(tg * st, LANES - MLA_ROPE, 1)


def _proj_kernel(x_ref, mod_ref, g1_ref, win_ref, qag_ref, wq_ref, kvag_ref, wkv_ref, gq_ref, gk_ref,
                 ct_ref, st_ref, q_out, k_out, vt_out, hg_out):
    x = x_ref[0]
    mod = mod_ref[0]
    h = _rms(x) * g1_ref[...]
    h = h * (1.0 + mod[1:2]) + mod[0:1]
    hb = h.astype(BF16)

    ct = ct_ref[0]
    st = st_ref[0]
    tm = x.shape[0]
    lane_valid = lax.broadcasted_iota(I32, (tm, LANES), 1) < MLA_QK

    c0 = MLA_Q_LORA
    c1 = c0 + MLA_KV_LORA
    c2 = c1 + LANES

    q_lat = _dot(hb, win_ref[:, 0:c0])
    qn = (_rms(q_lat) * qag_ref[...]).astype(BF16)
    qe = _dot(qn, wq_ref[...])
    for hd in range(MLA_HEADS):
        sl = slice(hd * LANES, (hd + 1) * LANES)
        q_out[0, hd] = _head_finish(qe[:, sl], gq_ref[:, sl], ct, st, lane_valid).astype(BF16)

    kv_lat = _dot(hb, win_ref[:, c0:c1])
    kvn = (_rms(kv_lat) * kvag_ref[...]).astype(BF16)
    ke = _dot(kvn, wkv_ref[:, 0:SLOT_W])
    kr = _dot(hb, win_ref[:, c1:c2])
    for hd in range(MLA_HEADS):
        sl = slice(hd * LANES, (hd + 1) * LANES)
        k_out[0, hd] = _head_finish(ke[:, sl] + kr, gk_ref[:, sl], ct, st, lane_valid).astype(BF16)

    v_t = _dot(kvn, wkv_ref[:, SLOT_W:SLOT_W + V_W]).T.astype(BF16)
    ones = jnp.ones((ONES_ROWS, tm), BF16)
    for hd in range(MLA_HEADS):
        vt_out[0, hd, 0:MLA_V, :] = v_t[hd * MLA_V:(hd + 1) * MLA_V, :]
        vt_out[0, hd, MLA_V:VT_ROWS, :] = ones

    for sg in range(HG_SEGS):
        lo = c2 + sg * HG_WIDTH
        hg_out[0, :, sg * HG_WIDTH:(sg + 1) * HG_WIDTH] = _dot(hb, win_ref[:, lo:lo + HG_WIDTH]).astype(BF16)


def _proj_call(x, mod3, g1, win_p, qag, wq_ext, kvag, wkv_ext, gq, gk, ctab, stab, tm):
    B, S, D = x.shape
    nt = S // tm
    full = lambda shape: pl.BlockSpec(shape, lambda b, i: (0,) * len(shape))
    return pl.pallas_call(
        _proj_kernel,
        grid=(B, nt),
        in_specs=[
            pl.BlockSpec((1, tm, D), lambda b, i: (b, i, 0)),
            pl.BlockSpec((1, 6, D), lambda b, i: (b, 0, 0)),
            full((1, D)),
            full((D, W_IN_COLS)),
            full((1, MLA_Q_LORA)),
            full((MLA_Q_LORA, SLOT_W)),
            full((1, MLA_KV_LORA)),
            full((MLA_KV_LORA, SLOT_W + V_W)),
            full((1, SLOT_W)),
            full((1, SLOT_W)),
            pl.BlockSpec((1, tm, LANES), lambda b, i: (b, i, 0)),
            pl.BlockSpec((1, tm, LANES), lambda b, i: (b, i, 0)),
        ],
        out_specs=[
            pl.BlockSpec((1, MLA_HEADS, tm, LANES), lambda b, i: (b, 0, i, 0)),
            pl.BlockSpec((1, MLA_HEADS, tm, LANES), lambda b, i: (b, 0, i, 0)),
            pl.BlockSpec((1, MLA_HEADS, VT_ROWS, tm), lambda b, i: (b, 0, 0, i)),
            pl.BlockSpec((1, tm, HG_SEGS * HG_WIDTH), lambda b, i: (b, i, 0)),
        ],
        out_shape=[
            jax.ShapeDtypeStruct((B, MLA_HEADS, S, LANES), BF16),
            jax.ShapeDtypeStruct((B, MLA_HEADS, S, LANES), BF16),
            jax.ShapeDtypeStruct((B, MLA_HEADS, VT_ROWS, S), BF16),
            jax.ShapeDtypeStruct((B, S, HG_SEGS * HG_WIDTH), BF16),
        ],
        compiler_params=_cparams(("arbitrary", "arbitrary")),
        name="proj",
    )(x, mod3, g1, win_p, qag, wq_ext, kvag, wkv_ext, gq, gk, ctab, stab)


def _attn_kernel(q_ref, k_ref, vt_ref, o_ref, s_scr, p_scr):
    tq = o_ref.shape[3]
    n_items = o_ref.shape[1] * MLA_HEADS

    def split(i):
        return i // MLA_HEADS, i % MLA_HEADS

    def qk(i, slot):
        qt, hd = split(i)
        q = q_ref[0, hd, pl.ds(pl.multiple_of(qt * tq, tq), tq), :]
        s_scr[slot] = _dot_nt(k_ref[0, hd], q)

    def softmax(slot):
        s_t = s_scr[slot]
        m = jnp.max(s_t, axis=0, keepdims=True)
        p_scr[slot] = jnp.exp2(s_t - m).astype(BF16)

    def pv(i, slot):
        qt, hd = split(i)
        o_t = _dot(vt_ref[0, hd], p_scr[slot])
        o = o_t[:MLA_V] / o_t[MLA_V:MLA_V + 1]
        o_ref[0, qt, pl.ds(pl.multiple_of(hd * MLA_V, MLA_V), MLA_V), :] = o.astype(BF16)

    qk(0, 0)
    qk(1, 1)
    softmax(0)

    def body(jj, carry):
        j = 2 * jj
        qk(j, 0)
        softmax(1)
        pv(j - 2, 0)
        qk(j + 1, 1)
        softmax(0)
        pv(j - 1, 1)
        return carry

    lax.fori_loop(1, n_items // 2, body, 0)
    softmax(1)
    pv(n_items - 2, 0)
    pv(n_items - 1, 1)


def _attn_call(q_hm, k_hm, vt, tq):
    B, _, S, _ = q_hm.shape
    nq = S // tq
    return pl.pallas_call(
        _attn_kernel,
        grid=(B,),
        in_specs=[
            pl.BlockSpec((1, MLA_HEADS, S, LANES), lambda b: (b, 0, 0, 0)),
            pl.BlockSpec((1, MLA_HEADS, S, LANES), lambda b: (b, 0, 0, 0)),
            pl.BlockSpec((1, MLA_HEADS, VT_ROWS, S), lambda b: (b, 0, 0, 0)),
        ],
        out_specs=pl.BlockSpec((1, nq, V_W, tq), lambda b: (b, 0, 0, 0)),
        out_shape=jax.ShapeDtypeStruct((B, nq, V_W, tq), BF16),
        scratch_shapes=[
            pltpu.VMEM((2, S, tq), F32),
            pltpu.VMEM((2, S, tq), BF16),
        ],
        compiler_params=_cparams(("arbitrary",)),
        name="attn",
    )(q_hm, k_hm, vt)


def _hgrn_kernel(hq_ref, ff_ref, fb_ref, hi_ref, hgate_ref, lbl_ref, ng_ref, o_ref, of_scr, ob_scr, st_scr, *, hps):
    S = hq_ref.shape[1]
    C = HG_CHUNK
    G = HG_GROUP_ROWS
    cpg = G // C
    n_groups = S // G

    def lower_bound(l0, l1):
        m = jnp.maximum(l0, l1)
        e0 = jnp.exp(l0 - m)
        return e0 / (e0 + jnp.exp(l1 - m))

    lb_f = lower_bound(lbl_ref[0:1, :], lbl_ref[1:2, :])
    lb_b = lower_bound(lbl_ref[2:3, :], lbl_ref[3:4, :])

    rr = lax.broadcasted_iota(I32, (G, G), 0)
    cc = lax.broadcasted_iota(I32, (G, G), 1)
    same_chunk = (rr // C) == (cc // C)
    mask_f = same_chunk & (cc <= rr)
    mask_b = same_chunk & (cc >= rr)
    tri_f = jnp.where(mask_f, 1.0, 0.0).astype(BF16)
    tri_b = jnp.where(mask_b, 1.0, 0.0).astype(BF16)

    st_scr[...] = jnp.zeros_like(st_scr)

    def group_step(d, row0, j, lb, f_ref, o_scr, mask, tri, fwd):
        ls = slice(j * HG_DIM, (j + 1) * HG_DIM)
        rows = pl.ds(row0, G)
        lbj = lb[:, ls]
        f = lbj + (1.0 - lbj) * jax.nn.sigmoid(f_ref[0, rows, ls].astype(F32))
        k = 1.0 - f
        lf = jnp.log(f)
        hi = lf.astype(BF16)
        lo = (lf - hi.astype(F32)).astype(BF16)
        bc = _dot(tri, hi) + _dot(tri, lo)
        eb = jnp.exp(bc)
        q = _silu(hq_ref[0, rows, ls].astype(F32))
        v = hi_ref[0, rows, ls]
        qd = (q * eb).astype(BF16)
        kd = k * jnp.exp(-bc)
        att = jnp.where(mask, _dot_nt(qd, kd.astype(BF16)), 0.0)
        o_intra = _dot(att.astype(BF16), v)
        st_t = st_scr[d * hps + j]
        o_inter = [None] * cpg
        for c in (range(cpg) if fwd else reversed(range(cpg))):
            cs = slice(c * C, (c + 1) * C)
            o_inter[c] = _dot_nt(qd[cs], st_t.astype(BF16))
            last = (c + 1) * C - 1 if fwd else c * C
            e_last = eb[last:last + 1, :]
            k_rem = (kd[cs] * e_last).astype(BF16)
            st_t = st_t * e_last + _dot_tn(v[cs], k_rem)
        st_scr[d * hps + j] = st_t
        o_scr[rows, ls] = o_intra + jnp.concatenate(o_inter, axis=0)

    def body(n, carry):
        row_f = pl.multiple_of(n * G, G)
        row_b = pl.multiple_of((n_groups - 1 - n) * G, G)
        for j in range(hps):
            group_step(0, row_f, j, lb_f, ff_ref, of_scr, mask_f, tri_f, True)
            group_step(1, row_b, j, lb_b, fb_ref, ob_scr, mask_b, tri_b, False)
        return carry

    lax.fori_loop(0, n_groups, body, 0)

    for j in range(hps):
        ls = slice(j * HG_DIM, (j + 1) * HG_DIM)
        o = of_scr[:, ls] + ob_scr[:, ls]
        gate = _silu(hgate_ref[0, :, ls].astype(F32))
        o_ref[0, :, ls] = (_rms(o) * ng_ref[...] * gate).astype(BF16)


def _hgrn_call(hg, lb_logits, norm_g, hps):
    B, S, _ = hg.shape
    W = hps * HG_DIM
    nh = HG_HEADS // hps
    seg = lambda sg: pl.BlockSpec((1, S, W), lambda b, j, sg=sg: (b, 0, sg * nh + j))
    return pl.pallas_call(
        functools.partial(_hgrn_kernel, hps=hps),
        grid=(B, nh),
        in_specs=[
            seg(0), seg(1), seg(2), seg(3), seg(4),
            pl.BlockSpec((4, W), lambda b, j: (0, j)),
            pl.BlockSpec((1, HG_DIM), lambda b, j: (0, 0)),
        ],
        out_specs=pl.BlockSpec((1, S, W), lambda b, j: (b, 0, j)),
        out_shape=jax.ShapeDtypeStruct((B, S, HG_WIDTH), BF16),
        scratch_shapes=[
            pltpu.VMEM((S, W), F32),
            pltpu.VMEM((S, W), F32),
            pltpu.VMEM((2 * hps, HG_DIM, HG_DIM), F32),
        ],
        compiler_params=_cparams(("arbitrary", "arbitrary")),
        name="hgrn",
    )(hg, hg, hg, hg, hg, lb_logits, norm_g)


def _router_kernel(attn_ref, rec_ref, x_ref, mod_ref, woa_ref, wor_ref, g2_ref, wr_ref, br_ref,
                   x1_ref, h2_ref, info_ref, wts_ref, cnt_ref, run_scr):
    i = pl.program_id(0)

    @pl.when(i == 0)
    def _():
        run_scr[...] = jnp.zeros_like(run_scr)

    mod = mod_ref[0]
    mixed = _dot_tn(attn_ref[0, 0], woa_ref[...]) + _dot(rec_ref[...], wor_ref[...])
    x1 = x_ref[...] + mod[2:3] * mixed
    x1_ref[...] = x1
    h2 = _rms(x1) * g2_ref[...]
    h2 = h2 * (1.0 + mod[4:5]) + mod[3:4]
    h2_ref[...] = h2

    logits = _dot(h2.astype(BF16), wr_ref[...]) + br_ref[...]
    lt = logits.T
    tm = lt.shape[1]
    r = lax.broadcasted_iota(I32, (ROUTE_ROWS, tm), 0)
    neg = -jnp.inf
    big = ROUTE_ROWS

    isg = r < N_GROUPS
    gmax = jnp.max(jnp.where(isg, lt, neg), axis=0, keepdims=True)
    gidx = jnp.min(jnp.where(isg & (lt == gmax), r, big), axis=0, keepdims=True)
    gp = 1.0 / jnp.sum(jnp.where(isg, jnp.exp(lt - gmax), 0.0), axis=0, keepdims=True)

    ing = (r >= N_GROUPS) & (r < N_GROUPS + N_EXPERTS) & (((r - N_GROUPS) >> 3) == gidx)
    v1 = jnp.max(jnp.where(ing, lt, neg), axis=0, keepdims=True)
    i1 = jnp.min(jnp.where(ing & (lt == v1), r, big), axis=0, keepdims=True)
    ing2 = ing & (r != i1)
    v2 = jnp.max(jnp.where(ing2, lt, neg), axis=0, keepdims=True)
    i2 = jnp.min(jnp.where(ing2 & (lt == v2), r, big), axis=0, keepdims=True)
    e21 = jnp.exp(v2 - v1)
    w1 = gp / (1.0 + e21)
    w2 = gp * e21 / (1.0 + e21)

    oh1 = r == i1
    oh2 = r == i2
    oh = jnp.where(oh1 | oh2, 1.0, 0.0)
    tt = lax.broadcasted_iota(I32, (tm, tm), 0)
    tc = lax.broadcasted_iota(I32, (tm, tm), 1)
    upper = jnp.where(tt < tc, 1.0, 0.0).astype(BF16)
    before = _dot(oh.astype(BF16), upper) + run_scr[...]
    rank1 = jnp.sum(jnp.where(oh1, before, 0.0), axis=0, keepdims=True)
    rank2 = jnp.sum(jnp.where(oh2, before, 0.0), axis=0, keepdims=True)
    run_scr[...] = run_scr[...] + jnp.sum(oh, axis=1, keepdims=True)
    cnt_ref[...] = run_scr[...]

    zero = jnp.zeros((1, tm), I32)
    info_ref[0] = jnp.concatenate(
        [i1 - N_GROUPS, i2 - N_GROUPS, rank1.astype(I32), rank2.astype(I32), zero, zero, zero, zero], axis=0)
    wfull = jnp.concatenate([jnp.broadcast_to(w1, (LANES, tm)), jnp.broadcast_to(w2, (LANES, tm))], axis=0)
    wts_ref[...] = wfull.T


def _router_call(attn_t, rec, x2, mod3, wo_a, wo_r, g2, wr, br, tm, tiles_per_batch):
    T, D = x2.shape
    nt = T // tm
    full = lambda shape: pl.BlockSpec(shape, lambda i: (0,) * len(shape))
    return pl.pallas_call(
        _router_kernel,
        grid=(nt,),
        in_specs=[
            pl.BlockSpec((1, 1, V_W, tm), lambda i: (i // tiles_per_batch, i % tiles_per_batch, 0, 0)),
            pl.BlockSpec((tm, HG_WIDTH), lambda i: (i, 0)),
            pl.BlockSpec((tm, D), lambda i: (i, 0)),
            pl.BlockSpec((1, 6, D), lambda i: (i // tiles_per_batch, 0, 0)),
            full((V_W, D)),
            full((HG_WIDTH, D)),
            full((1, D)),
            full((D, ROUTE_ROWS)),
            full((1, ROUTE_ROWS)),
        ],
        out_specs=[
            pl.BlockSpec((tm, D), lambda i: (i, 0)),
            pl.BlockSpec((tm, D), lambda i: (i, 0)),
            pl.BlockSpec((1, 8, tm), lambda i: (i, 0, 0)),
            pl.BlockSpec((tm, 2 * LANES), lambda i: (i, 0)),
            pl.BlockSpec((ROUTE_ROWS, 1), lambda i: (0, 0)),
        ],
        out_shape=[
            jax.ShapeDtypeStruct((T, D), F32),
            jax.ShapeDtypeStruct((T, D), F32),
            jax.ShapeDtypeStruct((nt, 8, tm), I32),
            jax.ShapeDtypeStruct((T, 2 * LANES), F32),
            jax.ShapeDtypeStruct((ROUTE_ROWS, 1), F32),
        ],
        scratch_shapes=[pltpu.VMEM((ROUTE_ROWS, 1), F32)],
        compiler_params=_cparams(("arbitrary",)),
        name="router",
    )(attn_t, rec, x2, mod3, wo_a, wo_r, g2, wr, br)


def _dispatch_kernel(pstart_ref, info_ref, h2_ref, xb_in_ref, xb_ref, sem):
    del xb_in_ref
    tm = h2_ref.shape[0]

    def row_copy(t, d):
        return pltpu.make_async_copy(h2_ref.at[pl.ds(t, 1)], xb_ref.at[pl.ds(d, 1)], sem)

    def issue(t, carry):
        d1 = pstart_ref[info_ref[0, 0, t]] + info_ref[0, 2, t]
        d2 = pstart_ref[info_ref[0, 1, t]] + info_ref[0, 3, t]
        row_copy(t, d1).start()
        row_copy(t, d2).start()
        return carry

    lax.fori_loop(0, tm, issue, 0)

    def drain(t, carry):
        row_copy(0, 0).wait()
        row_copy(0, 0).wait()
        return carry

    lax.fori_loop(0, tm, drain, 0)


def _dispatch_call(pstart, info, h2, xb_zero, tm):
    T, D = h2.shape
    nt = T // tm
    cap = xb_zero.shape[0]
    grid_spec = pltpu.PrefetchScalarGridSpec(
        num_scalar_prefetch=1,
        grid=(nt,),
        in_specs=[
            pl.BlockSpec((1, 8, tm), lambda i, ps: (i, 0, 0), memory_space=pltpu.SMEM),
            pl.BlockSpec((tm, D), lambda i, ps: (i, 0)),
            pl.BlockSpec(memory_space=pl.ANY),
        ],
        out_specs=pl.BlockSpec(memory_space=pl.ANY),
        scratch_shapes=[pltpu.SemaphoreType.DMA],
    )
    return pl.pallas_call(
        _dispatch_kernel,
        grid_spec=grid_spec,
        out_shape=jax.ShapeDtypeStruct((cap, D), F32),
        input_output_aliases={3: 0},
        compiler_params=_cparams(("arbitrary",)),
        name="dispatch",
    )(pstart, info, h2, xb_zero)


def _expert_kernel(te_ref, tv_ref, xb_ref, wg_ref, wu_ref, wd_ref, y_ref, wg_b, wu_b, wd_b):
    j = pl.program_id(0)
    prev = te_ref[jnp.maximum(j - 1, 0)]

    @pl.when((j == 0) | (te_ref[j] != prev))
    def _():
        wg_b[...] = wg_ref[0].astype(BF16)
        wu_b[...] = wu_ref[0].astype(BF16)
        wd_b[...] = wd_ref[0].astype(BF16)

    @pl.when(tv_ref[j] > 0)
    def _():
        x = xb_ref[...].astype(BF16)
        hid = (_silu(_dot(x, wg_b[...])) * _dot(x, wu_b[...])).astype(BF16)
        y_ref[...] = _dot(hid, wd_b[...])

    @pl.when(tv_ref[j] == 0)
    def _():
        y_ref[...] = jnp.zeros_like(y_ref)


def _expert_call(tile_expert, tile_valid, xb, w_gate, w_up, w_down):
    cap, D = xb.shape
    nt = cap // EXPERT_TILE
    grid_spec = pltpu.PrefetchScalarGridSpec(
        num_scalar_prefetch=2,
        grid=(nt,),
        in_specs=[
            pl.BlockSpec((EXPERT_TILE, D), lambda j, te, tv: (j, 0)),
            pl.BlockSpec((1, D, D_EXPERT), lambda j, te, tv: (te[j], 0, 0)),
            pl.BlockSpec((1, D, D_EXPERT), lambda j, te, tv: (te[j], 0, 0)),
            pl.BlockSpec((1, D_EXPERT, D), lambda j, te, tv: (te[j], 0, 0)),
        ],
        out_specs=pl.BlockSpec((EXPERT_TILE, D), lambda j, te, tv: (j, 0)),
        scratch_shapes=[
            pltpu.VMEM((D, D_EXPERT), BF16),
            pltpu.VMEM((D, D_EXPERT), BF16),
            pltpu.VMEM((D_EXPERT, D), BF16),
        ],
    )
    return pl.pallas_call(
        _expert_kernel,
        grid_spec=grid_spec,
        out_shape=jax.ShapeDtypeStruct((cap, D), F32),
        compiler_params=_cparams(("arbitrary",)),
        name="expert",
    )(tile_expert, tile_valid, xb, w_gate, w_up, w_down)


def _combine_kernel(pstart_ref, info_ref, x1_ref, mod_ref, wts_ref, yb_ref, o_ref, g1_scr, g2_scr, sem):
    tm = x1_ref.shape[0]

    def row_copy(d, dst, t):
        return pltpu.make_async_copy(yb_ref.at[pl.ds(d, 1)], dst.at[pl.ds(t, 1)], sem)

    def issue(t, carry):
        d1 = pstart_ref[info_ref[0, 0, t]] + info_ref[0, 2, t]
        d2 = pstart_ref[info_ref[0, 1, t]] + info_ref[0, 3, t]
        row_copy(d1, g1_scr, t).start()
        row_copy(d2, g2_scr, t).start()
        return carry

    lax.fori_loop(0, tm, issue, 0)

    def drain(t, carry):
        row_copy(0, g1_scr, 0).wait()
        row_copy(0, g2_scr, 0).wait()
        return carry

    lax.fori_loop(0, tm, drain, 0)

    w1 = wts_ref[:, 0:1]
    w2 = wts_ref[:, LANES:LANES + 1]
    moe = w1 * g1_scr[...] + w2 * g2_scr[...]
    o_ref[...] = x1_ref[...] + mod_ref[0][5:6] * moe


def _combine_call(pstart, info, x1, mod3, wts, yb, tm, tiles_per_batch):
    T, D = x1.shape
    nt = T // tm
    grid_spec = pltpu.PrefetchScalarGridSpec(
        num_scalar_prefetch=1,
        grid=(nt,),
        in_specs=[
            pl.BlockSpec((1, 8, tm), lambda i, ps: (i, 0, 0), memory_space=pltpu.SMEM),
            pl.BlockSpec((tm, D), lambda i, ps: (i, 0)),
            pl.BlockSpec((1, 6, D), lambda i, ps: (i // tiles_per_batch, 0, 0)),
            pl.BlockSpec((tm, 2 * LANES), lambda i, ps: (i, 0)),
            pl.BlockSpec(memory_space=pl.ANY),
        ],
        out_specs=pl.BlockSpec((tm, D), lambda i, ps: (i, 0)),
        scratch_shapes=[
            pltpu.VMEM((tm, D), F32),
            pltpu.VMEM((tm, D), F32),
            pltpu.SemaphoreType.DMA,
        ],
    )
    return pl.pallas_call(
        _combine_kernel,
        grid_spec=grid_spec,
        out_shape=jax.ShapeDtypeStruct((T, D), F32),
        compiler_params=_cparams(("arbitrary",)),
        name="combine",
    )(pstart, info, x1, mod3, wts, yb)


def _rotate_half_matrix():
    half = MLA_ROPE // 2
    p = np.zeros((MLA_ROPE, MLA_ROPE), np.float32)
    for j in range(half):
        p[j + half, j] = -1.0
        p[j, j + half] = 1.0
    return jnp.asarray(p)


def _slot_gain(g, factor):
    half = MLA_ROPE // 2
    g_n, g_r = g[:MLA_NOPE], g[MLA_NOPE:]
    slot = jnp.concatenate([g_n, g_r, g_r[half:], g_r[:half]]) * factor
    return jnp.tile(slot, MLA_HEADS)[None, :].astype(F32)


def _prepare_weights(w_in, wq_up, wkv_up, qn_g, kn_g):
    rot = _rotate_half_matrix()
    s0 = MLA_Q_LORA
    s1 = s0 + MLA_KV_LORA
    s2 = s1 + MLA_ROPE
    w_kr = w_in[:, s1:s2]
    kr_slot = jnp.concatenate([jnp.zeros((D_MODEL, MLA_NOPE), F32), w_kr, w_kr @ rot], axis=1)
    win_p = jnp.concatenate([w_in[:, :s1], kr_slot, w_in[:, s2:]], axis=1).astype(BF16)

    wq = wq_up.reshape(MLA_Q_LORA, MLA_HEADS, MLA_QK)
    wq_r = wq[:, :, MLA_NOPE:]
    wq_ext = jnp.concatenate([wq, jnp.einsum('lhr,rs->lhs', wq_r, rot)], axis=2)
    wq_ext = wq_ext.reshape(MLA_Q_LORA, SLOT_W).astype(BF16)

    wkv = wkv_up.reshape(MLA_KV_LORA, MLA_HEADS, MLA_NOPE + MLA_V)
    wk_slot = jnp.concatenate([wkv[:, :, :MLA_NOPE], jnp.zeros((MLA_KV_LORA, MLA_HEADS, LANES - MLA_NOPE), F32)],
                              axis=2).reshape(MLA_KV_LORA, SLOT_W)
    wv = wkv[:, :, MLA_NOPE:].reshape(MLA_KV_LORA, V_W)
    wkv_ext = jnp.concatenate([wk_slot, wv], axis=1).astype(BF16)

    gq = _slot_gain(qn_g, MLA_QK ** -0.5 * math.log2(math.e))
    gk = _slot_gain(kn_g, 1.0)
    return win_p, wq_ext, wkv_ext, gq, gk


def _pick(n, pref):
    return pref if n % pref == 0 else n


def kernel(x, c, positions, ada_w, ada_b, norm1_g, w_in, mla_qa_g, mla_wq_up, mla_kva_g, mla_wkv_up, mla_qn_g, mla_kn_g, hg_lb_logits, hg_norm_g, w_out, norm2_g, router_group_w, router_group_b, router_expert_w, router_expert_b, w_gate, w_up, w_down):
    B, S, D = x.shape
    assert D == D_MODEL and ada_w.shape[0] == 1, "one layer of width 1024"
    T = B * S
    tm = _pick(S, 512)
    tiles_per_batch = S // tm

    mod3 = _mod_call(c, ada_w[0], ada_b).reshape(B, 6, D)
    ctab, stab = _rope_call(positions, tm)

    win_p, wq_ext, wkv_ext, gq, gk = _prepare_weights(w_in[0], mla_wq_up[0], mla_wkv_up[0], mla_qn_g[0], mla_kn_g[0])
    q_hm, k_hm, vt, hg = _proj_call(x, mod3, norm1_g, win_p, mla_qa_g, wq_ext, mla_kva_g, wkv_ext, gq, gk,
                                        ctab, stab, tm)
    attn_t = _attn_call(q_hm, k_hm, vt, tm)
    rec = _hgrn_call(hg, hg_lb_logits.reshape(2 * hg_lb_logits.shape[1], HG_WIDTH), hg_norm_g, hps=2)

    wo = w_out[0].astype(BF16)
    wr = jnp.concatenate([router_group_w[0], router_expert_w[0],
                          jnp.zeros((D, ROUTE_ROWS - N_GROUPS - N_EXPERTS), F32)], axis=1).astype(BF16)
    br = jnp.concatenate([router_group_b[0], router_expert_b[0],
                          jnp.zeros((ROUTE_ROWS - N_GROUPS - N_EXPERTS,), F32)])[None, :]
    x1, h2, info, wts, cnt = _router_call(attn_t, rec.reshape(T, HG_WIDTH), x.reshape(T, D), mod3,
                                          wo[:V_W], wo[V_W:], norm2_g, wr, br, tm, tiles_per_batch)

    counts = cnt[N_GROUPS:N_GROUPS + N_EXPERTS, 0].astype(I32)
    padded = (counts + EXPERT_TILE - 1) // EXPERT_TILE * EXPERT_TILE
    pend = jnp.cumsum(padded)
    pstart = (pend - padded).astype(I32)
    n_tiles = (T * TOP_K) // EXPERT_TILE + N_EXPERTS
    tile_start = jnp.arange(n_tiles, dtype=I32) * EXPERT_TILE
    tile_expert = jnp.minimum(jnp.sum(tile_start[:, None] >= pend[None, :], axis=1), N_EXPERTS - 1).astype(I32)
    tile_valid = (tile_start < pend[-1]).astype(I32)

    xb = _dispatch_call(pstart, info, h2, jnp.zeros((n_tiles * EXPERT_TILE, D), F32), tm)
    yb = _expert_call(tile_expert, tile_valid, xb, w_gate[0], w_up[0], w_down[0])
    out = _combine_call(pstart, info, x1, mod3, wts, yb, tm, tiles_per_batch)
    return out.reshape(B, S, D)
```

```python
import functools
import math

import numpy as np
import jax
import jax.numpy as jnp
from jax import lax
from jax.experimental import pallas as pl
from jax.experimental.pallas import tpu as pltpu

F32 = jnp.float32
BF16 = jnp.bfloat16
I32 = jnp.int32

D_MODEL = 1024
MLA_HEADS = 8
MLA_NOPE = 64
MLA_ROPE = 32
MLA_QK = MLA_NOPE + MLA_ROPE
MLA_V = 64
MLA_Q_LORA = 384
MLA_KV_LORA = 256
ROPE_THETA = 10000.0
HG_HEADS = 4
HG_DIM = 128
HG_WIDTH = HG_HEADS * HG_DIM
HG_CHUNK = 64
N_GROUPS = 4
EPG = 8
N_EXPERTS = N_GROUPS * EPG
TOP_K = 2
D_EXPERT = 512
EPS = 1e-6

LANES = 128
VMEM_LIMIT = 56 * 1024 * 1024

SLOT_W = MLA_HEADS * LANES
V_W = MLA_HEADS * MLA_V
HG_SEGS = 5
W_IN_COLS = MLA_Q_LORA + MLA_KV_LORA + LANES + HG_SEGS * HG_WIDTH
HG_GROUP_ROWS = 4 * HG_CHUNK
ONES_ROWS = 16
VT_ROWS = MLA_V + ONES_ROWS
ROUTE_ROWS = 128
EXPERT_TILE = 256
ROW_CHUNK = 8


def _cparams(sem, vmem=VMEM_LIMIT):
    return pltpu.CompilerParams(dimension_semantics=sem, vmem_limit_bytes=vmem)


def _rms(x):
    return x * lax.rsqrt(jnp.mean(x * x, axis=-1, keepdims=True) + EPS)


def _silu(x):
    return x * jax.nn.sigmoid(x)


def _dot(a, b):
    return jnp.dot(a, b, preferred_element_type=F32)


def _dot_nt(a, b):
    return lax.dot_general(a, b, (((1,), (1,)), ((), ())), preferred_element_type=F32)


def _dot_tn(a, b):
    return lax.dot_general(a, b, (((0,), (0,)), ((), ())), preferred_element_type=F32)


def _mod_kernel(c_ref, w_ref, b_ref, o_ref):
    a = _silu(c_ref[...]).astype(BF16)
    o_ref[...] = _dot(a, w_ref[...].astype(BF16)) + b_ref[...]


def _mod_call(c, ada_w, ada_b):
    B = c.shape[0]
    n = ada_w.shape[1] // D_MODEL
    return pl.pallas_call(
        _mod_kernel,
        grid=(n,),
        in_specs=[
            pl.BlockSpec((B, D_MODEL), lambda j: (0, 0)),
            pl.BlockSpec((D_MODEL, D_MODEL), lambda j: (0, j)),
            pl.BlockSpec((1, D_MODEL), lambda j: (0, j)),
        ],
        out_specs=pl.BlockSpec((B, D_MODEL), lambda j: (0, j)),
        out_shape=jax.ShapeDtypeStruct((B, n * D_MODEL), F32),
        compiler_params=_cparams(("arbitrary",)),
        name="mod",
    )(c, ada_w, ada_b)


def _rope_kernel(pos_ref, c_ref, s_ref):
    half = MLA_ROPE // 2
    tm = pos_ref.shape[2]
    idx = lax.broadcasted_iota(I32, (half, tm), 0).astype(F32)
    inv_freq = ROPE_THETA ** (-idx / half)
    ang = pos_ref[0].astype(F32) * inv_freq
    c = jnp.cos(ang)
    s = jnp.sin(ang)
    ct = jnp.concatenate([jnp.ones((MLA_NOPE, tm), F32), c, c, jnp.zeros((MLA_ROPE, tm), F32)], axis=0)
    st = jnp.concatenate([jnp.zeros((MLA_QK, tm), F32), s, s], axis=0)
    c_ref[0] = ct.T
    s_ref[0] = st.T


def _rope_call(positions, tm):
    B, S = positions.shape
    pos3 = positions.reshape(B, 1, S)
    out = jax.ShapeDtypeStruct((B, S, LANES), F32)
    return pl.pallas_call(
        _rope_kernel,
        grid=(B, S // tm),
        in_specs=[pl.BlockSpec((1, 1, tm), lambda b, i: (b, 0, i))],
        out_specs=[pl.BlockSpec((1, tm, LANES), lambda b, i: (b, i, 0))] * 2,
        out_shape=[out, out],
        compiler_params=_cparams(("arbitrary", "arbitrary")),
        name="rope",
    )(pos3)


def _head_finish(t, g_row, ct, st, lane_valid):
    ss = jnp.sum(jnp.where(lane_valid, t * t, 0.0), axis=-1, keepdims=True) * (1.0 / MLA_QK)
    tg = t * lax.rsqrt(ss + EPS) * g_row
    return tg * ct + pltpu.roll(tg * st, LANES - MLA_ROPE, 1)


def _proj_kernel(x_ref, mod_ref, g1_ref, win_ref, qag_ref, wq_ref, kvag_ref, wkv_ref, gq_ref, gk_ref,
                 ct_ref, st_ref, q_out, k_out, vt_out, hg_out):
    x = x_ref[0]
    mod = mod_ref[0]
    h = _rms(x) * g1_ref[...]
    h = h * (1.0 + mod[1:2]) + mod[0:1]
    hb = h.astype(BF16)

    ct = ct_ref[0]
    st = st_ref[0]
    tm = x.shape[0]
    lane_valid = lax.broadcasted_iota(I32, (tm, LANES), 1) < MLA_QK

    c0 = MLA_Q_LORA
    c1 = c0 + MLA_KV_LORA
    c2 = c1 + LANES

    q_lat = _dot(hb, win_ref[:, 0:c0])
    qn = (_rms(q_lat) * qag_ref[...]).astype(BF16)
    qe = _dot(qn, wq_ref[...])
    for hd in range(MLA_HEADS):
        sl = slice(hd * LANES, (hd + 1) * LANES)
        q_out[0, hd] = _head_finish(qe[:, sl], gq_ref[:, sl], ct, st, lane_valid).astype(BF16)

    kv_lat = _dot(hb, win_ref[:, c0:c1])
    kvn = (_rms(kv_lat) * kvag_ref[...]).astype(BF16)
    ke = _dot(kvn, wkv_ref[:, 0:SLOT_W])
    kr = _dot(hb, win_ref[:, c1:c2])
    for hd in range(MLA_HEADS):
        sl = slice(hd * LANES, (hd + 1) * LANES)
        k_out[0, hd] = _head_finish(ke[:, sl] + kr, gk_ref[:, sl], ct, st, lane_valid).astype(BF16)

    v_t = _dot(kvn, wkv_ref[:, SLOT_W:SLOT_W + V_W]).T.astype(BF16)
    ones = jnp.ones((ONES_ROWS, tm), BF16)
    for hd in range(MLA_HEADS):
        vt_out[0, hd, 0:MLA_V, :] = v_t[hd * MLA_V:(hd + 1) * MLA_V, :]
        vt_out[0, hd, MLA_V:VT_ROWS, :] = ones

    for sg in range(HG_SEGS):
        lo = c2 + sg * HG_WIDTH
        hg_out[0, :, sg * HG_WIDTH:(sg + 1) * HG_WIDTH] = _dot(hb, win_ref[:, lo:lo + HG_WIDTH]).astype(BF16)


def _proj_call(x, mod3, g1, win_p, qag, wq_ext, kvag, wkv_ext, gq, gk, ctab, stab, tm):
    B, S, D = x.shape
    nt = S // tm
    full = lambda shape: pl.BlockSpec(shape, lambda b, i: (0,) * len(shape))
    return pl.pallas_call(
        _proj_kernel,
        grid=(B, nt),
        in_specs=[
            pl.BlockSpec((1, tm, D), lambda b, i: (b, i, 0)),
            pl.BlockSpec((1, 6, D), lambda b, i: (b, 0, 0)),
            full((1, D)),
            full((D, W_IN_COLS)),
            full((1, MLA_Q_LORA)),
            full((MLA_Q_LORA, SLOT_W)),
            full((1, MLA_KV_LORA)),
            full((MLA_KV_LORA, SLOT_W + V_W)),
            full((1, SLOT_W)),
            full((1, SLOT_W)),
            pl.BlockSpec((1, tm, LANES), lambda b, i: (b, i, 0)),
            pl.BlockSpec((1, tm, LANES), lambda b, i: (b, i, 0)),
        ],
        out_specs=[
            pl.BlockSpec((1, MLA_HEADS, tm, LANES), lambda b, i: (b, 0, i, 0)),
            pl.BlockSpec((1, MLA_HEADS, tm, LANES), lambda b, i: (b, 0, i, 0)),
            pl.BlockSpec((1, MLA_HEADS, VT_ROWS, tm), lambda b, i: (b, 0, 0, i)),
            pl.BlockSpec((1, tm, HG_SEGS * HG_WIDTH), lambda b, i: (b, i, 0)),
        ],
        out_shape=[
            jax.ShapeDtypeStruct((B, MLA_HEADS, S, LANES), BF16),
            jax.ShapeDtypeStruct((B, MLA_HEADS, S, LANES), BF16),
            jax.ShapeDtypeStruct((B, MLA_HEADS, VT_ROWS, S), BF16),
            jax.ShapeDtypeStruct((B, S, HG_SEGS * HG_WIDTH), BF16),
        ],
        compiler_params=_cparams(("arbitrary", "arbitrary")),
        name="proj",
    )(x, mod3, g1, win_p, qag, wq_ext, kvag, wkv_ext, gq, gk, ctab, stab)


def _attn_kernel(q_ref, k_ref, vt_ref, o_ref, s_scr, p_scr):
    tq = o_ref.shape[3]
    n_items = o_ref.shape[1] * MLA_HEADS

    def split(i):
        return i // MLA_HEADS, i % MLA_HEADS

    def qk(i, slot):
        qt, hd = split(i)
        q = q_ref[0, hd, pl.ds(pl.multiple_of(qt * tq, tq), tq), :]
        s_scr[slot] = _dot_nt(k_ref[0, hd], q)

    def softmax(slot):
        s_t = s_scr[slot]
        m = jnp.max(s_t, axis=0, keepdims=True)
        p_scr[slot] = jnp.exp2(s_t - m).astype(BF16)

    def pv(i, slot):
        qt, hd = split(i)
        o_t = _dot(vt_ref[0, hd], p_scr[slot])
        o = o_t[:MLA_V] / o_t[MLA_V:MLA_V + 1]
        o_ref[0, qt, pl.ds(pl.multiple_of(hd * MLA_V, MLA_V), MLA_V), :] = o.astype(BF16)

    qk(0, 0)
    qk(1, 1)
    softmax(0)

    def body(jj, carry):
        j = 2 * jj
        qk(j, 0)
        softmax(1)
        pv(j - 2, 0)
        qk(j + 1, 1)
        softmax(0)
        pv(j - 1, 1)
        return carry

    lax.fori_loop(1, n_items // 2, body, 0)
    softmax(1)
    pv(n_items - 2, 0)
    pv(n_items - 1, 1)


def _attn_call(q_hm, k_hm, vt, tq):
    B, _, S, _ = q_hm.shape
    nq = S // tq
    return pl.pallas_call(
        _attn_kernel,
        grid=(B,),
        in_specs=[
            pl.BlockSpec((1, MLA_HEADS, S, LANES), lambda b: (b, 0, 0, 0)),
            pl.BlockSpec((1, MLA_HEADS, S, LANES), lambda b: (b, 0, 0, 0)),
            pl.BlockSpec((1, MLA_HEADS, VT_ROWS, S), lambda b: (b, 0, 0, 0)),
        ],
        out_specs=pl.BlockSpec((1, nq, V_W, tq), lambda b: (b, 0, 0, 0)),
        out_shape=jax.ShapeDtypeStruct((B, nq, V_W, tq), BF16),
        scratch_shapes=[
            pltpu.VMEM((2, S, tq), F32),
            pltpu.VMEM((2, S, tq), BF16),
        ],
        compiler_params=_cparams(("arbitrary",)),
        name="attn",
    )(q_hm, k_hm, vt)


def _hgrn_kernel(hq_ref, ff_ref, fb_ref, hi_ref, hgate_ref, lbl_ref, ng_ref, o_ref, of_scr, ob_scr, st_scr, *, hps):
    S = hq_ref.shape[1]
    C = HG_CHUNK
    G = HG_GROUP_ROWS
    cpg = G // C
    n_groups = S // G

    def lower_bound(l0, l1):
        m = jnp.maximum(l0, l1)
        e0 = jnp.exp(l0 - m)
        return e0 / (e0 + jnp.exp(l1 - m))

    lb_f = lower_bound(lbl_ref[0:1, :], lbl_ref[1:2, :])
    lb_b = lower_bound(lbl_ref[2:3, :], lbl_ref[3:4, :])

    rr = lax.broadcasted_iota(I32, (G, G), 0)
    cc = lax.broadcasted_iota(I32, (G, G), 1)
    same_chunk = (rr // C) == (cc // C)
    mask_f = same_chunk & (cc <= rr)
    mask_b = same_chunk & (cc >= rr)
    tri_f = jnp.where(mask_f, 1.0, 0.0).astype(BF16)
    tri_b = jnp.where(mask_b, 1.0, 0.0).astype(BF16)

    st_scr[...] = jnp.zeros_like(st_scr)

    def group_step(d, row0, j, lb, f_ref, o_scr, mask, tri, fwd):
        ls = slice(j * HG_DIM, (j + 1) * HG_DIM)
        rows = pl.ds(row0, G)
        lbj = lb[:, ls]
        f = lbj + (1.0 - lbj) * jax.nn.sigmoid(f_ref[0, rows, ls].astype(F32))
        k = 1.0 - f
        lf = jnp.log(f)
        hi = lf.astype(BF16)
        lo = (lf - hi.astype(F32)).astype(BF16)
        bc = _dot(tri, hi) + _dot(tri, lo)
        eb = jnp.exp(bc)
        q = _silu(hq_ref[0, rows, ls].astype(F32))
        v = hi_ref[0, rows, ls]
        qd = (q * eb).astype(BF16)
        kd = k * jnp.exp(-bc)
        att = jnp.where(mask, _dot_nt(qd, kd.astype(BF16)), 0.0)
        o_intra = _dot(att.astype(BF16), v)
        st_t = st_scr[d * hps + j]
        o_inter = [None] * cpg
        for c in (range(cpg) if fwd else reversed(range(cpg))):
            cs = slice(c * C, (c + 1) * C)
            o_inter[c] = _dot_nt(qd[cs], st_t.astype(BF16))
            last = (c + 1) * C - 1 if fwd else c * C
            e_last = eb[last:last + 1, :]
            k_rem = (kd[cs] * e_last).astype(BF16)
            st_t = st_t * e_last + _dot_tn(v[cs], k_rem)
        st_scr[d * hps + j] = st_t
        o_scr[rows, ls] = o_intra + jnp.concatenate(o_inter, axis=0)

    def body(n, carry):
        row_f = pl.multiple_of(n * G, G)
        row_b = pl.multiple_of((n_groups - 1 - n) * G, G)
        for j in range(hps):
            group_step(0, row_f, j, lb_f, ff_ref, of_scr, mask_f, tri_f, True)
            group_step(1, row_b, j, lb_b, fb_ref, ob_scr, mask_b, tri_b, False)
        return carry

    lax.fori_loop(0, n_groups, body, 0)

    for j in range(hps):
        ls = slice(j * HG_DIM, (j + 1) * HG_DIM)
        o = of_scr[:, ls] + ob_scr[:, ls]
        gate = _silu(hgate_ref[0, :, ls].astype(F32))
        o_ref[0, :, ls] = (_rms(o) * ng_ref[...] * gate).astype(BF16)


def _hgrn_call(hg, lb_logits, norm_g, hps):
    B, S, _ = hg.shape
    W = hps * HG_DIM
    nh = HG_HEADS // hps
    seg = lambda sg: pl.BlockSpec((1, S, W), lambda b, j, sg=sg: (b, 0, sg * nh + j))
    return pl.pallas_call(
        functools.partial(_hgrn_kernel, hps=hps),
        grid=(B, nh),
        in_specs=[
            seg(0), seg(1), seg(2), seg(3), seg(4),
            pl.BlockSpec((4, W), lambda b, j: (0, j)),
            pl.BlockSpec((1, HG_DIM), lambda b, j: (0, 0)),
        ],
        out_specs=pl.BlockSpec((1, S, W), lambda b, j: (b, 0, j)),
        out_shape=jax.ShapeDtypeStruct((B, S, HG_WIDTH), BF16),
        scratch_shapes=[
            pltpu.VMEM((S, W), F32),
            pltpu.VMEM((S, W), F32),
            pltpu.VMEM((2 * hps, HG_DIM, HG_DIM), F32),
        ],
        compiler_params=_cparams(("arbitrary", "arbitrary")),
        name="hgrn",
    )(hg, hg, hg, hg, hg, lb_logits, norm_g)


def _router_kernel(attn_ref, rec_ref, x_ref, mod_ref, woa_ref, wor_ref, g2_ref, wr_ref, br_ref,
                   x1_ref, h2_ref, info_ref, wts_ref, lch_ref):
    mod = mod_ref[0]
    mixed = _dot_tn(attn_ref[0, 0], woa_ref[...]) + _dot(rec_ref[...], wor_ref[...])
    x1 = x_ref[...] + mod[2:3] * mixed
    x1_ref[...] = x1
    h2 = _rms(x1) * g2_ref[...]
    h2 = (h2 * (1.0 + mod[4:5]) + mod[3:4]).astype(BF16)
    h2_ref[...] = h2

    logits = _dot(h2, wr_ref[...]) + br_ref[...]
    lt = logits.T
    tm = lt.shape[1]
    r = lax.broadcasted_iota(I32, (ROUTE_ROWS, tm), 0)
    neg = -jnp.inf
    big = ROUTE_ROWS

    isg = r < N_GROUPS
    gmax = jnp.max(jnp.where(isg, lt, neg), axis=0, keepdims=True)
    gidx = jnp.min(jnp.where(isg & (lt == gmax), r, big), axis=0, keepdims=True)
    gp = 1.0 / jnp.sum(jnp.where(isg, jnp.exp(lt - gmax), 0.0), axis=0, keepdims=True)

    ing = (r >= N_GROUPS) & (r < N_GROUPS + N_EXPERTS) & (((r - N_GROUPS) >> 3) == gidx)
    v1 = jnp.max(jnp.where(ing, lt, neg), axis=0, keepdims=True)
    i1 = jnp.min(jnp.where(ing & (lt == v1), r, big), axis=0, keepdims=True)
    ing2 = ing & (r != i1)
    v2 = jnp.max(jnp.where(ing2, lt, neg), axis=0, keepdims=True)
    i2 = jnp.min(jnp.where(ing2 & (lt == v2), r, big), axis=0, keepdims=True)
    e21 = jnp.exp(v2 - v1)
    w1 = gp / (1.0 + e21)
    w2 = gp * e21 / (1.0 + e21)

    oh1 = r == i1
    oh2 = r == i2
    oh = jnp.where(oh1 | oh2, 1.0, 0.0)
    ohb = oh.astype(BF16)
    tt = lax.broadcasted_iota(I32, (tm, tm), 0)
    tc = lax.broadcasted_iota(I32, (tm, tm), 1)
    upper = jnp.where(tt < tc, 1.0, 0.0).astype(BF16)
    before = _dot(ohb, upper)
    cnt_col = jnp.sum(oh, axis=1, keepdims=True)
    pad_col = jnp.ceil(cnt_col * (1.0 / ROW_CHUNK)) * ROW_CHUNK
    er = lax.broadcasted_iota(I32, (ROUTE_ROWS, ROUTE_ROWS), 0)
    ec = lax.broadcasted_iota(I32, (ROUTE_ROWS, ROUTE_ROWS), 1)
    lower = jnp.where(ec < er, 1.0, 0.0).astype(BF16)
    start_col = _dot(lower, jnp.broadcast_to(pad_col, (ROUTE_ROWS, ROUTE_ROWS)).astype(BF16))[:, 0:1]
    pos = before + start_col
    slot1 = jnp.sum(jnp.where(oh1, pos, 0.0), axis=0, keepdims=True)
    slot2 = jnp.sum(jnp.where(oh2, pos, 0.0), axis=0, keepdims=True)

    zero = jnp.zeros((1, tm), I32)
    info_ref[0] = jnp.concatenate(
        [slot1.astype(I32), slot2.astype(I32), i1 - N_GROUPS, i2 - N_GROUPS, zero, zero, zero, zero], axis=0)
    rows = [jnp.broadcast_to(v, (LANES, tm)) for v in (w1, w2, slot1, slot2)]
    wts_ref[...] = jnp.concatenate(rows, axis=0).T
    cnt_row = _dot_nt(jnp.ones((8, tm), BF16), ohb)
    lch_ref[0] = jnp.ceil(cnt_row * (1.0 / ROW_CHUNK)).astype(I32)


def _router_call(attn_t, rec, x2, mod3, wo_a, wo_r, g2, wr, br, tm, tiles_per_batch):
    T, D = x2.shape
    nt = T // tm
    full = lambda shape: pl.BlockSpec(shape, lambda i: (0,) * len(shape))
    return pl.pallas_call(
        _router_kernel,
        grid=(nt,),
        in_specs=[
            pl.BlockSpec((1, 1, V_W, tm), lambda i: (i // tiles_per_batch, i % tiles_per_batch, 0, 0)),
            pl.BlockSpec((tm, HG_WIDTH), lambda i: (i, 0)),
            pl.BlockSpec((tm, D), lambda i: (i, 0)),
            pl.BlockSpec((1, 6, D), lambda i: (i // tiles_per_batch, 0, 0)),
            full((V_W, D)),
            full((HG_WIDTH, D)),
            full((1, D)),
            full((D, ROUTE_ROWS)),
            full((1, ROUTE_ROWS)),
        ],
        out_specs=[
            pl.BlockSpec((tm, D), lambda i: (i, 0)),
            pl.BlockSpec((tm, D), lambda i: (i, 0)),
            pl.BlockSpec((1, 8, tm), lambda i: (i, 0, 0)),
            pl.BlockSpec((tm, 4 * LANES), lambda i: (i, 0)),
            pl.BlockSpec((1, 8, ROUTE_ROWS), lambda i: (i, 0, 0)),
        ],
        out_shape=[
            jax.ShapeDtypeStruct((T, D), F32),
            jax.ShapeDtypeStruct((T, D), BF16),
            jax.ShapeDtypeStruct((nt, 8, tm), I32),
            jax.ShapeDtypeStruct((T, 4 * LANES), F32),
            jax.ShapeDtypeStruct((nt, 8, ROUTE_ROWS), I32),
        ],
        compiler_params=_cparams(("arbitrary",)),
        name="router",
    )(attn_t, rec, x2, mod3, wo_a, wo_r, g2, wr, br)


def _local_rows(tm):
    need = TOP_K * tm + N_EXPERTS * (ROW_CHUNK - 1)
    return -(-need // EXPERT_TILE) * EXPERT_TILE


def _dispatch_kernel(dst_ref, info_ref, h2_ref, xb_ref, xs_scr, sem, *, n_real):
    i = pl.program_id(0)
    n = pl.num_programs(0)
    slot = i % 2
    L = xs_scr.shape[1]
    tm = h2_ref.shape[0]

    def whole(sl):
        return pltpu.make_async_copy(xs_scr.at[sl], xb_ref.at[pl.ds(0, L)], sem.at[sl])

    @pl.when(i >= 2)
    def _():
        whole(slot).wait()

    @pl.when(i < n_real)
    def _():
        s = lax.broadcasted_iota(I32, (L, tm), 0)
        hit = (s == info_ref[0, 0:1, :]) | (s == info_ref[0, 1:2, :])
        xs_scr[slot] = _dot(jnp.where(hit, 1.0, 0.0).astype(BF16), h2_ref[...])

    @pl.when(i >= n_real)
    def _():
        xs_scr[slot] = jnp.zeros((L, xs_scr.shape[2]), F32)

    for c in range(L // ROW_CHUNK):
        d = pl.multiple_of(dst_ref[i, c] * ROW_CHUNK, ROW_CHUNK)
        pltpu.make_async_copy(xs_scr.at[slot, pl.ds(c * ROW_CHUNK, ROW_CHUNK)],
                              xb_ref.at[pl.ds(d, ROW_CHUNK)], sem.at[slot]).start()

    @pl.when(i == n - 1)
    def _():
        whole(slot).wait()
        whole(1 - slot).wait()


def _dispatch_call(dst_table, info, h2, cap_rows, tm):
    T, D = h2.shape
    nt = T // tm
    n_steps = dst_table.shape[0]
    L = _local_rows(tm)
    last = nt - 1
    grid_spec = pltpu.PrefetchScalarGridSpec(
        num_scalar_prefetch=1,
        grid=(n_steps,),
        in_specs=[
            pl.BlockSpec((1, 8, tm), lambda i, dst: (jnp.minimum(i, last), 0, 0)),
            pl.BlockSpec((tm, D), lambda i, dst: (jnp.minimum(i, last), 0)),
        ],
        out_specs=pl.BlockSpec(memory_space=pl.ANY),
        scratch_shapes=[
            pltpu.VMEM((2, L, D), F32),
            pltpu.SemaphoreType.DMA((2,)),
        ],
    )
    return pl.pallas_call(
        functools.partial(_dispatch_kernel, n_real=nt),
        grid_spec=grid_spec,
        out_shape=jax.ShapeDtypeStruct((cap_rows, D), F32),
        compiler_params=_cparams(("arbitrary",)),
        name="dispatch",
    )(dst_table, info, h2)


def _expert_kernel(te_ref, tv_ref, xb_ref, wg_ref, wu_ref, wd_ref, y_ref, wg_b, wu_b, wd_b):
    j = pl.program_id(0)
    prev = te_ref[jnp.maximum(j - 1, 0)]

    @pl.when((j == 0) | (te_ref[j] != prev))
    def _():
        wg_b[...] = wg_ref[0].astype(BF16)
        wu_b[...] = wu_ref[0].astype(BF16)
        wd_b[...] = wd_ref[0].astype(BF16)

    @pl.when(tv_ref[j] > 0)
    def _():
        x = xb_ref[...].astype(BF16)
        hid = (_silu(_dot(x, wg_b[...])) * _dot(x, wu_b[...])).astype(BF16)
        y_ref[...] = _dot(hid, wd_b[...])

    @pl.when(tv_ref[j] == 0)
    def _():
        y_ref[...] = jnp.zeros_like(y_ref)


def _expert_call(tile_expert, tile_valid, xb, w_gate, w_up, w_down):
    cap, D = xb.shape
    nt = tile_expert.shape[0]
    grid_spec = pltpu.PrefetchScalarGridSpec(
        num_scalar_prefetch=2,
        grid=(nt,),
        in_specs=[
            pl.BlockSpec((EXPERT_TILE, D), lambda j, te, tv: (j, 0)),
            pl.BlockSpec((1, D, D_EXPERT), lambda j, te, tv: (te[j], 0, 0)),
            pl.BlockSpec((1, D, D_EXPERT), lambda j, te, tv: (te[j], 0, 0)),
            pl.BlockSpec((1, D_EXPERT, D), lambda j, te, tv: (te[j], 0, 0)),
        ],
        out_specs=pl.BlockSpec((EXPERT_TILE, D), lambda j, te, tv: (j, 0)),
        scratch_shapes=[
            pltpu.VMEM((D, D_EXPERT), BF16),
            pltpu.VMEM((D, D_EXPERT), BF16),
            pltpu.VMEM((D_EXPERT, D), BF16),
        ],
    )
    return pl.pallas_call(
        _expert_kernel,
        grid_spec=grid_spec,
        out_shape=jax.ShapeDtypeStruct((cap, D), F32),
        compiler_params=_cparams(("arbitrary",)),
        name="expert",
    )(tile_expert, tile_valid, xb, w_gate, w_up, w_down)


def _combine_kernel(src_ref, x1_ref, mod_ref, wts_ref, yb_ref, o_ref, ys_scr, sem):
    i = pl.program_id(0)
    n = pl.num_programs(0)
    slot = i % 2
    L = ys_scr.shape[1]
    tm = x1_ref.shape[0]

    def fetch(step, sl):
        for c in range(L // ROW_CHUNK):
            d = pl.multiple_of(src_ref[step, c] * ROW_CHUNK, ROW_CHUNK)
            pltpu.make_async_copy(yb_ref.at[pl.ds(d, ROW_CHUNK)],
                                  ys_scr.at[sl, pl.ds(c * ROW_CHUNK, ROW_CHUNK)], sem.at[sl]).start()

    @pl.when(i == 0)
    def _():
        fetch(0, 0)

    @pl.when(i + 1 < n)
    def _():
        fetch(i + 1, 1 - slot)

    pltpu.make_async_copy(yb_ref.at[pl.ds(0, L)], ys_scr.at[slot], sem.at[slot]).wait()

    s = lax.broadcasted_iota(I32, (tm, L), 1).astype(F32)
    w1 = wts_ref[:, 0:1]
    w2 = wts_ref[:, LANES:LANES + 1]
    slot1 = wts_ref[:, 2 * LANES:2 * LANES + 1]
    slot2 = wts_ref[:, 3 * LANES:3 * LANES + 1]
    pick = jnp.where(s == slot1, w1, 0.0) + jnp.where(s == slot2, w2, 0.0)
    moe = _dot(pick.astype(BF16), ys_scr[slot].astype(BF16))
    o_ref[...] = x1_ref[...] + mod_ref[0][5:6] * moe


def _combine_call(src_table, x1, mod3, wts, yb, tm, tiles_per_batch):
    T, D = x1.shape
    nt = T // tm
    L = _local_rows(tm)
    grid_spec = pltpu.PrefetchScalarGridSpec(
        num_scalar_prefetch=1,
        grid=(nt,),
        in_specs=[
            pl.BlockSpec((tm, D), lambda i, src: (i, 0)),
            pl.BlockSpec((1, 6, D), lambda i, src: (i // tiles_per_batch, 0, 0)),
            pl.BlockSpec((tm, 4 * LANES), lambda i, src: (i, 0)),
            pl.BlockSpec(memory_space=pl.ANY),
        ],
        out_specs=pl.BlockSpec((tm, D), lambda i, src: (i, 0)),
        scratch_shapes=[
            pltpu.VMEM((2, L, D), F32),
            pltpu.SemaphoreType.DMA((2,)),
        ],
    )
    return pl.pallas_call(
        _combine_kernel,
        grid_spec=grid_spec,
        out_shape=jax.ShapeDtypeStruct((T, D), F32),
        compiler_params=_cparams(("arbitrary",)),
        name="combine",
    )(src_table, x1, mod3, wts, yb)


def _rotate_half_matrix():
    half = MLA_ROPE // 2
    p = np.zeros((MLA_ROPE, MLA_ROPE), np.float32)
    for j in range(half):
        p[j + half, j] = -1.0
        p[j, j + half] = 1.0
    return jnp.asarray(p)


def _slot_gain(g, factor):
    half = MLA_ROPE // 2
    g_n, g_r = g[:MLA_NOPE], g[MLA_NOPE:]
    slot = jnp.concatenate([g_n, g_r, g_r[half:], g_r[:half]]) * factor
    return jnp.tile(slot, MLA_HEADS)[None, :].astype(F32)


def _prepare_weights(w_in, wq_up, wkv_up, qn_g, kn_g):
    rot = _rotate_half_matrix()
    s0 = MLA_Q_LORA
    s1 = s0 + MLA_KV_LORA
    s2 = s1 + MLA_ROPE
    w_kr = w_in[:, s1:s2]
    kr_slot = jnp.concatenate([jnp.zeros((D_MODEL, MLA_NOPE), F32), w_kr, w_kr @ rot], axis=1)
    win_p = jnp.concatenate([w_in[:, :s1], kr_slot, w_in[:, s2:]], axis=1).astype(BF16)

    wq = wq_up.reshape(MLA_Q_LORA, MLA_HEADS, MLA_QK)
    wq_r = wq[:, :, MLA_NOPE:]
    wq_ext = jnp.concatenate([wq, jnp.einsum('lhr,rs->lhs', wq_r, rot)], axis=2)
    wq_ext = wq_ext.reshape(MLA_Q_LORA, SLOT_W).astype(BF16)

    wkv = wkv_up.reshape(MLA_KV_LORA, MLA_HEADS, MLA_NOPE + MLA_V)
    wk_slot = jnp.concatenate([wkv[:, :, :MLA_NOPE], jnp.zeros((MLA_KV_LORA, MLA_HEADS, LANES - MLA_NOPE), F32)],
                              axis=2).reshape(MLA_KV_LORA, SLOT_W)
    wv = wkv[:, :, MLA_NOPE:].reshape(MLA_KV_LORA, V_W)
    wkv_ext = jnp.concatenate([wk_slot, wv], axis=1).astype(BF16)

    gq = _slot_gain(qn_g, MLA_QK ** -0.5 * math.log2(math.e))
    gk = _slot_gain(kn_g, 1.0)
    return win_p, wq_ext, wkv_ext, gq, gk


def _chunk_tables(lch, tm):
    nt = lch.shape[0]
    lchunks = _local_rows(tm) // ROW_CHUNK
    tch = EXPERT_TILE // ROW_CHUNK
    n_pad_steps = -(-(N_EXPERTS * (tch - 1)) // lchunks)
    n_steps = nt + n_pad_steps
    total = n_steps * lchunks
    n_tiles = total // tch

    nch = lch[:, 0, N_GROUPS:N_GROUPS + N_EXPERTS]
    lend = jnp.cumsum(nch, axis=1)
    lstart = lend - nch
    ltot = lend[:, -1]
    tot = jnp.sum(nch, axis=0)
    ptot = (tot + tch - 1) // tch * tch
    gend = jnp.cumsum(ptot)
    gstart = gend - ptot
    toff = jnp.cumsum(nch, axis=0) - nch
    c = jnp.arange(lchunks, dtype=I32)

    def lookup(idx, table):
        hit = idx[..., None] == jnp.arange(table.shape[-1], dtype=I32)
        return jnp.sum(jnp.where(hit, table, 0), axis=-1)

    e_of = jnp.minimum(jnp.sum(c[None, :, None] >= lend[:, None, :], axis=2), N_EXPERTS - 1)
    dest = lookup(e_of, (gstart[None, :] + toff - lstart)[:, None, :]) + c[None, :]
    valid = c[None, :] < ltot[:, None]
    src_table = jnp.where(valid, dest, 0).astype(I32)

    gap = jnp.concatenate([ptot - tot, total - gend[-1:]])
    gap_first = jnp.concatenate([gstart + tot, gend[-1:]])
    gap_cum = jnp.cumsum(gap)
    unused = lchunks - ltot
    rank = jnp.concatenate([
        (jnp.cumsum(unused) - unused)[:, None] + c[None, :] - ltot[:, None],
        jnp.sum(unused) + jnp.arange(n_pad_steps, dtype=I32)[:, None] * lchunks + c[None, :]])
    g_of = jnp.minimum(jnp.sum(rank[:, :, None] >= gap_cum[None, None, :], axis=2), N_EXPERTS)
    filler = lookup(g_of, gap_first - (gap_cum - gap)) + rank
    routed = jnp.concatenate([valid, jnp.zeros((n_pad_steps, lchunks), bool)])
    dst_table = jnp.where(routed, jnp.concatenate([dest, jnp.zeros((n_pad_steps, lchunks), I32)]), filler).astype(I32)

    ts = jnp.arange(n_tiles, dtype=I32) * tch
    tile_expert = jnp.minimum(jnp.sum(ts[:, None] >= gend[None, :], axis=1), N_EXPERTS - 1).astype(I32)
    tile_valid = (ts < gend[-1]).astype(I32)
    return dst_table, src_table, tile_expert, tile_valid, total * ROW_CHUNK


def _pick(n, pref):
    return pref if n % pref == 0 else n


def kernel(x, c, positions, ada_w, ada_b, norm1_g, w_in, mla_qa_g, mla_wq_up, mla_kva_g, mla_wkv_up, mla_qn_g, mla_kn_g, hg_lb_logits, hg_norm_g, w_out, norm2_g, router_group_w, router_group_b, router_expert_w, router_expert_b, w_gate, w_up, w_down):
    B, S, D = x.shape
    assert D == D_MODEL and ada_w.shape[0] == 1, "one layer of width 1024"
    T = B * S
    tm = _pick(S, 512)
    tiles_per_batch = S // tm

    mod3 = _mod_call(c, ada_w[0], ada_b).reshape(B, 6, D)
    ctab, stab = _rope_call(positions, tm)

    win_p, wq_ext, wkv_ext, gq, gk = _prepare_weights(w_in[0], mla_wq_up[0], mla_wkv_up[0], mla_qn_g[0], mla_kn_g[0])
    q_hm, k_hm, vt, hg = _proj_call(x, mod3, norm1_g, win_p, mla_qa_g, wq_ext, mla_kva_g, wkv_ext, gq, gk,
                                        ctab, stab, tm)
    attn_t = _attn_call(q_hm, k_hm, vt, tm)
    rec = _hgrn_call(hg, hg_lb_logits.reshape(2 * hg_lb_logits.shape[1], HG_WIDTH), hg_norm_g, hps=2)

    wo = w_out[0].astype(BF16)
    wr = jnp.concatenate([router_group_w[0], router_expert_w[0],
                          jnp.zeros((D, ROUTE_ROWS - N_GROUPS - N_EXPERTS), F32)], axis=1).astype(BF16)
    br = jnp.concatenate([router_group_b[0], router_expert_b[0],
                          jnp.zeros((ROUTE_ROWS - N_GROUPS - N_EXPERTS,), F32)])[None, :]
    x1, h2, info, wts, lch = _router_call(attn_t, rec.reshape(T, HG_WIDTH), x.reshape(T, D), mod3,
                                          wo[:V_W], wo[V_W:], norm2_g, wr, br, tm, tiles_per_batch)

    dst_table, src_table, tile_expert, tile_valid, cap_rows = _chunk_tables(lch, tm)
    xb = _dispatch_call(dst_table, info, h2, cap_rows, tm)
    yb = _expert_call(tile_expert, tile_valid, xb, w_gate[0], w_up[0], w_down[0])
    out = _combine_call(src_table, x1, mod3, wts, yb, tm, tiles_per_batch)
    return out.reshape(B, S, D)
```

```python
import functools
import math

import numpy as np
import jax
import jax.numpy as jnp
from jax import lax
from jax.experimental import pallas as pl
from jax.experimental.pallas import tpu as pltpu

F32 = jnp.float32
BF16 = jnp.bfloat16
I32 = jnp.int32

D_MODEL = 1024
MLA_HEADS = 8
MLA_NOPE = 64
MLA_ROPE = 32
MLA_QK = MLA_NOPE + MLA_ROPE
MLA_V = 64
MLA_Q_LORA = 384
MLA_KV_LORA = 256
ROPE_THETA = 10000.0
HG_HEADS = 4
HG_DIM = 128
HG_WIDTH = HG_HEADS * HG_DIM
HG_CHUNK = 64
N_GROUPS = 4
EPG = 8
N_EXPERTS = N_GROUPS * EPG
TOP_K = 2
D_EXPERT = 512
EPS = 1e-6

LANES = 128
VMEM_LIMIT = 56 * 1024 * 1024

SLOT_W = MLA_HEADS * LANES
V_W = MLA_HEADS * MLA_V
HG_SEGS = 5
W_IN_COLS = MLA_Q_LORA + MLA_KV_LORA + LANES + HG_SEGS * HG_WIDTH
HG_GROUP_ROWS = 4 * HG_CHUNK
ONES_ROWS = 16
VT_ROWS = MLA_V + ONES_ROWS
ROUTE_ROWS = 128
EXPERT_TILE = 256
ROW_CHUNK = 8


def _cparams(sem, vmem=VMEM_LIMIT):
    return pltpu.CompilerParams(dimension_semantics=sem, vmem_limit_bytes=vmem)


def _rms(x):
    return x * lax.rsqrt(jnp.mean(x * x, axis=-1, keepdims=True) + EPS)


def _silu(x):
    return x * jax.nn.sigmoid(x)


def _dot(a, b):
    return jnp.dot(a, b, preferred_element_type=F32)


def _dot_nt(a, b):
    return lax.dot_general(a, b, (((1,), (1,)), ((), ())), preferred_element_type=F32)


def _dot_tn(a, b):
    return lax.dot_general(a, b, (((0,), (0,)), ((), ())), preferred_element_type=F32)


def _mod_kernel(c_ref, w_ref, b_ref, o_ref):
    a = _silu(c_ref[...]).astype(BF16)
    o_ref[...] = _dot(a, w_ref[...].astype(BF16)) + b_ref[...]


def _mod_call(c, ada_w, ada_b):
    B = c.shape[0]
    n = ada_w.shape[1] // D_MODEL
    return pl.pallas_call(
        _mod_kernel,
        grid=(n,),
        in_specs=[
            pl.BlockSpec((B, D_MODEL), lambda j: (0, 0)),
            pl.BlockSpec((D_MODEL, D_MODEL), lambda j: (0, j)),
            pl.BlockSpec((1, D_MODEL), lambda j: (0, j)),
        ],
        out_specs=pl.BlockSpec((B, D_MODEL), lambda j: (0, j)),
        out_shape=jax.ShapeDtypeStruct((B, n * D_MODEL), F32),
        compiler_params=_cparams(("arbitrary",)),
        name="mod",
    )(c, ada_w, ada_b)


def _rope_kernel(pos_ref, c_ref, s_ref):
    half = MLA_ROPE // 2
    tm = pos_ref.shape[2]
    idx = lax.broadcasted_iota(I32, (half, tm), 0).astype(F32)
    inv_freq = ROPE_THETA ** (-idx / half)
    ang = pos_ref[0].astype(F32) * inv_freq
    c = jnp.cos(ang)
    s = jnp.sin(ang)
    ct = jnp.concatenate([jnp.ones((MLA_NOPE, tm), F32), c, c, jnp.zeros((MLA_ROPE, tm), F32)], axis=0)
    st = jnp.concatenate([jnp.zeros((MLA_QK, tm), F32), s, s], axis=0)
    c_ref[0] = ct.T
    s_ref[0] = st.T


def _rope_call(positions, tm):
    B, S = positions.shape
    pos3 = positions.reshape(B, 1, S)
    out = jax.ShapeDtypeStruct((B, S, LANES), F32)
    return pl.pallas_call(
        _rope_kernel,
        grid=(B, S // tm),
        in_specs=[pl.BlockSpec((1, 1, tm), lambda b, i: (b, 0, i))],
        out_specs=[pl.BlockSpec((1, tm, LANES), lambda b, i: (b, i, 0))] * 2,
        out_shape=[out, out],
        compiler_params=_cparams(("arbitrary", "arbitrary")),
        name="rope",
    )(pos3)


def _head_finish(t, g_row, ct, st, lane_valid):
    ss = jnp.sum(jnp.where(lane_valid, t * t, 0.0), axis=-1, keepdims=True) * (1.0 / MLA_QK)
    tg = t * lax.rsqrt(ss + EPS) * g_row
    return tg * ct + pltpu.roll(tg * st, LANES - MLA_ROPE, 1)


def _proj_kernel(x_ref, mod_ref, g1_ref, win_ref, qag_ref, wq_ref, kvag_ref, wkv_ref, gq_ref, gk_ref,
                 ct_ref, st_ref, q_out, k_out, vt_out, hg_out):
    x = x_ref[0]
    mod = mod_ref[0]
    h = _rms(x) * g1_ref[...]
    h = h * (1.0 + mod[1:2]) + mod[0:1]
    hb = h.astype(BF16)

    ct = ct_ref[0]
    st = st_ref[0]
    tm = x.shape[0]
    lane_valid = lax.broadcasted_iota(I32, (tm, LANES), 1) < MLA_QK

    c0 = MLA_Q_LORA
    c1 = c0 + MLA_KV_LORA
    c2 = c1 + LANES

    q_lat = _dot(hb, win_ref[:, 0:c0])
    qn = (_rms(q_lat) * qag_ref[...]).astype(BF16)
    qe = _dot(qn, wq_ref[...])
    for hd in range(MLA_HEADS):
        sl = slice(hd * LANES, (hd + 1) * LANES)
        q_out[0, hd] = _head_finish(qe[:, sl], gq_ref[:, sl], ct, st, lane_valid).astype(BF16)

    kv_lat = _dot(hb, win_ref[:, c0:c1])
    kvn = (_rms(kv_lat) * kvag_ref[...]).astype(BF16)
    ke = _dot(kvn, wkv_ref[:, 0:SLOT_W])
    kr = _dot(hb, win_ref[:, c1:c2])
    for hd in range(MLA_HEADS):
        sl = slice(hd * LANES, (hd + 1) * LANES)
        k_out[0, hd] = _head_finish(ke[:, sl] + kr, gk_ref[:, sl], ct, st, lane_valid).astype(BF16)

    v_t = _dot(kvn, wkv_ref[:, SLOT_W:SLOT_W + V_W]).T.astype(BF16)
    ones = jnp.ones((ONES_ROWS, tm), BF16)
    for hd in range(MLA_HEADS):
        vt_out[0, hd, 0:MLA_V, :] = v_t[hd * MLA_V:(hd + 1) * MLA_V, :]
        vt_out[0, hd, MLA_V:VT_ROWS, :] = ones

    for sg in range(HG_SEGS):
        lo = c2 + sg * HG_WIDTH
        hg_out[0, :, sg * HG_WIDTH:(sg + 1) * HG_WIDTH] = _dot(hb, win_ref[:, lo:lo + HG_WIDTH]).astype(BF16)


def _proj_call(x, mod3, g1, win_p, qag, wq_ext, kvag, wkv_ext, gq, gk, ctab, stab, tm):
    B, S, D = x.shape
    nt = S // tm
    full = lambda shape: pl.BlockSpec(shape, lambda b, i: (0,) * len(shape))
    return pl.pallas_call(
        _proj_kernel,
        grid=(B, nt),
        in_specs=[
            pl.BlockSpec((1, tm, D), lambda b, i: (b, i, 0)),
            pl.BlockSpec((1, 6, D), lambda b, i: (b, 0, 0)),
            full((1, D)),
            full((D, W_IN_COLS)),
            full((1, MLA_Q_LORA)),
            full((MLA_Q_LORA, SLOT_W)),
            full((1, MLA_KV_LORA)),
            full((MLA_KV_LORA, SLOT_W + V_W)),
            full((1, SLOT_W)),
            full((1, SLOT_W)),
            pl.BlockSpec((1, tm, LANES), lambda b, i: (b, i, 0)),
            pl.BlockSpec((1, tm, LANES), lambda b, i: (b, i, 0)),
        ],
        out_specs=[
            pl.BlockSpec((1, MLA_HEADS, tm, LANES), lambda b, i: (b, 0, i, 0)),
            pl.BlockSpec((1, MLA_HEADS, tm, LANES), lambda b, i: (b, 0, i, 0)),
            pl.BlockSpec((1, MLA_HEADS, VT_ROWS, tm), lambda b, i: (b, 0, 0, i)),
            pl.BlockSpec((1, tm, HG_SEGS * HG_WIDTH), lambda b, i: (b, i, 0)),
        ],
        out_shape=[
            jax.ShapeDtypeStruct((B, MLA_HEADS, S, LANES), BF16),
            jax.ShapeDtypeStruct((B, MLA_HEADS, S, LANES), BF16),
            jax.ShapeDtypeStruct((B, MLA_HEADS, VT_ROWS, S), BF16),
            jax.ShapeDtypeStruct((B, S, HG_SEGS * HG_WIDTH), BF16),
        ],
        compiler_params=_cparams(("arbitrary", "arbitrary")),
        name="proj",
    )(x, mod3, g1, win_p, qag, wq_ext, kvag, wkv_ext, gq, gk, ctab, stab)


def _attn_kernel(q_ref, k_ref, vt_ref, o_ref, s_scr, p_scr, m_scr):
    tq = o_ref.shape[3]
    n_items = o_ref.shape[1] * MLA_HEADS

    def split(i):
        return i // MLA_HEADS, i % MLA_HEADS

    def qk(i, slot):
        qt, hd = split(i)
        q = q_ref[0, hd, pl.ds(pl.multiple_of(qt * tq, tq), tq), :]
        s_t = _dot_nt(k_ref[0, hd], q)
        s_scr[slot] = s_t
        m_scr[slot] = jnp.max(s_t, axis=0, keepdims=True)

    def softmax(slot):
        p_scr[slot] = jnp.exp2(s_scr[slot] - m_scr[slot]).astype(BF16)

    def pv(i, slot):
        qt, hd = split(i)
        o_t = _dot(vt_ref[0, hd], p_scr[slot])
        o = o_t[:MLA_V] / o_t[MLA_V:MLA_V + 1]
        o_ref[0, qt, pl.ds(pl.multiple_of(hd * MLA_V, MLA_V), MLA_V), :] = o.astype(BF16)

    qk(0, 0)
    qk(1, 1)
    softmax(0)

    def body(jj, carry):
        j = 2 * jj
        qk(j, 0)
        softmax(1)
        pv(j - 2, 0)
        qk(j + 1, 1)
        softmax(0)
        pv(j - 1, 1)
        return carry

    lax.fori_loop(1, n_items // 2, body, 0)
    softmax(1)
    pv(n_items - 2, 0)
    pv(n_items - 1, 1)


def _attn_call(q_hm, k_hm, vt, tq):
    B, _, S, _ = q_hm.shape
    nq = S // tq
    return pl.pallas_call(
        _attn_kernel,
        grid=(B,),
        in_specs=[
            pl.BlockSpec((1, MLA_HEADS, S, LANES), lambda b: (b, 0, 0, 0)),
            pl.BlockSpec((1, MLA_HEADS, S, LANES), lambda b: (b, 0, 0, 0)),
            pl.BlockSpec((1, MLA_HEADS, VT_ROWS, S), lambda b: (b, 0, 0, 0)),
        ],
        out_specs=pl.BlockSpec((1, nq, V_W, tq), lambda b: (b, 0, 0, 0)),
        out_shape=jax.ShapeDtypeStruct((B, nq, V_W, tq), BF16),
        scratch_shapes=[
            pltpu.VMEM((2, S, tq), F32),
            pltpu.VMEM((2, S, tq), BF16),
            pltpu.VMEM((2, 1, tq), F32),
        ],
        compiler_params=_cparams(("arbitrary",)),
        name="attn",
    )(q_hm, k_hm, vt)


def _hgrn_kernel(hq_ref, ff_ref, fb_ref, hi_ref, hgate_ref, lbl_ref, ng_ref, o_ref, of_scr, ob_scr, st_scr,
                 bc_scr, k_scr, qd_scr, att_scr, kv_scr, el_scr, *, hps):
    S = hq_ref.shape[1]
    C = HG_CHUNK
    G = HG_GROUP_ROWS
    cpg = G // C
    n_groups = S // G

    def lower_bound(l0, l1):
        m = jnp.maximum(l0, l1)
        e0 = jnp.exp(l0 - m)
        return e0 / (e0 + jnp.exp(l1 - m))

    lb_f = lower_bound(lbl_ref[0:1, :], lbl_ref[1:2, :])
    lb_b = lower_bound(lbl_ref[2:3, :], lbl_ref[3:4, :])

    rr = lax.broadcasted_iota(I32, (G, G), 0)
    cc = lax.broadcasted_iota(I32, (G, G), 1)
    same_chunk = (rr // C) == (cc // C)
    mask_f = same_chunk & (cc <= rr)
    mask_b = same_chunk & (cc >= rr)
    tri_f = jnp.where(mask_f, 1.0, 0.0).astype(BF16)
    tri_b = jnp.where(mask_b, 1.0, 0.0).astype(BF16)

    st_scr[...] = jnp.zeros_like(st_scr)
    W = hps * HG_DIM
    dirs = ((0, lb_f, ff_ref, of_scr, mask_f, tri_f), (1, lb_b, fb_ref, ob_scr, mask_b, tri_b))

    def rows_of(d, g):
        grp = g if d == 0 else n_groups - 1 - g
        return pl.ds(grp * G if isinstance(grp, int) else pl.multiple_of(grp * G, G), G)

    def stage_gates(g, slot):
        for d, lb, f_ref, _, _, tri in dirs:
            f = lb + (1.0 - lb) * jax.nn.sigmoid(f_ref[0, rows_of(d, g), :].astype(F32))
            lf = jnp.log(f)
            hi = lf.astype(BF16)
            lo = (lf - hi.astype(F32)).astype(BF16)
            both = _dot(tri, jnp.concatenate([hi, lo], axis=1))
            bc_scr[slot, d] = both[:, :W] + both[:, W:]
            k_scr[slot, d] = 1.0 - f

    def stage_decay(g, slot):
        for d, _, _, _, mask, _ in dirs:
            rows = rows_of(d, g)
            bc = bc_scr[slot, d]
            eb = jnp.exp(bc)
            qd = (_silu(hq_ref[0, rows, :].astype(F32)) * eb).astype(BF16)
            kd = k_scr[slot, d] * jnp.exp(-bc)
            kdb = kd.astype(BF16)
            qd_scr[slot, d] = qd
            v = hi_ref[0, rows, :]
            for j in range(hps):
                ls = slice(j * HG_DIM, (j + 1) * HG_DIM)
                ch = d * hps + j
                att_scr[slot, ch] = jnp.where(mask, _dot_nt(qd[:, ls], kdb[:, ls]), 0.0).astype(BF16)
                lasts = []
                for c in range(cpg):
                    cs = slice(c * C, (c + 1) * C)
                    last = (c + 1) * C - 1 if d == 0 else c * C
                    e_last = eb[last:last + 1, ls]
                    lasts.append(e_last)
                    k_rem = (kd[cs, ls] * e_last).astype(BF16)
                    kv_scr[slot, ch * cpg + c] = _dot_tn(v[cs, ls], k_rem)
                el_scr[slot, ch] = jnp.concatenate(lasts + [jnp.zeros((8 - cpg, HG_DIM), F32)], axis=0)

    def stage_state(g, slot):
        for d, _, _, o_scr, _, _ in dirs:
            rows = rows_of(d, g)
            qd = qd_scr[slot, d]
            v = hi_ref[0, rows, :]
            for j in range(hps):
                ls = slice(j * HG_DIM, (j + 1) * HG_DIM)
                ch = d * hps + j
                o_intra = _dot(att_scr[slot, ch], v[:, ls])
                el = el_scr[slot, ch]
                st_t = st_scr[ch]
                o_inter = [None] * cpg
                for c in (range(cpg) if d == 0 else reversed(range(cpg))):
                    o_inter[c] = _dot_nt(qd[c * C:(c + 1) * C, ls], st_t.astype(BF16))
                    st_t = st_t * el[c:c + 1, :] + kv_scr[slot, ch * cpg + c]
                st_scr[ch] = st_t
                o_scr[rows, ls] = o_intra + jnp.concatenate(o_inter, axis=0)

    def step(t, parity):
        if not isinstance(t, int) or t < n_groups:
            stage_gates(t, parity)
        if not isinstance(t, int) or 1 <= t <= n_groups:
            stage_decay(t - 1, 1 - parity)
        if not isinstance(t, int) or t >= 2:
            stage_state(t - 2, parity)

    n_steps = n_groups + 2
    head = min(2, n_steps)
    tail_start = max(head, n_groups)
    for t in range(head):
        step(t, t % 2)
    n_mid = tail_start - head
    if n_mid % 2 == 0 and n_mid > 4:
        def body(i, carry):
            t = 2 + 2 * i
            step(t, 0)
            step(t + 1, 1)
            return carry
        lax.fori_loop(0, n_mid // 2, body, 0)
    else:
        for t in range(head, tail_start):
            step(t, t % 2)
    for t in range(tail_start, n_steps):
        step(t, t % 2)

    for j in range(hps):
        ls = slice(j * HG_DIM, (j + 1) * HG_DIM)
        o = of_scr[:, ls] + ob_scr[:, ls]
        gate = _silu(hgate_ref[0, :, ls].astype(F32))
        o_ref[0, :, ls] = (_rms(o) * ng_ref[...] * gate).astype(BF16)


def _hgrn_call(hg, lb_logits, norm_g, hps):
    B, S, _ = hg.shape
    W = hps * HG_DIM
    nh = HG_HEADS // hps
    G = HG_GROUP_ROWS
    cpg = G // HG_CHUNK
    seg = lambda sg: pl.BlockSpec((1, S, W), lambda b, j, sg=sg: (b, 0, sg * nh + j))
    return pl.pallas_call(
        functools.partial(_hgrn_kernel, hps=hps),
        grid=(B, nh),
        in_specs=[
            seg(0), seg(1), seg(2), seg(3), seg(4),
            pl.BlockSpec((4, W), lambda b, j: (0, j)),
            pl.BlockSpec((1, HG_DIM), lambda b, j: (0, 0)),
        ],
        out_specs=pl.BlockSpec((1, S, W), lambda b, j: (b, 0, j)),
        out_shape=jax.ShapeDtypeStruct((B, S, HG_WIDTH), BF16),
        scratch_shapes=[
            pltpu.VMEM((S, W), F32),
            pltpu.VMEM((S, W), F32),
            pltpu.VMEM((2 * hps, HG_DIM, HG_DIM), F32),
            pltpu.VMEM((2, 2, G, W), F32),
            pltpu.VMEM((2, 2, G, W), F32),
            pltpu.VMEM((2, 2, G, W), BF16),
            pltpu.VMEM((2, 2 * hps, G, G), BF16),
            pltpu.VMEM((2, 2 * hps * cpg, HG_DIM, HG_DIM), F32),
            pltpu.VMEM((2, 2 * hps, 8, HG_DIM), F32),
        ],
        compiler_params=_cparams(("arbitrary", "arbitrary")),
        name="hgrn",
    )(hg, hg, hg, hg, hg, lb_logits, norm_g)


def _router_kernel(attn_ref, rec_ref, x_ref, mod_ref, woa_ref, wor_ref, g2_ref, wr_ref, br_ref,
                   x1_ref, h2_ref, info_ref, wts_ref, lch_ref):
    mod = mod_ref[0]
    mixed = _dot_tn(attn_ref[0, 0], woa_ref[...]) + _dot(rec_ref[...], wor_ref[...])
    x1 = x_ref[...] + mod[2:3] * mixed
    x1_ref[...] = x1
    h2 = _rms(x1) * g2_ref[...]
    h2 = (h2 * (1.0 + mod[4:5]) + mod[3:4]).astype(BF16)
    h2_ref[...] = h2

    logits = _dot(h2, wr_ref[...]) + br_ref[...]
    lt = logits.T
    tm = lt.shape[1]
    r = lax.broadcasted_iota(I32, (ROUTE_ROWS, tm), 0)
    neg = -jnp.inf
    big = ROUTE_ROWS

    isg = r < N_GROUPS
    gmax = jnp.max(jnp.where(isg, lt, neg), axis=0, keepdims=True)
    gidx = jnp.min(jnp.where(isg & (lt == gmax), r, big), axis=0, keepdims=True)
    gp = 1.0 / jnp.sum(jnp.where(isg, jnp.exp(lt - gmax), 0.0), axis=0, keepdims=True)

    ing = (r >= N_GROUPS) & (r < N_GROUPS + N_EXPERTS) & (((r - N_GROUPS) >> 3) == gidx)
    v1 = jnp.max(jnp.where(ing, lt, neg), axis=0, keepdims=True)
    i1 = jnp.min(jnp.where(ing & (lt == v1), r, big), axis=0, keepdims=True)
    ing2 = ing & (r != i1)
    v2 = jnp.max(jnp.where(ing2, lt, neg), axis=0, keepdims=True)
    i2 = jnp.min(jnp.where(ing2 & (lt == v2), r, big), axis=0, keepdims=True)
    e21 = jnp.exp(v2 - v1)
    w1 = gp / (1.0 + e21)
    w2 = gp * e21 / (1.0 + e21)

    oh1 = r == i1
    oh2 = r == i2
    oh = jnp.where(oh1 | oh2, 1.0, 0.0)
    ohb = oh.astype(BF16)
    tt = lax.broadcasted_iota(I32, (tm, tm), 0)
    tc = lax.broadcasted_iota(I32, (tm, tm), 1)
    upper = jnp.where(tt < tc, 1.0, 0.0).astype(BF16)
    before = _dot(ohb, upper)
    cnt_col = jnp.sum(oh, axis=1, keepdims=True)
    pad_col = jnp.ceil(cnt_col * (1.0 / ROW_CHUNK)) * ROW_CHUNK
    er = lax.broadcasted_iota(I32, (ROUTE_ROWS, ROUTE_ROWS), 0)
    ec = lax.broadcasted_iota(I32, (ROUTE_ROWS, ROUTE_ROWS), 1)
    lower = jnp.where(ec < er, 1.0, 0.0).astype(BF16)
    start_col = _dot(lower, jnp.broadcast_to(pad_col, (ROUTE_ROWS, ROUTE_ROWS)).astype(BF16))[:, 0:1]
    pos = before + start_col
    slot1 = jnp.sum(jnp.where(oh1, pos, 0.0), axis=0, keepdims=True)
    slot2 = jnp.sum(jnp.where(oh2, pos, 0.0), axis=0, keepdims=True)

    zero = jnp.zeros((1, tm), I32)
    info_ref[0] = jnp.concatenate(
        [slot1.astype(I32), slot2.astype(I32), i1 - N_GROUPS, i2 - N_GROUPS, zero, zero, zero, zero], axis=0)
    rows = [jnp.broadcast_to(v, (LANES, tm)) for v in (w1, w2, slot1, slot2)]
    wts_ref[...] = jnp.concatenate(rows, axis=0).T
    cnt_row = _dot_nt(jnp.ones((8, tm), BF16), ohb)
    lch_ref[0] = jnp.ceil(cnt_row * (1.0 / ROW_CHUNK)).astype(I32)


def _router_call(attn_t, rec, x2, mod3, wo_a, wo_r, g2, wr, br, tm, tiles_per_batch):
    T, D = x2.shape
    nt = T // tm
    full = lambda shape: pl.BlockSpec(shape, lambda i: (0,) * len(shape))
    return pl.pallas_call(
        _router_kernel,
        grid=(nt,),
        in_specs=[
            pl.BlockSpec((1, 1, V_W, tm), lambda i: (i // tiles_per_batch, i % tiles_per_batch, 0, 0)),
            pl.BlockSpec((tm, HG_WIDTH), lambda i: (i, 0)),
            pl.BlockSpec((tm, D), lambda i: (i, 0)),
            pl.BlockSpec((1, 6, D), lambda i: (i // tiles_per_batch, 0, 0)),
            full((V_W, D)),
            full((HG_WIDTH, D)),
            full((1, D)),
            full((D, ROUTE_ROWS)),
            full((1, ROUTE_ROWS)),
        ],
        out_specs=[
            pl.BlockSpec((tm, D), lambda i: (i, 0)),
            pl.BlockSpec((tm, D), lambda i: (i, 0)),
            pl.BlockSpec((1, 8, tm), lambda i: (i, 0, 0)),
            pl.BlockSpec((tm, 4 * LANES), lambda i: (i, 0)),
            pl.BlockSpec((1, 8, ROUTE_ROWS), lambda i: (i, 0, 0)),
        ],
        out_shape=[
            jax.ShapeDtypeStruct((T, D), F32),
            jax.ShapeDtypeStruct((T, D), BF16),
            jax.ShapeDtypeStruct((nt, 8, tm), I32),
            jax.ShapeDtypeStruct((T, 4 * LANES), F32),
            jax.ShapeDtypeStruct((nt, 8, ROUTE_ROWS), I32),
        ],
        compiler_params=_cparams(("arbitrary",)),
        name="router",
    )(attn_t, rec, x2, mod3, wo_a, wo_r, g2, wr, br)


def _local_rows(tm):
    need = TOP_K * tm + N_EXPERTS * (ROW_CHUNK - 1)
    return -(-need // EXPERT_TILE) * EXPERT_TILE


def _dispatch_kernel(dst_ref, info_ref, h2_ref, xb_ref, xs_scr, sem, *, n_real):
    i = pl.program_id(0)
    n = pl.num_programs(0)
    slot = i % 2
    L = xs_scr.shape[1]
    tm = h2_ref.shape[0]

    def whole(sl):
        return pltpu.make_async_copy(xs_scr.at[sl], xb_ref.at[pl.ds(0, L)], sem.at[sl])

    @pl.when(i >= 2)
    def _():
        whole(slot).wait()

    @pl.when(i < n_real)
    def _():
        s = lax.broadcasted_iota(I32, (L, tm), 0)
        hit = (s == info_ref[0, 0:1, :]) | (s == info_ref[0, 1:2, :])
        xs_scr[slot] = _dot(jnp.where(hit, 1.0, 0.0).astype(BF16), h2_ref[...])

    @pl.when(i >= n_real)
    def _():
        xs_scr[slot] = jnp.zeros((L, xs_scr.shape[2]), F32)

    for c in range(L // ROW_CHUNK):
        d = pl.multiple_of(dst_ref[i, c] * ROW_CHUNK, ROW_CHUNK)
        pltpu.make_async_copy(xs_scr.at[slot, pl.ds(c * ROW_CHUNK, ROW_CHUNK)],
                              xb_ref.at[pl.ds(d, ROW_CHUNK)], sem.at[slot]).start()

    @pl.when(i == n - 1)
    def _():
        whole(slot).wait()
        whole(1 - slot).wait()


def _dispatch_call(dst_table, info, h2, cap_rows, tm):
    T, D = h2.shape
    nt = T // tm
    n_steps = dst_table.shape[0]
    L = _local_rows(tm)
    last = nt - 1
    grid_spec = pltpu.PrefetchScalarGridSpec(
        num_scalar_prefetch=1,
        grid=(n_steps,),
        in_specs=[
            pl.BlockSpec((1, 8, tm), lambda i, dst: (jnp.minimum(i, last), 0, 0)),
            pl.BlockSpec((tm, D), lambda i, dst: (jnp.minimum(i, last), 0)),
        ],
        out_specs=pl.BlockSpec(memory_space=pl.ANY),
        scratch_shapes=[
            pltpu.VMEM((2, L, D), F32),
            pltpu.SemaphoreType.DMA((2,)),
        ],
    )
    return pl.pallas_call(
        functools.partial(_dispatch_kernel, n_real=nt),
        grid_spec=grid_spec,
        out_shape=jax.ShapeDtypeStruct((cap_rows, D), F32),
        compiler_params=_cparams(("arbitrary",)),
        name="dispatch",
    )(dst_table, info, h2)


def _expert_kernel(te_ref, tv_ref, xb_ref, wg_ref, wu_ref, wd_ref, y_ref, wg_b, wu_b, wd_b):
    j = pl.program_id(0)
    prev = te_ref[jnp.maximum(j - 1, 0)]

    @pl.when((j == 0) | (te_ref[j] != prev))
    def _():
        wg_b[...] = wg_ref[0].astype(BF16)
        wu_b[...] = wu_ref[0].astype(BF16)
        wd_b[...] = wd_ref[0].astype(BF16)

    @pl.when(tv_ref[j] > 0)
    def _():
        x = xb_ref[...].astype(BF16)
        hid = (_silu(_dot(x, wg_b[...])) * _dot(x, wu_b[...])).astype(BF16)
        y_ref[...] = _dot(hid, wd_b[...])

    @pl.when(tv_ref[j] == 0)
    def _():
        y_ref[...] = jnp.zeros_like(y_ref)


def _expert_call(tile_expert, tile_valid, xb, w_gate, w_up, w_down):
    cap, D = xb.shape
    nt = tile_expert.shape[0]
    grid_spec = pltpu.PrefetchScalarGridSpec(
        num_scalar_prefetch=2,
        grid=(nt,),
        in_specs=[
            pl.BlockSpec((EXPERT_TILE, D), lambda j, te, tv: (j * tv[j], 0)),
            pl.BlockSpec((1, D, D_EXPERT), lambda j, te, tv: (te[j], 0, 0)),
            pl.BlockSpec((1, D, D_EXPERT), lambda j, te, tv: (te[j], 0, 0)),
            pl.BlockSpec((1, D_EXPERT, D), lambda j, te, tv: (te[j], 0, 0)),
        ],
        out_specs=pl.BlockSpec((EXPERT_TILE, D), lambda j, te, tv: (j, 0)),
        scratch_shapes=[
            pltpu.VMEM((D, D_EXPERT), BF16),
            pltpu.VMEM((D, D_EXPERT), BF16),
            pltpu.VMEM((D_EXPERT, D), BF16),
        ],
    )
    return pl.pallas_call(
        _expert_kernel,
        grid_spec=grid_spec,
        out_shape=jax.ShapeDtypeStruct((cap, D), F32),
        compiler_params=_cparams(("arbitrary",)),
        name="expert",
    )(tile_expert, tile_valid, xb, w_gate, w_up, w_down)


def _combine_kernel(src_ref, x1_ref, mod_ref, wts_ref, yb_ref, o_ref, ys_scr, sem):
    i = pl.program_id(0)
    n = pl.num_programs(0)
    slot = i % 2
    L = ys_scr.shape[1]
    tm = x1_ref.shape[0]

    def fetch(step, sl):
        for c in range(L // ROW_CHUNK):
            d = pl.multiple_of(src_ref[step, c] * ROW_CHUNK, ROW_CHUNK)
            pltpu.make_async_copy(yb_ref.at[pl.ds(d, ROW_CHUNK)],
                                  ys_scr.at[sl, pl.ds(c * ROW_CHUNK, ROW_CHUNK)], sem.at[sl]).start()

    @pl.when(i == 0)
    def _():
        fetch(0, 0)

    @pl.when(i + 1 < n)
    def _():
        fetch(i + 1, 1 - slot)

    pltpu.make_async_copy(yb_ref.at[pl.ds(0, L)], ys_scr.at[slot], sem.at[slot]).wait()

    s = lax.broadcasted_iota(I32, (tm, L), 1).astype(F32)
    w1 = wts_ref[:, 0:1]
    w2 = wts_ref[:, LANES:LANES + 1]
    slot1 = wts_ref[:, 2 * LANES:2 * LANES + 1]
    slot2 = wts_ref[:, 3 * LANES:3 * LANES + 1]
    pick = jnp.where(s == slot1, w1, 0.0) + jnp.where(s == slot2, w2, 0.0)
    moe = _dot(pick.astype(BF16), ys_scr[slot].astype(BF16))
    o_ref[...] = x1_ref[...] + mod_ref[0][5:6] * moe


def _combine_call(src_table, x1, mod3, wts, yb, tm, tiles_per_batch):
    T, D = x1.shape
    nt = T // tm
    L = _local_rows(tm)
    grid_spec = pltpu.PrefetchScalarGridSpec(
        num_scalar_prefetch=1,
        grid=(nt,),
        in_specs=[
            pl.BlockSpec((tm, D), lambda i, src: (i, 0)),
            pl.BlockSpec((1, 6, D), lambda i, src: (i // tiles_per_batch, 0, 0)),
            pl.BlockSpec((tm, 4 * LANES), lambda i, src: (i, 0)),
            pl.BlockSpec(memory_space=pl.ANY),
        ],
        out_specs=pl.BlockSpec((tm, D), lambda i, src: (i, 0)),
        scratch_shapes=[
            pltpu.VMEM((2, L, D), F32),
            pltpu.SemaphoreType.DMA((2,)),
        ],
    )
    return pl.pallas_call(
        _combine_kernel,
        grid_spec=grid_spec,
        out_shape=jax.ShapeDtypeStruct((T, D), F32),
        compiler_params=_cparams(("arbitrary",)),
        name="combine",
    )(src_table, x1, mod3, wts, yb)


def _rotate_half_matrix():
    half = MLA_ROPE // 2
    p = np.zeros((MLA_ROPE, MLA_ROPE), np.float32)
    for j in range(half):
        p[j + half, j] = -1.0
        p[j, j + half] = 1.0
    return jnp.asarray(p)


def _slot_gain(g, factor):
    half = MLA_ROPE // 2
    g_n, g_r = g[:MLA_NOPE], g[MLA_NOPE:]
    slot = jnp.concatenate([g_n, g_r, g_r[half:], g_r[:half]]) * factor
    return jnp.tile(slot, MLA_HEADS)[None, :].astype(F32)


def _prepare_weights(w_in, wq_up, wkv_up, qn_g, kn_g):
    rot = _rotate_half_matrix()
    s0 = MLA_Q_LORA
    s1 = s0 + MLA_KV_LORA
    s2 = s1 + MLA_ROPE
    w_kr = w_in[:, s1:s2]
    kr_slot = jnp.concatenate([jnp.zeros((D_MODEL, MLA_NOPE), F32), w_kr, w_kr @ rot], axis=1)
    win_p = jnp.concatenate([w_in[:, :s1], kr_slot, w_in[:, s2:]], axis=1).astype(BF16)

    wq = wq_up.reshape(MLA_Q_LORA, MLA_HEADS, MLA_QK)
    wq_r = wq[:, :, MLA_NOPE:]
    wq_ext = jnp.concatenate([wq, jnp.einsum('lhr,rs->lhs', wq_r, rot)], axis=2)
    wq_ext = wq_ext.reshape(MLA_Q_LORA, SLOT_W).astype(BF16)

    wkv = wkv_up.reshape(MLA_KV_LORA, MLA_HEADS, MLA_NOPE + MLA_V)
    wk_slot = jnp.concatenate([wkv[:, :, :MLA_NOPE], jnp.zeros((MLA_KV_LORA, MLA_HEADS, LANES - MLA_NOPE), F32)],
                              axis=2).reshape(MLA_KV_LORA, SLOT_W)
    wv = wkv[:, :, MLA_NOPE:].reshape(MLA_KV_LORA, V_W)
    wkv_ext = jnp.concatenate([wk_slot, wv], axis=1).astype(BF16)

    gq = _slot_gain(qn_g, MLA_QK ** -0.5 * math.log2(math.e))
    gk = _slot_gain(kn_g, 1.0)
    return win_p, wq_ext, wkv_ext, gq, gk


def _chunk_tables(lch, tm):
    nt = lch.shape[0]
    lchunks = _local_rows(tm) // ROW_CHUNK
    tch = EXPERT_TILE // ROW_CHUNK
    n_pad_steps = -(-(N_EXPERTS * (tch - 1)) // lchunks)
    n_steps = nt + n_pad_steps
    total = n_steps * lchunks
    n_tiles = total // tch

    nch = lch[:, 0, N_GROUPS:N_GROUPS + N_EXPERTS]
    lend = jnp.cumsum(nch, axis=1)
    lstart = lend - nch
    ltot = lend[:, -1]
    tot = jnp.sum(nch, axis=0)
    ptot = (tot + tch - 1) // tch * tch
    gend = jnp.cumsum(ptot)
    gstart = gend - ptot
    toff = jnp.cumsum(nch, axis=0) - nch
    c = jnp.arange(lchunks, dtype=I32)

    def lookup(idx, table):
        hit = idx[..., None] == jnp.arange(table.shape[-1], dtype=I32)
        return jnp.sum(jnp.where(hit, table, 0), axis=-1)

    e_of = jnp.minimum(jnp.sum(c[None, :, None] >= lend[:, None, :], axis=2), N_EXPERTS - 1)
    dest = lookup(e_of, (gstart[None, :] + toff - lstart)[:, None, :]) + c[None, :]
    valid = c[None, :] < ltot[:, None]
    src_table = jnp.where(valid, dest, 0).astype(I32)

    gap = jnp.concatenate([ptot - tot, total - gend[-1:]])
    gap_first = jnp.concatenate([gstart + tot, gend[-1:]])
    gap_cum = jnp.cumsum(gap)
    unused = lchunks - ltot
    rank = jnp.concatenate([
        (jnp.cumsum(unused) - unused)[:, None] + c[None, :] - ltot[:, None],
        jnp.sum(unused) + jnp.arange(n_pad_steps, dtype=I32)[:, None] * lchunks + c[None, :]])
    g_of = jnp.minimum(jnp.sum(rank[:, :, None] >= gap_cum[None, None, :], axis=2), N_EXPERTS)
    filler = lookup(g_of, gap_first - (gap_cum - gap)) + rank
    routed = jnp.concatenate([valid, jnp.zeros((n_pad_steps, lchunks), bool)])
    dst_table = jnp.where(routed, jnp.concatenate([dest, jnp.zeros((n_pad_steps, lchunks), I32)]), filler).astype(I32)

    ts = jnp.arange(n_tiles, dtype=I32) * tch
    tile_expert = jnp.minimum(jnp.sum(ts[:, None] >= gend[None, :], axis=1), N_EXPERTS - 1).astype(I32)
    tile_valid = (ts < gend[-1]).astype(I32)
    return dst_table, src_table, tile_expert, tile_valid, total * ROW_CHUNK


def _pick(n, pref):
    return pref if n % pref == 0 else n


def kernel(x, c, positions, ada_w, ada_b, norm1_g, w_in, mla_qa_g, mla_wq_up, mla_kva_g, mla_wkv_up, mla_qn_g, mla_kn_g, hg_lb_logits, hg_norm_g, w_out, norm2_g, router_group_w, router_group_b, router_expert_w, router_expert_b, w_gate, w_up, w_down):
    B, S, D = x.shape
    assert D == D_MODEL and ada_w.shape[0] == 1, "one layer of width 1024"
    T = B * S
    tm = _pick(S, 512)
    tiles_per_batch = S // tm

    mod3 = _mod_call(c, ada_w[0], ada_b).reshape(B, 6, D)
    ctab, stab = _rope_call(positions, tm)

    win_p, wq_ext, wkv_ext, gq, gk = _prepare_weights(w_in[0], mla_wq_up[0], mla_wkv_up[0], mla_qn_g[0], mla_kn_g[0])
    q_hm, k_hm, vt, hg = _proj_call(x, mod3, norm1_g, win_p, mla_qa_g, wq_ext, mla_kva_g, wkv_ext, gq, gk,
                                        ctab, stab, tm)
    attn_t = _attn_call(q_hm, k_hm, vt, tm)
    rec = _hgrn_call(hg, hg_lb_logits.reshape(2 * hg_lb_logits.shape[1], HG_WIDTH), hg_norm_g, hps=2)

    wo = w_out[0].astype(BF16)
    wr = jnp.concatenate([router_group_w[0], router_expert_w[0],
                          jnp.zeros((D, ROUTE_ROWS - N_GROUPS - N_EXPERTS), F32)], axis=1).astype(BF16)
    br = jnp.concatenate([router_group_b[0], router_expert_b[0],
                          jnp.zeros((ROUTE_ROWS - N_GROUPS - N_EXPERTS,), F32)])[None, :]
    x1, h2, info, wts, lch = _router_call(attn_t, rec.reshape(T, HG_WIDTH), x.reshape(T, D), mod3,
                                          wo[:V_W], wo[V_W:], norm2_g, wr, br, tm, tiles_per_batch)

    dst_table, src_table, tile_expert, tile_valid, cap_rows = _chunk_tables(lch, tm)
    xb = _dispatch_call(dst_table, info, h2, cap_rows, tm)
    yb = _expert_call(tile_expert, tile_valid, xb, w_gate[0], w_up[0], w_down[0])
    out = _combine_call(src_table, x1, mod3, wts, yb, tm, tiles_per_batch)
    return out.reshape(B, S, D)
```

```python
import functools
import math

import numpy as np
import jax
import jax.numpy as jnp
from jax import lax
from jax.experimental import pallas as pl
from jax.experimental.pallas import tpu as pltpu

F32 = jnp.float32
BF16 = jnp.bfloat16
I32 = jnp.int32

D_MODEL = 1024
MLA_HEADS = 8
MLA_NOPE = 64
MLA_ROPE = 32
MLA_QK = MLA_NOPE + MLA_ROPE
MLA_V = 64
MLA_Q_LORA = 384
MLA_KV_LORA = 256
ROPE_THETA = 10000.0
HG_HEADS = 4
HG_DIM = 128
HG_WIDTH = HG_HEADS * HG_DIM
HG_CHUNK = 64
N_GROUPS = 4
EPG = 8
N_EXPERTS = N_GROUPS * EPG
TOP_K = 2
D_EXPERT = 512
EPS = 1e-6

LANES = 128
VMEM_LIMIT = 56 * 1024 * 1024

SLOT_W = MLA_HEADS * LANES
V_W = MLA_HEADS * MLA_V
HG_SEGS = 5
W_IN_COLS = MLA_Q_LORA + MLA_KV_LORA + LANES + HG_SEGS * HG_WIDTH
HG_GROUP_ROWS = 4 * HG_CHUNK
ONES_ROWS = 16
VT_ROWS = MLA_V + ONES_ROWS
ROUTE_ROWS = 128
EXPERT_TILE = 512
MXU_DIM = 256
ROW_CHUNK = 8


def _cparams(sem, vmem=VMEM_LIMIT):
    return pltpu.CompilerParams(dimension_semantics=sem, vmem_limit_bytes=vmem)


def _rms(x):
    return x * lax.rsqrt(jnp.mean(x * x, axis=-1, keepdims=True) + EPS)


def _silu(x):
    return x * jax.nn.sigmoid(x)


def _dot(a, b):
    return jnp.dot(a, b, preferred_element_type=F32)


def _dot_nt(a, b):
    return lax.dot_general(a, b, (((1,), (1,)), ((), ())), preferred_element_type=F32)


def _dot_tn(a, b):
    return lax.dot_general(a, b, (((0,), (0,)), ((), ())), preferred_element_type=F32)


def _mod_kernel(c_ref, w_ref, b_ref, o_ref):
    a = _silu(c_ref[...]).astype(BF16)
    o_ref[...] = _dot(a, w_ref[...].astype(BF16)) + b_ref[...]


def _mod_call(c, ada_w, ada_b):
    B = c.shape[0]
    n = ada_w.shape[1] // D_MODEL
    return pl.pallas_call(
        _mod_kernel,
        grid=(n,),
        in_specs=[
            pl.BlockSpec((B, D_MODEL), lambda j: (0, 0)),
            pl.BlockSpec((D_MODEL, D_MODEL), lambda j: (0, j)),
            pl.BlockSpec((1, D_MODEL), lambda j: (0, j)),
        ],
        out_specs=pl.BlockSpec((B, D_MODEL), lambda j: (0, j)),
        out_shape=jax.ShapeDtypeStruct((B, n * D_MODEL), F32),
        compiler_params=_cparams(("arbitrary",)),
        name="mod",
    )(c, ada_w, ada_b)


def _rope_tables(pos_row):
    half = MLA_ROPE // 2
    tm = pos_row.shape[1]
    idx = lax.broadcasted_iota(I32, (half, tm), 0).astype(F32)
    inv_freq = ROPE_THETA ** (-idx / half)
    ang = pos_row.astype(F32) * inv_freq
    c = jnp.cos(ang)
    s = jnp.sin(ang)
    ct = jnp.concatenate([jnp.ones((MLA_NOPE, tm), F32), c, c, jnp.zeros((MLA_ROPE, tm), F32)], axis=0)
    st = jnp.concatenate([jnp.zeros((MLA_QK, tm), F32), s, s], axis=0)
    return ct.T, st.T


def _head_finish(t, g_row, ct, st, lane_valid):
    ss = jnp.sum(jnp.where(lane_valid, t * t, 0.0), axis=-1, keepdims=True) * (1.0 / MLA_QK)
    tg = t * lax.rsqrt(ss + EPS) * g_row
    return tg * ct + pltpu.roll(tg * st, LANES - MLA_ROPE, 1)


def _proj_kernel(x_ref, mod_ref, g1_ref, win_ref, qag_ref, wq_ref, kvag_ref, wkv_ref, gq_ref, gk_ref,
                 pos_ref, q_out, k_out, vt_out, hg_out):
    x = x_ref[0]
    mod = mod_ref[0]
    h = _rms(x) * g1_ref[...]
    h = h * (1.0 + mod[1:2]) + mod[0:1]
    hb = h.astype(BF16)

    ct, st = _rope_tables(pos_ref[0])
    tm = x.shape[0]
    lane_valid = lax.broadcasted_iota(I32, (tm, LANES), 1) < MLA_QK

    c0 = MLA_Q_LORA
    c1 = c0 + MLA_KV_LORA
    c2 = c1 + LANES

    q_lat = _dot(hb, win_ref[:, 0:c0])
    qn = (_rms(q_lat) * qag_ref[...]).astype(BF16)
    qe = _dot(qn, wq_ref[...])
    for hd in range(MLA_HEADS):
        sl = slice(hd * LANES, (hd + 1) * LANES)
        q_out[0, hd] = _head_finish(qe[:, sl], gq_ref[:, sl], ct, st, lane_valid).astype(BF16)

    kv_lat = _dot(hb, win_ref[:, c0:c1])
    kvn = (_rms(kv_lat) * kvag_ref[...]).astype(BF16)
    ke = _dot(kvn, wkv_ref[:, 0:SLOT_W])
    kr = _dot(hb, win_ref[:, c1:c2])
    for hd in range(MLA_HEADS):
        sl = slice(hd * LANES, (hd + 1) * LANES)
        k_out[0, hd] = _head_finish(ke[:, sl] + kr, gk_ref[:, sl], ct, st, lane_valid).astype(BF16)

    v_t = _dot(kvn, wkv_ref[:, SLOT_W:SLOT_W + V_W]).T.astype(BF16)
    ones = jnp.ones((ONES_ROWS, tm), BF16)
    for hd in range(MLA_HEADS):
        vt_out[0, hd, 0:MLA_V, :] = v_t[hd * MLA_V:(hd + 1) * MLA_V, :]
        vt_out[0, hd, MLA_V:VT_ROWS, :] = ones

    for sg in range(HG_SEGS):
        lo = c2 + sg * HG_WIDTH
        hg_out[0, :, sg * HG_WIDTH:(sg + 1) * HG_WIDTH] = _dot(hb, win_ref[:, lo:lo + HG_WIDTH]).astype(BF16)


def _proj_call(x, mod3, g1, win_p, qag, wq_ext, kvag, wkv_ext, gq, gk, positions, tm):
    B, S, D = x.shape
    nt = S // tm
    pos3 = positions.reshape(B, 1, S)
    full = lambda shape: pl.BlockSpec(shape, lambda b, i: (0,) * len(shape))
    return pl.pallas_call(
        _proj_kernel,
        grid=(B, nt),
        in_specs=[
            pl.BlockSpec((1, tm, D), lambda b, i: (b, i, 0)),
            pl.BlockSpec((1, 6, D), lambda b, i: (b, 0, 0)),
            full((1, D)),
            full((D, W_IN_COLS)),
            full((1, MLA_Q_LORA)),
            full((MLA_Q_LORA, SLOT_W)),
            full((1, MLA_KV_LORA)),
            full((MLA_KV_LORA, SLOT_W + V_W)),
            full((1, SLOT_W)),
            full((1, SLOT_W)),
            pl.BlockSpec((1, 1, tm), lambda b, i: (b, 0, i)),
        ],
        out_specs=[
            pl.BlockSpec((1, MLA_HEADS, tm, LANES), lambda b, i: (b, 0, i, 0)),
            pl.BlockSpec((1, MLA_HEADS, tm, LANES), lambda b, i: (b, 0, i, 0)),
            pl.BlockSpec((1, MLA_HEADS, VT_ROWS, tm), lambda b, i: (b, 0, 0, i)),
            pl.BlockSpec((1, tm, HG_SEGS * HG_WIDTH), lambda b, i: (b, i, 0)),
        ],
        out_shape=[
            jax.ShapeDtypeStruct((B, MLA_HEADS, S, LANES), BF16),
            jax.ShapeDtypeStruct((B, MLA_HEADS, S, LANES), BF16),
            jax.ShapeDtypeStruct((B, MLA_HEADS, VT_ROWS, S), BF16),
            jax.ShapeDtypeStruct((B, S, HG_SEGS * HG_WIDTH), BF16),
        ],
        compiler_params=_cparams(("arbitrary", "arbitrary")),
        name="proj",
    )(x, mod3, g1, win_p, qag, wq_ext, kvag, wkv_ext, gq, gk, pos3)


def _attn_kernel(q_ref, k_ref, vt_ref, o_ref, s_scr, p_scr, m_scr):
    tq = o_ref.shape[3]
    n_items = o_ref.shape[1] * MLA_HEADS

    def split(i):
        return i // MLA_HEADS, i % MLA_HEADS

    def qk(i, slot):
        qt, hd = split(i)
        q = q_ref[0, hd, pl.ds(pl.multiple_of(qt * tq, tq), tq), :]
        s_t = _dot_nt(k_ref[0, hd], q)
        s_scr[slot] = s_t
        m_scr[slot] = jnp.max(s_t, axis=0, keepdims=True)

    def softmax(slot):
        p_scr[slot] = jnp.exp2(s_scr[slot] - m_scr[slot]).astype(BF16)

    def pv(i, slot):
        qt, hd = split(i)
        o_t = _dot(vt_ref[0, hd], p_scr[slot])
        o = o_t[:MLA_V] / o_t[MLA_V:MLA_V + 1]
        o_ref[0, qt, pl.ds(pl.multiple_of(hd * MLA_V, MLA_V), MLA_V), :] = o.astype(BF16)

    qk(0, 0)
    qk(1, 1)
    softmax(0)

    def body(jj, carry):
        j = 2 * jj
        qk(j, 0)
        softmax(1)
        pv(j - 2, 0)
        qk(j + 1, 1)
        softmax(0)
        pv(j - 1, 1)
        return carry

    lax.fori_loop(1, n_items // 2, body, 0)
    softmax(1)
    pv(n_items - 2, 0)
    pv(n_items - 1, 1)


def _attn_call(q_hm, k_hm, vt, tq):
    B, _, S, _ = q_hm.shape
    nq = S // tq
    return pl.pallas_call(
        _attn_kernel,
        grid=(B,),
        in_specs=[
            pl.BlockSpec((1, MLA_HEADS, S, LANES), lambda b: (b, 0, 0, 0)),
            pl.BlockSpec((1, MLA_HEADS, S, LANES), lambda b: (b, 0, 0, 0)),
            pl.BlockSpec((1, MLA_HEADS, VT_ROWS, S), lambda b: (b, 0, 0, 0)),
        ],
        out_specs=pl.BlockSpec((1, nq, V_W, tq), lambda b: (b, 0, 0, 0)),
        out_shape=jax.ShapeDtypeStruct((B, nq, V_W, tq), BF16),
        scratch_shapes=[
            pltpu.VMEM((2, S, tq), F32),
            pltpu.VMEM((2, S, tq), BF16),
            pltpu.VMEM((2, 1, tq), F32),
        ],
        compiler_params=_cparams(("arbitrary",)),
        name="attn",
    )(q_hm, k_hm, vt)


def _hgrn_kernel(hq_ref, ff_ref, fb_ref, hi_ref, hgate_ref, lbl_ref, ng_ref, o_ref, of_scr, ob_scr, st_scr,
                 bc_scr, k_scr, qd_scr, att_scr, kv_scr, el_scr, *, hps):
    S = hq_ref.shape[1]
    C = HG_CHUNK
    G = HG_GROUP_ROWS
    cpg = G // C
    n_groups = S // G

    def lower_bound(l0, l1):
        m = jnp.maximum(l0, l1)
        e0 = jnp.exp(l0 - m)
        return e0 / (e0 + jnp.exp(l1 - m))

    lb_f = lower_bound(lbl_ref[0:1, :], lbl_ref[1:2, :])
    lb_b = lower_bound(lbl_ref[2:3, :], lbl_ref[3:4, :])

    rr = lax.broadcasted_iota(I32, (G, G), 0)
    cc = lax.broadcasted_iota(I32, (G, G), 1)
    same_chunk = (rr // C) == (cc // C)
    mask_f = same_chunk & (cc <= rr)
    mask_b = same_chunk & (cc >= rr)
    tri_f = jnp.where(mask_f, 1.0, 0.0).astype(BF16)
    tri_b = jnp.where(mask_b, 1.0, 0.0).astype(BF16)

    st_scr[...] = jnp.zeros_like(st_scr)
    W = hps * HG_DIM
    dirs = ((0, lb_f, ff_ref, of_scr, mask_f, tri_f), (1, lb_b, fb_ref, ob_scr, mask_b, tri_b))

    def rows_of(d, g):
        grp = g if d == 0 else n_groups - 1 - g
        return pl.ds(grp * G if isinstance(grp, int) else pl.multiple_of(grp * G, G), G)

    def stage_gates(g, slot):
        for d, lb, f_ref, _, _, tri in dirs:
            f = lb + (1.0 - lb) * jax.nn.sigmoid(f_ref[0, rows_of(d, g), :].astype(F32))
            lf = jnp.log(f)
            hi = lf.astype(BF16)
            lo = (lf - hi.astype(F32)).astype(BF16)
            both = _dot(tri, jnp.concatenate([hi, lo], axis=1))
            bc_scr[slot, d] = both[:, :W] + both[:, W:]
            k_scr[slot, d] = 1.0 - f

    def stage_decay(g, slot):
        for d, _, _, _, mask, _ in dirs:
            rows = rows_of(d, g)
            bc = bc_scr[slot, d]
            eb = jnp.exp(bc)
            qd = (_silu(hq_ref[0, rows, :].astype(F32)) * eb).astype(BF16)
            kd = k_scr[slot, d] * jnp.exp(-bc)
            kdb = kd.astype(BF16)
            qd_scr[slot, d] = qd
            v = hi_ref[0, rows, :]
            for j in range(hps):
                ls = slice(j * HG_DIM, (j + 1) * HG_DIM)
                ch = d * hps + j
                att_scr[slot, ch] = jnp.where(mask, _dot_nt(qd[:, ls], kdb[:, ls]), 0.0).astype(BF16)
                lasts = []
                for c in range(cpg):
                    cs = slice(c * C, (c + 1) * C)
                    last = (c + 1) * C - 1 if d == 0 else c * C
                    e_last = eb[last:last + 1, ls]
                    lasts.append(e_last)
                    k_rem = (kd[cs, ls] * e_last).astype(BF16)
                    kv_scr[slot, ch * cpg + c] = _dot_tn(v[cs, ls], k_rem)
                el_scr[slot, ch] = jnp.concatenate(lasts + [jnp.zeros((8 - cpg, HG_DIM), F32)], axis=0)

    def stage_state(g, slot):
        for d, _, _, o_scr, _, _ in dirs:
            rows = rows_of(d, g)
            qd = qd_scr[slot, d]
            v = hi_ref[0, rows, :]
            for j in range(hps):
                ls = slice(j * HG_DIM, (j + 1) * HG_DIM)
                ch = d * hps + j
                o_intra = _dot(att_scr[slot, ch], v[:, ls])
                el = el_scr[slot, ch]
                st_t = st_scr[ch]
                o_inter = [None] * cpg
                for c in (range(cpg) if d == 0 else reversed(range(cpg))):
                    o_inter[c] = _dot_nt(qd[c * C:(c + 1) * C, ls], st_t.astype(BF16))
                    st_t = st_t * el[c:c + 1, :] + kv_scr[slot, ch * cpg + c]
                st_scr[ch] = st_t
                o_scr[rows, ls] = o_intra + jnp.concatenate(o_inter, axis=0)

    def step(t, parity):
        if not isinstance(t, int) or t < n_groups:
            stage_gates(t, parity)
        if not isinstance(t, int) or 1 <= t <= n_groups:
            stage_decay(t - 1, 1 - parity)
        if not isinstance(t, int) or t >= 2:
            stage_state(t - 2, parity)

    n_steps = n_groups + 2
    head = min(2, n_steps)
    tail_start = max(head, n_groups)
    for t in range(head):
        step(t, t % 2)
    n_mid = tail_start - head
    if n_mid % 2 == 0 and n_mid > 4:
        def body(i, carry):
            t = 2 + 2 * i
            step(t, 0)
            step(t + 1, 1)
            return carry
        lax.fori_loop(0, n_mid // 2, body, 0)
    else:
        for t in range(head, tail_start):
            step(t, t % 2)
    for t in range(tail_start, n_steps):
        step(t, t % 2)

    for j in range(hps):
        ls = slice(j * HG_DIM, (j + 1) * HG_DIM)
        o = of_scr[:, ls] + ob_scr[:, ls]
        gate = _silu(hgate_ref[0, :, ls].astype(F32))
        o_ref[0, :, ls] = (_rms(o) * ng_ref[...] * gate).astype(BF16)


def _hgrn_call(hg, lb_logits, norm_g, hps):
    B, S, _ = hg.shape
    W = hps * HG_DIM
    nh = HG_HEADS // hps
    G = HG_GROUP_ROWS
    cpg = G // HG_CHUNK
    seg = lambda sg: pl.BlockSpec((1, S, W), lambda b, j, sg=sg: (b, 0, sg * nh + j))
    return pl.pallas_call(
        functools.partial(_hgrn_kernel, hps=hps),
        grid=(B, nh),
        in_specs=[
            seg(0), seg(1), seg(2), seg(3), seg(4),
            pl.BlockSpec((4, W), lambda b, j: (0, j)),
            pl.BlockSpec((1, HG_DIM), lambda b, j: (0, 0)),
        ],
        out_specs=pl.BlockSpec((1, S, W), lambda b, j: (b, 0, j)),
        out_shape=jax.ShapeDtypeStruct((B, S, HG_WIDTH), BF16),
        scratch_shapes=[
            pltpu.VMEM((S, W), F32),
            pltpu.VMEM((S, W), F32),
            pltpu.VMEM((2 * hps, HG_DIM, HG_DIM), F32),
            pltpu.VMEM((2, 2, G, W), F32),
            pltpu.VMEM((2, 2, G, W), F32),
            pltpu.VMEM((2, 2, G, W), BF16),
            pltpu.VMEM((2, 2 * hps, G, G), BF16),
            pltpu.VMEM((2, 2 * hps * cpg, HG_DIM, HG_DIM), F32),
            pltpu.VMEM((2, 2 * hps, 8, HG_DIM), F32),
        ],
        compiler_params=_cparams(("arbitrary", "arbitrary")),
        name="hgrn",
    )(hg, hg, hg, hg, hg, lb_logits, norm_g)


def _router_kernel(attn_ref, rec_ref, x_ref, mod_ref, woa_ref, wor_ref, g2_ref, wr_ref, br_ref,
                   x1_ref, h2_ref, info_ref, wts_ref, lch_ref):
    mod = mod_ref[0]
    mixed = _dot_tn(attn_ref[0, 0], woa_ref[...]) + _dot(rec_ref[...], wor_ref[...])
    x1 = x_ref[...] + mod[2:3] * mixed
    x1_ref[...] = x1
    h2 = _rms(x1) * g2_ref[...]
    h2 = (h2 * (1.0 + mod[4:5]) + mod[3:4]).astype(BF16)
    h2_ref[...] = h2

    logits = _dot(h2, wr_ref[...]) + br_ref[...]
    lt = logits.T
    tm = lt.shape[1]
    r = lax.broadcasted_iota(I32, (ROUTE_ROWS, tm), 0)
    neg = -jnp.inf
    big = ROUTE_ROWS

    isg = r < N_GROUPS
    gmax = jnp.max(jnp.where(isg, lt, neg), axis=0, keepdims=True)
    gidx = jnp.min(jnp.where(isg & (lt == gmax), r, big), axis=0, keepdims=True)
    gp = 1.0 / jnp.sum(jnp.where(isg, jnp.exp(lt - gmax), 0.0), axis=0, keepdims=True)

    ing = (r >= N_GROUPS) & (r < N_GROUPS + N_EXPERTS) & (((r - N_GROUPS) >> 3) == gidx)
    v1 = jnp.max(jnp.where(ing, lt, neg), axis=0, keepdims=True)
    i1 = jnp.min(jnp.where(ing & (lt == v1), r, big), axis=0, keepdims=True)
    ing2 = ing & (r != i1)
    v2 = jnp.max(jnp.where(ing2, lt, neg), axis=0, keepdims=True)
    i2 = jnp.min(jnp.where(ing2 & (lt == v2), r, big), axis=0, keepdims=True)
    e21 = jnp.exp(v2 - v1)
    w1 = gp / (1.0 + e21)
    w2 = gp * e21 / (1.0 + e21)

    oh1 = r == i1
    oh2 = r == i2
    oh = jnp.where(oh1 | oh2, 1.0, 0.0)
    ohb = oh.astype(BF16)
    tt = lax.broadcasted_iota(I32, (tm, tm), 0)
    tc = lax.broadcasted_iota(I32, (tm, tm), 1)
    upper = jnp.where(tt < tc, 1.0, 0.0).astype(BF16)
    before = _dot(ohb, upper)
    cnt_col = jnp.sum(oh, axis=1, keepdims=True)
    pad_col = jnp.ceil(cnt_col * (1.0 / ROW_CHUNK)) * ROW_CHUNK
    er = lax.broadcasted_iota(I32, (ROUTE_ROWS, ROUTE_ROWS), 0)
    ec = lax.broadcasted_iota(I32, (ROUTE_ROWS, ROUTE_ROWS), 1)
    lower = jnp.where(ec < er, 1.0, 0.0).astype(BF16)
    start_col = _dot(lower, jnp.broadcast_to(pad_col, (ROUTE_ROWS, ROUTE_ROWS)).astype(BF16))[:, 0:1]
    pos = before + start_col
    slot1 = jnp.sum(jnp.where(oh1, pos, 0.0), axis=0, keepdims=True)
    slot2 = jnp.sum(jnp.where(oh2, pos, 0.0), axis=0, keepdims=True)

    zero = jnp.zeros((1, tm), I32)
    info_ref[0] = jnp.concatenate(
        [slot1.astype(I32), slot2.astype(I32), i1 - N_GROUPS, i2 - N_GROUPS, zero, zero, zero, zero], axis=0)
    rows = [jnp.broadcast_to(v, (LANES, tm)) for v in (w1, w2, slot1, slot2)]
    wts_ref[...] = jnp.concatenate(rows, axis=0).T
    cnt_row = _dot_nt(jnp.ones((8, tm), BF16), ohb)
    lch_ref[0] = jnp.ceil(cnt_row * (1.0 / ROW_CHUNK)).astype(I32)


def _router_call(attn_t, rec, x2, mod3, wo_a, wo_r, g2, wr, br, tm, tiles_per_batch):
    T, D = x2.shape
    nt = T // tm
    full = lambda shape: pl.BlockSpec(shape, lambda i: (0,) * len(shape))
    return pl.pallas_call(
        _router_kernel,
        grid=(nt,),
        in_specs=[
            pl.BlockSpec((1, 1, V_W, tm), lambda i: (i // tiles_per_batch, i % tiles_per_batch, 0, 0)),
            pl.BlockSpec((tm, HG_WIDTH), lambda i: (i, 0)),
            pl.BlockSpec((tm, D), lambda i: (i, 0)),
            pl.BlockSpec((1, 6, D), lambda i: (i // tiles_per_batch, 0, 0)),
            full((V_W, D)),
            full((HG_WIDTH, D)),
            full((1, D)),
            full((D, ROUTE_ROWS)),
            full((1, ROUTE_ROWS)),
        ],
        out_specs=[
            pl.BlockSpec((tm, D), lambda i: (i, 0)),
            pl.BlockSpec((tm, D), lambda i: (i, 0)),
            pl.BlockSpec((1, 8, tm), lambda i: (i, 0, 0)),
            pl.BlockSpec((tm, 4 * LANES), lambda i: (i, 0)),
            pl.BlockSpec((1, 8, ROUTE_ROWS), lambda i: (i, 0, 0)),
        ],
        out_shape=[
            jax.ShapeDtypeStruct((T, D), F32),
            jax.ShapeDtypeStruct((T, D), BF16),
            jax.ShapeDtypeStruct((nt, 8, tm), I32),
            jax.ShapeDtypeStruct((T, 4 * LANES), F32),
            jax.ShapeDtypeStruct((nt, 8, ROUTE_ROWS), I32),
        ],
        compiler_params=_cparams(("arbitrary",)),
        name="router",
    )(attn_t, rec, x2, mod3, wo_a, wo_r, g2, wr, br)


def _local_rows(tm):
    need = TOP_K * tm + N_EXPERTS * (ROW_CHUNK - 1)
    return -(-need // MXU_DIM) * MXU_DIM


def _dispatch_kernel(dst_ref, info_ref, h2_ref, xb_ref, xs_scr, sem, *, n_real):
    i = pl.program_id(0)
    n = pl.num_programs(0)
    slot = i % 2
    L = xs_scr.shape[1]
    tm = h2_ref.shape[0]

    def whole(sl):
        return pltpu.make_async_copy(xs_scr.at[sl], xb_ref.at[pl.ds(0, L)], sem.at[sl])

    @pl.when(i >= 2)
    def _():
        whole(slot).wait()

    @pl.when(i < n_real)
    def _():
        s = lax.broadcasted_iota(I32, (L, tm), 0)
        hit = (s == info_ref[0, 0:1, :]) | (s == info_ref[0, 1:2, :])
        xs_scr[slot] = _dot(jnp.where(hit, 1.0, 0.0).astype(BF16), h2_ref[...])

    @pl.when(i >= n_real)
    def _():
        xs_scr[slot] = jnp.zeros((L, xs_scr.shape[2]), F32)

    for c in range(L // ROW_CHUNK):
        d = pl.multiple_of(dst_ref[i, c] * ROW_CHUNK, ROW_CHUNK)
        pltpu.make_async_copy(xs_scr.at[slot, pl.ds(c * ROW_CHUNK, ROW_CHUNK)],
                              xb_ref.at[pl.ds(d, ROW_CHUNK)], sem.at[slot]).start()

    @pl.when(i == n - 1)
    def _():
        whole(slot).wait()
        whole(1 - slot).wait()


def _dispatch_call(dst_table, info, h2, cap_rows, tm):
    T, D = h2.shape
    nt = T // tm
    n_steps = dst_table.shape[0]
    L = _local_rows(tm)
    last = nt - 1
    grid_spec = pltpu.PrefetchScalarGridSpec(
        num_scalar_prefetch=1,
        grid=(n_steps,),
        in_specs=[
            pl.BlockSpec((1, 8, tm), lambda i, dst: (jnp.minimum(i, last), 0, 0)),
            pl.BlockSpec((tm, D), lambda i, dst: (jnp.minimum(i, last), 0)),
        ],
        out_specs=pl.BlockSpec(memory_space=pl.ANY),
        scratch_shapes=[
            pltpu.VMEM((2, L, D), F32),
            pltpu.SemaphoreType.DMA((2,)),
        ],
    )
    return pl.pallas_call(
        functools.partial(_dispatch_kernel, n_real=nt),
        grid_spec=grid_spec,
        out_shape=jax.ShapeDtypeStruct((cap_rows, D), F32),
        compiler_params=_cparams(("arbitrary",)),
        name="dispatch",
    )(dst_table, info, h2)


def _expert_kernel(te_ref, tv_ref, xb_ref, wg_ref, wu_ref, wd_ref, y_ref, wg_b, wu_b, wd_b):
    j = pl.program_id(0)
    prev = te_ref[jnp.maximum(j - 1, 0)]

    @pl.when((j == 0) | (te_ref[j] != prev))
    def _():
        wg_b[...] = wg_ref[0].astype(BF16)
        wu_b[...] = wu_ref[0].astype(BF16)
        wd_b[...] = wd_ref[0].astype(BF16)

    @pl.when(tv_ref[j] > 0)
    def _():
        x = xb_ref[...].astype(BF16)
        hid = (_silu(_dot(x, wg_b[...])) * _dot(x, wu_b[...])).astype(BF16)
        y_ref[...] = _dot(hid, wd_b[...])

    @pl.when(tv_ref[j] == 0)
    def _():
        y_ref[...] = jnp.zeros_like(y_ref)


def _expert_call(tile_expert, tile_valid, xb, w_gate, w_up, w_down):
    cap, D = xb.shape
    nt = tile_expert.shape[0]
    grid_spec = pltpu.PrefetchScalarGridSpec(
        num_scalar_prefetch=2,
        grid=(nt,),
        in_specs=[
            pl.BlockSpec((EXPERT_TILE, D), lambda j, te, tv: (j * tv[j], 0)),
            pl.BlockSpec((1, D, D_EXPERT), lambda j, te, tv: (te[j], 0, 0)),
            pl.BlockSpec((1, D, D_EXPERT), lambda j, te, tv: (te[j], 0, 0)),
            pl.BlockSpec((1, D_EXPERT, D), lambda j, te, tv: (te[j], 0, 0)),
        ],
        out_specs=pl.BlockSpec((EXPERT_TILE, D), lambda j, te, tv: (j, 0)),
        scratch_shapes=[
            pltpu.VMEM((D, D_EXPERT), BF16),
            pltpu.VMEM((D, D_EXPERT), BF16),
            pltpu.VMEM((D_EXPERT, D), BF16),
        ],
    )
    return pl.pallas_call(
        _expert_kernel,
        grid_spec=grid_spec,
        out_shape=jax.ShapeDtypeStruct((cap, D), F32),
        compiler_params=_cparams(("arbitrary",)),
        name="expert",
    )(tile_expert, tile_valid, xb, w_gate, w_up, w_down)


def _combine_kernel(src_ref, x1_ref, mod_ref, wts_ref, yb_ref, o_ref, ys_scr, sem):
    i = pl.program_id(0)
    n = pl.num_programs(0)
    slot = i % 2
    L = ys_scr.shape[1]
    tm = x1_ref.shape[0]

    def fetch(step, sl):
        for c in range(L // ROW_CHUNK):
            d = pl.multiple_of(src_ref[step, c] * ROW_CHUNK, ROW_CHUNK)
            pltpu.make_async_copy(yb_ref.at[pl.ds(d, ROW_CHUNK)],
                                  ys_scr.at[sl, pl.ds(c * ROW_CHUNK, ROW_CHUNK)], sem.at[sl]).start()

    @pl.when(i == 0)
    def _():
        fetch(0, 0)

    @pl.when(i + 1 < n)
    def _():
        fetch(i + 1, 1 - slot)

    pltpu.make_async_copy(yb_ref.at[pl.ds(0, L)], ys_scr.at[slot], sem.at[slot]).wait()

    s = lax.broadcasted_iota(I32, (tm, L), 1).astype(F32)
    w1 = wts_ref[:, 0:1]
    w2 = wts_ref[:, LANES:LANES + 1]
    slot1 = wts_ref[:, 2 * LANES:2 * LANES + 1]
    slot2 = wts_ref[:, 3 * LANES:3 * LANES + 1]
    pick = jnp.where(s == slot1, w1, 0.0) + jnp.where(s == slot2, w2, 0.0)
    moe = _dot(pick.astype(BF16), ys_scr[slot].astype(BF16))
    o_ref[...] = x1_ref[...] + mod_ref[0][5:6] * moe


def _combine_call(src_table, x1, mod3, wts, yb, tm, tiles_per_batch):
    T, D = x1.shape
    nt = T // tm
    L = _local_rows(tm)
    grid_spec = pltpu.PrefetchScalarGridSpec(
        num_scalar_prefetch=1,
        grid=(nt,),
        in_specs=[
            pl.BlockSpec((tm, D), lambda i, src: (i, 0)),
            pl.BlockSpec((1, 6, D), lambda i, src: (i // tiles_per_batch, 0, 0)),
            pl.BlockSpec((tm, 4 * LANES), lambda i, src: (i, 0)),
            pl.BlockSpec(memory_space=pl.ANY),
        ],
        out_specs=pl.BlockSpec((tm, D), lambda i, src: (i, 0)),
        scratch_shapes=[
            pltpu.VMEM((2, L, D), F32),
            pltpu.SemaphoreType.DMA((2,)),
        ],
    )
    return pl.pallas_call(
        _combine_kernel,
        grid_spec=grid_spec,
        out_shape=jax.ShapeDtypeStruct((T, D), F32),
        compiler_params=_cparams(("arbitrary",)),
        name="combine",
    )(src_table, x1, mod3, wts, yb)


def _rotate_half_matrix():
    half = MLA_ROPE // 2
    p = np.zeros((MLA_ROPE, MLA_ROPE), np.float32)
    for j in range(half):
        p[j + half, j] = -1.0
        p[j, j + half] = 1.0
    return jnp.asarray(p)


def _slot_gain(g, factor):
    half = MLA_ROPE // 2
    g_n, g_r = g[:MLA_NOPE], g[MLA_NOPE:]
    slot = jnp.concatenate([g_n, g_r, g_r[half:], g_r[:half]]) * factor
    return jnp.tile(slot, MLA_HEADS)[None, :].astype(F32)


def _prepare_weights(w_in, wq_up, wkv_up, qn_g, kn_g):
    rot = _rotate_half_matrix()
    s0 = MLA_Q_LORA
    s1 = s0 + MLA_KV_LORA
    s2 = s1 + MLA_ROPE
    w_kr = w_in[:, s1:s2]
    kr_slot = jnp.concatenate([jnp.zeros((D_MODEL, MLA_NOPE), F32), w_kr, w_kr @ rot], axis=1)
    win_p = jnp.concatenate([w_in[:, :s1], kr_slot, w_in[:, s2:]], axis=1).astype(BF16)

    wq = wq_up.reshape(MLA_Q_LORA, MLA_HEADS, MLA_QK)
    wq_r = wq[:, :, MLA_NOPE:]
    wq_ext = jnp.concatenate([wq, jnp.einsum('lhr,rs->lhs', wq_r, rot)], axis=2)
    wq_ext = wq_ext.reshape(MLA_Q_LORA, SLOT_W).astype(BF16)

    wkv = wkv_up.reshape(MLA_KV_LORA, MLA_HEADS, MLA_NOPE + MLA_V)
    wk_slot = jnp.concatenate([wkv[:, :, :MLA_NOPE], jnp.zeros((MLA_KV_LORA, MLA_HEADS, LANES - MLA_NOPE), F32)],
                              axis=2).reshape(MLA_KV_LORA, SLOT_W)
    wv = wkv[:, :, MLA_NOPE:].reshape(MLA_KV_LORA, V_W)
    wkv_ext = jnp.concatenate([wk_slot, wv], axis=1).astype(BF16)

    gq = _slot_gain(qn_g, MLA_QK ** -0.5 * math.log2(math.e))
    gk = _slot_gain(kn_g, 1.0)
    return win_p, wq_ext, wkv_ext, gq, gk


def _chunk_tables(lch, tm):
    nt = lch.shape[0]
    lchunks = _local_rows(tm) // ROW_CHUNK
    tch = EXPERT_TILE // ROW_CHUNK
    n_pad_steps = -(-(N_EXPERTS * (tch - 1)) // lchunks)
    while ((nt + n_pad_steps) * lchunks) % tch:
        n_pad_steps += 1
    n_steps = nt + n_pad_steps
    total = n_steps * lchunks
    n_tiles = total // tch

    nch = lch[:, 0, N_GROUPS:N_GROUPS + N_EXPERTS]
    lend = jnp.cumsum(nch, axis=1)
    lstart = lend - nch
    ltot = lend[:, -1]
    tot = jnp.sum(nch, axis=0)
    ptot = (tot + tch - 1) // tch * tch
    gend = jnp.cumsum(ptot)
    gstart = gend - ptot
    toff = jnp.cumsum(nch, axis=0) - nch
    c = jnp.arange(lchunks, dtype=I32)

    def lookup(idx, table):
        hit = idx[..., None] == jnp.arange(table.shape[-1], dtype=I32)
        return jnp.sum(jnp.where(hit, table, 0), axis=-1)

    e_of = jnp.minimum(jnp.sum(c[None, :, None] >= lend[:, None, :], axis=2), N_EXPERTS - 1)
    dest = lookup(e_of, (gstart[None, :] + toff - lstart)[:, None, :]) + c[None, :]
    valid = c[None, :] < ltot[:, None]
    src_table = jnp.where(valid, dest, 0).astype(I32)

    gap = jnp.concatenate([ptot - tot, total - gend[-1:]])
    gap_first = jnp.concatenate([gstart + tot, gend[-1:]])
    gap_cum = jnp.cumsum(gap)
    unused = lchunks - ltot
    rank = jnp.concatenate([
        (jnp.cumsum(unused) - unused)[:, None] + c[None, :] - ltot[:, None],
        jnp.sum(unused) + jnp.arange(n_pad_steps, dtype=I32)[:, None] * lchunks + c[None, :]])
    g_of = jnp.minimum(jnp.sum(rank[:, :, None] >= gap_cum[None, None, :], axis=2), N_EXPERTS)
    filler = lookup(g_of, gap_first - (gap_cum - gap)) + rank
    routed = jnp.concatenate([valid, jnp.zeros((n_pad_steps, lchunks), bool)])
    dst_table = jnp.where(routed, jnp.concatenate([dest, jnp.zeros((n_pad_steps, lchunks), I32)]), filler).astype(I32)

    ts = jnp.arange(n_tiles, dtype=I32) * tch
    tile_expert = jnp.minimum(jnp.sum(ts[:, None] >= gend[None, :], axis=1), N_EXPERTS - 1).astype(I32)
    tile_valid = (ts < gend[-1]).astype(I32)
    return dst_table, src_table, tile_expert, tile_valid, total * ROW_CHUNK


def _pick(n, pref):
    return pref if n % pref == 0 else n


def kernel(x, c, positions, ada_w, ada_b, norm1_g, w_in, mla_qa_g, mla_wq_up, mla_kva_g, mla_wkv_up, mla_qn_g, mla_kn_g, hg_lb_logits, hg_norm_g, w_out, norm2_g, router_group_w, router_group_b, router_expert_w, router_expert_b, w_gate, w_up, w_down):
    B, S, D = x.shape
    assert D == D_MODEL and ada_w.shape[0] == 1, "one layer of width 1024"
    T = B * S
    tm = _pick(S, 512)
    tiles_per_batch = S // tm

    mod3 = _mod_call(c, ada_w[0], ada_b).reshape(B, 6, D)

    win_p, wq_ext, wkv_ext, gq, gk = _prepare_weights(w_in[0], mla_wq_up[0], mla_wkv_up[0], mla_qn_g[0], mla_kn_g[0])
    q_hm, k_hm, vt, hg = _proj_call(x, mod3, norm1_g, win_p, mla_qa_g, wq_ext, mla_kva_g, wkv_ext, gq, gk,
                                    positions, tm)
    attn_t = _attn_call(q_hm, k_hm, vt, tm)
    rec = _hgrn_call(hg, hg_lb_logits.reshape(2 * hg_lb_logits.shape[1], HG_WIDTH), hg_norm_g, hps=2)

    wo = w_out[0].astype(BF16)
    wr = jnp.concatenate([router_group_w[0], router_expert_w[0],
                          jnp.zeros((D, ROUTE_ROWS - N_GROUPS - N_EXPERTS), F32)], axis=1).astype(BF16)
    br = jnp.concatenate([router_group_b[0], router_expert_b[0],
                          jnp.zeros((ROUTE_ROWS - N_GROUPS - N_EXPERTS,), F32)])[None, :]
    x1, h2, info, wts, lch = _router_call(attn_t, rec.reshape(T, HG_WIDTH), x.reshape(T, D), mod3,
                                          wo[:V_W], wo[V_W:], norm2_g, wr, br, tm, tiles_per_batch)

    dst_table, src_table, tile_expert, tile_valid, cap_rows = _chunk_tables(lch, tm)
    xb = _dispatch_call(dst_table, info, h2, cap_rows, tm)
    yb = _expert_call(tile_expert, tile_valid, xb, w_gate[0], w_up[0], w_down[0])
    out = _combine_call(src_table, x1, mod3, wts, yb, tm, tiles_per_batch)
    return out.reshape(B, S, D)
```

```python
import functools
import math

import numpy as np
import jax
import jax.numpy as jnp
from jax import lax
from jax.experimental import pallas as pl
from jax.experimental.pallas import tpu as pltpu

F32 = jnp.float32
BF16 = jnp.bfloat16
I32 = jnp.int32

D_MODEL = 1024
MLA_HEADS = 8
MLA_NOPE = 64
MLA_ROPE = 32
MLA_QK = MLA_NOPE + MLA_ROPE
MLA_V = 64
MLA_Q_LORA = 384
MLA_KV_LORA = 256
ROPE_THETA = 10000.0
HG_HEADS = 4
HG_DIM = 128
HG_WIDTH = HG_HEADS * HG_DIM
HG_CHUNK = 64
N_GROUPS = 4
EPG = 8
N_EXPERTS = N_GROUPS * EPG
TOP_K = 2
D_EXPERT = 512
EPS = 1e-6

LANES = 128
VMEM_LIMIT = 56 * 1024 * 1024

SLOT_W = MLA_HEADS * LANES
V_W = MLA_HEADS * MLA_V
HG_SEGS = 5
W_IN_COLS = MLA_Q_LORA + MLA_KV_LORA + LANES + HG_SEGS * HG_WIDTH
HG_GROUP_ROWS = 4 * HG_CHUNK
ONES_ROWS = 16
VT_ROWS = MLA_V + ONES_ROWS
ROUTE_ROWS = 128
EXPERT_TILE = 512
MXU_DIM = 256
ROW_CHUNK = 8
FOLD_CHUNK = 2 * ROW_CHUNK
COL_REP = LANES // 4


def _cparams(sem, vmem=VMEM_LIMIT):
    return pltpu.CompilerParams(dimension_semantics=sem, vmem_limit_bytes=vmem)


def _rms(x):
    return x * lax.rsqrt(jnp.mean(x * x, axis=-1, keepdims=True) + EPS)


def _silu(x):
    return x * jax.nn.sigmoid(x)


def _dot(a, b):
    return jnp.dot(a, b, preferred_element_type=F32)


def _dot_nt(a, b):
    return lax.dot_general(a, b, (((1,), (1,)), ((), ())), preferred_element_type=F32)


def _dot_tn(a, b):
    return lax.dot_general(a, b, (((0,), (0,)), ((), ())), preferred_element_type=F32)


def _fold_rows(v):
    r, c2 = v.shape
    c = c2 // 2
    left = v[:, :c].reshape(r // ROW_CHUNK, 1, ROW_CHUNK, c)
    right = v[:, c:].reshape(r // ROW_CHUNK, 1, ROW_CHUNK, c)
    return jnp.concatenate([left, right], axis=1).reshape(2 * r, c).astype(BF16)


def _unfold_rows(p):
    r2, c = p.shape
    r = r2 // 2
    q = p.astype(F32).reshape(r // ROW_CHUNK, 2, ROW_CHUNK, c)
    return q[:, 0].reshape(r, c).astype(BF16), q[:, 1].reshape(r, c).astype(BF16)


def _mod_kernel(c_ref, w_ref, b_ref, o_ref):
    a = _silu(c_ref[...]).astype(BF16)
    o_ref[...] = _dot(a, w_ref[...].astype(BF16)) + b_ref[...]


def _mod_call(c, ada_w, ada_b):
    B = c.shape[0]
    n = ada_w.shape[1] // D_MODEL
    return pl.pallas_call(
        _mod_kernel,
        grid=(n,),
        in_specs=[
            pl.BlockSpec((B, D_MODEL), lambda j: (0, 0)),
            pl.BlockSpec((D_MODEL, D_MODEL), lambda j: (0, j)),
            pl.BlockSpec((1, D_MODEL), lambda j: (0, j)),
        ],
        out_specs=pl.BlockSpec((B, D_MODEL), lambda j: (0, j)),
        out_shape=jax.ShapeDtypeStruct((B, n * D_MODEL), F32),
        compiler_params=_cparams(("arbitrary",)),
        name="mod",
    )(c, ada_w, ada_b)


def _rope_tables(pos_row):
    half = MLA_ROPE // 2
    tm = pos_row.shape[1]
    idx = lax.broadcasted_iota(I32, (half, tm), 0).astype(F32)
    inv_freq = ROPE_THETA ** (-idx / half)
    ang = pos_row.astype(F32) * inv_freq
    c = jnp.cos(ang)
    s = jnp.sin(ang)
    ct = jnp.concatenate([jnp.ones((MLA_NOPE, tm), F32), c, c, jnp.zeros((MLA_ROPE, tm), F32)], axis=0)
    st = jnp.concatenate([jnp.zeros((MLA_QK, tm), F32), s, s], axis=0)
    return ct.T, st.T


def _head_finish(t, g_row, ct, st, lane_valid):
    ss = jnp.sum(jnp.where(lane_valid, t * t, 0.0), axis=-1, keepdims=True) * (1.0 / MLA_QK)
    tg = t * lax.rsqrt(ss + EPS) * g_row
    return tg * ct + pltpu.roll(tg * st, LANES - MLA_ROPE, 1)


def _proj_kernel(x_ref, mod_ref, g1_ref, win_ref, qag_ref, wq_ref, kvag_ref, wkv_ref, gq_ref, gk_ref,
                 pos_ref, q_out, k_out, vt_out, hg_out):
    x = x_ref[0]
    mod = mod_ref[0]
    h = _rms(x) * g1_ref[...]
    h = h * (1.0 + mod[1:2]) + mod[0:1]
    hb = h.astype(BF16)

    ct, st = _rope_tables(pos_ref[0])
    tm = x.shape[0]
    lane_valid = lax.broadcasted_iota(I32, (tm, LANES), 1) < MLA_QK

    c0 = MLA_Q_LORA
    c1 = c0 + MLA_KV_LORA
    c2 = c1 + LANES

    q_lat = _dot(hb, win_ref[:, 0:c0])
    qn = (_rms(q_lat) * qag_ref[...]).astype(BF16)
    qe = _dot(qn, wq_ref[...])
    for hd in range(MLA_HEADS):
        sl = slice(hd * LANES, (hd + 1) * LANES)
        q_out[0, hd] = _head_finish(qe[:, sl], gq_ref[:, sl], ct, st, lane_valid).astype(BF16)

    kv_lat = _dot(hb, win_ref[:, c0:c1])
    kvn = (_rms(kv_lat) * kvag_ref[...]).astype(BF16)
    ke = _dot(kvn, wkv_ref[:, 0:SLOT_W])
    kr = _dot(hb, win_ref[:, c1:c2])
    for hd in range(MLA_HEADS):
        sl = slice(hd * LANES, (hd + 1) * LANES)
        k_out[0, hd] = _head_finish(ke[:, sl] + kr, gk_ref[:, sl], ct, st, lane_valid).astype(BF16)

    v_t = _dot(kvn, wkv_ref[:, SLOT_W:SLOT_W + V_W]).T.astype(BF16)
    ones = jnp.ones((ONES_ROWS, tm), BF16)
    for hd in range(MLA_HEADS):
        vt_out[0, hd, 0:MLA_V, :] = v_t[hd * MLA_V:(hd + 1) * MLA_V, :]
        vt_out[0, hd, MLA_V:VT_ROWS, :] = ones

    for sg in range(HG_SEGS):
        lo = c2 + sg * HG_WIDTH
        hg_out[0, :, sg * HG_WIDTH:(sg + 1) * HG_WIDTH] = _dot(hb, win_ref[:, lo:lo + HG_WIDTH]).astype(BF16)


def _proj_call(x, mod3, g1, win_p, qag, wq_ext, kvag, wkv_ext, gq, gk, positions, tm):
    B, S, D = x.shape
    nt = S // tm
    pos3 = positions.reshape(B, 1, S)
    full = lambda shape: pl.BlockSpec(shape, lambda b, i: (0,) * len(shape))
    return pl.pallas_call(
        _proj_kernel,
        grid=(B, nt),
        in_specs=[
            pl.BlockSpec((1, tm, D), lambda b, i: (b, i, 0)),
            pl.BlockSpec((1, 6, D), lambda b, i: (b, 0, 0)),
            full((1, D)),
            full((D, W_IN_COLS)),
            full((1, MLA_Q_LORA)),
            full((MLA_Q_LORA, SLOT_W)),
            full((1, MLA_KV_LORA)),
            full((MLA_KV_LORA, SLOT_W + V_W)),
            full((1, SLOT_W)),
            full((1, SLOT_W)),
            pl.BlockSpec((1, 1, tm), lambda b, i: (b, 0, i)),
        ],
        out_specs=[
            pl.BlockSpec((1, MLA_HEADS, tm, LANES), lambda b, i: (b, 0, i, 0)),
            pl.BlockSpec((1, MLA_HEADS, tm, LANES), lambda b, i: (b, 0, i, 0)),
            pl.BlockSpec((1, MLA_HEADS, VT_ROWS, tm), lambda b, i: (b, 0, 0, i)),
            pl.BlockSpec((1, tm, HG_SEGS * HG_WIDTH), lambda b, i: (b, i, 0)),
        ],
        out_shape=[
            jax.ShapeDtypeStruct((B, MLA_HEADS, S, LANES), BF16),
            jax.ShapeDtypeStruct((B, MLA_HEADS, S, LANES), BF16),
            jax.ShapeDtypeStruct((B, MLA_HEADS, VT_ROWS, S), BF16),
            jax.ShapeDtypeStruct((B, S, HG_SEGS * HG_WIDTH), BF16),
        ],
        compiler_params=_cparams(("arbitrary", "arbitrary")),
        name="proj",
    )(x, mod3, g1, win_p, qag, wq_ext, kvag, wkv_ext, gq, gk, pos3)


def _attn_kernel(q_ref, k_ref, vt_ref, o_ref, s_scr, p_scr, m_scr):
    tq = o_ref.shape[3]
    n_items = o_ref.shape[1] * MLA_HEADS

    def split(i):
        return i // MLA_HEADS, i % MLA_HEADS

    def qk(i, slot):
        qt, hd = split(i)
        q = q_ref[0, hd, pl.ds(pl.multiple_of(qt * tq, tq), tq), :]
        s_t = _dot_nt(k_ref[0, hd], q)
        s_scr[slot] = s_t
        m_scr[slot] = jnp.max(s_t, axis=0, keepdims=True)

    def softmax(slot):
        p_scr[slot] = jnp.exp2(s_scr[slot] - m_scr[slot]).astype(BF16)

    def pv(i, slot):
        qt, hd = split(i)
        o_t = _dot(vt_ref[0, hd], p_scr[slot])
        o = o_t[:MLA_V] / o_t[MLA_V:MLA_V + 1]
        o_ref[0, qt, pl.ds(pl.multiple_of(hd * MLA_V, MLA_V), MLA_V), :] = o.astype(BF16)

    qk(0, 0)
    qk(1, 1)
    softmax(0)

    def body(jj, carry):
        j = 2 * jj
        qk(j, 0)
        softmax(1)
        pv(j - 2, 0)
        qk(j + 1, 1)
        softmax(0)
        pv(j - 1, 1)
        return carry

    lax.fori_loop(1, n_items // 2, body, 0)
    softmax(1)
    pv(n_items - 2, 0)
    pv(n_items - 1, 1)


def _attn_call(q_hm, k_hm, vt, tq):
    B, _, S, _ = q_hm.shape
    nq = S // tq
    return pl.pallas_call(
        _attn_kernel,
        grid=(B,),
        in_specs=[
            pl.BlockSpec((1, MLA_HEADS, S, LANES), lambda b: (b, 0, 0, 0)),
            pl.BlockSpec((1, MLA_HEADS, S, LANES), lambda b: (b, 0, 0, 0)),
            pl.BlockSpec((1, MLA_HEADS, VT_ROWS, S), lambda b: (b, 0, 0, 0)),
        ],
        out_specs=pl.BlockSpec((1, nq, V_W, tq), lambda b: (b, 0, 0, 0)),
        out_shape=jax.ShapeDtypeStruct((B, nq, V_W, tq), BF16),
        scratch_shapes=[
            pltpu.VMEM((2, S, tq), F32),
            pltpu.VMEM((2, S, tq), BF16),
            pltpu.VMEM((2, 1, tq), F32),
        ],
        compiler_params=_cparams(("arbitrary",)),
        name="attn",
    )(q_hm, k_hm, vt)


def _hgrn_kernel(hq_ref, ff_ref, fb_ref, hi_ref, hgate_ref, lbl_ref, ng_ref, o_ref, of_scr, ob_scr, st_scr,
                 bc_scr, k_scr, qd_scr, att_scr, kv_scr, el_scr, *, hps):
    S = hq_ref.shape[1]
    C = HG_CHUNK
    G = HG_GROUP_ROWS
    cpg = G // C
    n_groups = S // G

    def lower_bound(l0, l1):
        m = jnp.maximum(l0, l1)
        e0 = jnp.exp(l0 - m)
        return e0 / (e0 + jnp.exp(l1 - m))

    lb_f = lower_bound(lbl_ref[0:1, :], lbl_ref[1:2, :])
    lb_b = lower_bound(lbl_ref[2:3, :], lbl_ref[3:4, :])

    rr = lax.broadcasted_iota(I32, (G, G), 0)
    cc = lax.broadcasted_iota(I32, (G, G), 1)
    same_chunk = (rr // C) == (cc // C)
    mask_f = same_chunk & (cc <= rr)
    mask_b = same_chunk & (cc >= rr)
    tri_f = jnp.where(mask_f, 1.0, 0.0).astype(BF16)
    tri_b = jnp.where(mask_b, 1.0, 0.0).astype(BF16)

    st_scr[...] = jnp.zeros_like(st_scr)
    W = hps * HG_DIM
    dirs = ((0, lb_f, ff_ref, of_scr, mask_f, tri_f), (1, lb_b, fb_ref, ob_scr, mask_b, tri_b))

    def rows_of(d, g):
        grp = g if d == 0 else n_groups - 1 - g
        return pl.ds(grp * G if isinstance(grp, int) else pl.multiple_of(grp * G, G), G)

    def stage_gates(g, slot):
        for d, lb, f_ref, _, _, tri in dirs:
            f = lb + (1.0 - lb) * jax.nn.sigmoid(f_ref[0, rows_of(d, g), :].astype(F32))
            lf = jnp.log(f)
            hi = lf.astype(BF16)
            lo = (lf - hi.astype(F32)).astype(BF16)
            both = _dot(tri, jnp.concatenate([hi, lo], axis=1))
            bc_scr[slot, d] = both[:, :W] + both[:, W:]
            k_scr[slot, d] = 1.0 - f

    def stage_decay(g, slot):
        for d, _, _, _, mask, _ in dirs:
            rows = rows_of(d, g)
            bc = bc_scr[slot, d]
            eb = jnp.exp(bc)
            qd = (_silu(hq_ref[0, rows, :].astype(F32)) * eb).astype(BF16)
            kd = k_scr[slot, d] * jnp.exp(-bc)
            kdb = kd.astype(BF16)
            qd_scr[slot, d] = qd
            v = hi_ref[0, rows, :]
            for j in range(hps):
                ls = slice(j * HG_DIM, (j + 1) * HG_DIM)
                ch = d * hps + j
                att_scr[slot, ch] = jnp.where(mask, _dot_nt(qd[:, ls], kdb[:, ls]), 0.0).astype(BF16)
                lasts = []
                for c in range(cpg):
                    cs = slice(c * C, (c + 1) * C)
                    last = (c + 1) * C - 1 if d == 0 else c * C
                    e_last = eb[last:last + 1, ls]
                    lasts.append(e_last)
                    k_rem = (kd[cs, ls] * e_last).astype(BF16)
                    kv_scr[slot, ch * cpg + c] = _dot_tn(v[cs, ls], k_rem)
                el_scr[slot, ch] = jnp.concatenate(lasts + [jnp.zeros((8 - cpg, HG_DIM), F32)], axis=0)

    def stage_state(g, slot):
        for d, _, _, o_scr, _, _ in dirs:
            rows = rows_of(d, g)
            qd = qd_scr[slot, d]
            v = hi_ref[0, rows, :]
            for j in range(hps):
                ls = slice(j * HG_DIM, (j + 1) * HG_DIM)
                ch = d * hps + j
                o_intra = _dot(att_scr[slot, ch], v[:, ls])
                el = el_scr[slot, ch]
                st_t = st_scr[ch]
                o_inter = [None] * cpg
                for c in (range(cpg) if d == 0 else reversed(range(cpg))):
                    o_inter[c] = _dot_nt(qd[c * C:(c + 1) * C, ls], st_t.astype(BF16))
                    st_t = st_t * el[c:c + 1, :] + kv_scr[slot, ch * cpg + c]
                st_scr[ch] = st_t
                o_scr[rows, ls] = o_intra + jnp.concatenate(o_inter, axis=0)

    def step(t, parity):
        if not isinstance(t, int) or t < n_groups:
            stage_gates(t, parity)
        if not isinstance(t, int) or 1 <= t <= n_groups:
            stage_decay(t - 1, 1 - parity)
        if not isinstance(t, int) or t >= 2:
            stage_state(t - 2, parity)

    n_steps = n_groups + 2
    head = min(2, n_steps)
    tail_start = max(head, n_groups)
    for t in range(head):
        step(t, t % 2)
    n_mid = tail_start - head
    if n_mid % 2 == 0 and n_mid > 4:
        def body(i, carry):
            t = 2 + 2 * i
            step(t, 0)
            step(t + 1, 1)
            return carry
        lax.fori_loop(0, n_mid // 2, body, 0)
    else:
        for t in range(head, tail_start):
            step(t, t % 2)
    for t in range(tail_start, n_steps):
        step(t, t % 2)

    for j in range(hps):
        ls = slice(j * HG_DIM, (j + 1) * HG_DIM)
        o = of_scr[:, ls] + ob_scr[:, ls]
        gate = _silu(hgate_ref[0, :, ls].astype(F32))
        o_ref[0, :, ls] = (_rms(o) * ng_ref[...] * gate).astype(BF16)


def _hgrn_call(hg, lb_logits, norm_g, hps):
    B, S, _ = hg.shape
    W = hps * HG_DIM
    nh = HG_HEADS // hps
    G = HG_GROUP_ROWS
    cpg = G // HG_CHUNK
    seg = lambda sg: pl.BlockSpec((1, S, W), lambda b, j, sg=sg: (b, 0, sg * nh + j))
    return pl.pallas_call(
        functools.partial(_hgrn_kernel, hps=hps),
        grid=(B, nh),
        in_specs=[
            seg(0), seg(1), seg(2), seg(3), seg(4),
            pl.BlockSpec((4, W), lambda b, j: (0, j)),
            pl.BlockSpec((1, HG_DIM), lambda b, j: (0, 0)),
        ],
        out_specs=pl.BlockSpec((1, S, W), lambda b, j: (b, 0, j)),
        out_shape=jax.ShapeDtypeStruct((B, S, HG_WIDTH), BF16),
        scratch_shapes=[
            pltpu.VMEM((S, W), F32),
            pltpu.VMEM((S, W), F32),
            pltpu.VMEM((2 * hps, HG_DIM, HG_DIM), F32),
            pltpu.VMEM((2, 2, G, W), F32),
            pltpu.VMEM((2, 2, G, W), F32),
            pltpu.VMEM((2, 2, G, W), BF16),
            pltpu.VMEM((2, 2 * hps, G, G), BF16),
            pltpu.VMEM((2, 2 * hps * cpg, HG_DIM, HG_DIM), F32),
            pltpu.VMEM((2, 2 * hps, 8, HG_DIM), F32),
        ],
        compiler_params=_cparams(("arbitrary", "arbitrary")),
        name="hgrn",
    )(hg, hg, hg, hg, hg, lb_logits, norm_g)


def _router_kernel(attn_ref, rec_ref, x_ref, mod_ref, woa_ref, wor_ref, g2_ref, wr_ref, br_ref,
                   x1_ref, h2_ref, info_ref, wts_ref, lch_ref):
    mod = mod_ref[0]
    mixed = _dot_tn(attn_ref[0, 0], woa_ref[...]) + _dot(rec_ref[...], wor_ref[...])
    x1 = x_ref[...] + mod[2:3] * mixed
    x1_ref[...] = x1
    h2 = _rms(x1) * g2_ref[...]
    h2 = (h2 * (1.0 + mod[4:5]) + mod[3:4]).astype(BF16)
    h2_ref[...] = h2

    logits = _dot(h2, wr_ref[...]) + br_ref[...]
    lt = logits.T
    tm = lt.shape[1]
    r = lax.broadcasted_iota(I32, (ROUTE_ROWS, tm), 0)
    neg = -jnp.inf
    big = ROUTE_ROWS

    isg = r < N_GROUPS
    gmax = jnp.max(jnp.where(isg, lt, neg), axis=0, keepdims=True)
    gidx = jnp.min(jnp.where(isg & (lt == gmax), r, big), axis=0, keepdims=True)
    gp = 1.0 / jnp.sum(jnp.where(isg, jnp.exp(lt - gmax), 0.0), axis=0, keepdims=True)

    ing = (r >= N_GROUPS) & (r < N_GROUPS + N_EXPERTS) & (((r - N_GROUPS) >> 3) == gidx)
    v1 = jnp.max(jnp.where(ing, lt, neg), axis=0, keepdims=True)
    i1 = jnp.min(jnp.where(ing & (lt == v1), r, big), axis=0, keepdims=True)
    ing2 = ing & (r != i1)
    v2 = jnp.max(jnp.where(ing2, lt, neg), axis=0, keepdims=True)
    i2 = jnp.min(jnp.where(ing2 & (lt == v2), r, big), axis=0, keepdims=True)
    e21 = jnp.exp(v2 - v1)
    w1 = gp / (1.0 + e21)
    w2 = gp * e21 / (1.0 + e21)

    oh1 = r == i1
    oh2 = r == i2
    oh = jnp.where(oh1 | oh2, 1.0, 0.0)
    ohb = oh.astype(BF16)
    tt = lax.broadcasted_iota(I32, (tm, tm), 0)
    tc = lax.broadcasted_iota(I32, (tm, tm), 1)
    upper = jnp.where(tt < tc, 1.0, 0.0).astype(BF16)
    before = _dot(ohb, upper)
    cnt_col = jnp.sum(oh, axis=1, keepdims=True)
    pad_col = jnp.ceil(cnt_col * (1.0 / ROW_CHUNK)) * ROW_CHUNK
    er = lax.broadcasted_iota(I32, (ROUTE_ROWS, ROUTE_ROWS), 0)
    ec = lax.broadcasted_iota(I32, (ROUTE_ROWS, ROUTE_ROWS), 1)
    lower = jnp.where(ec < er, 1.0, 0.0).astype(BF16)
    start_col = _dot(lower, jnp.broadcast_to(pad_col, (ROUTE_ROWS, ROUTE_ROWS)).astype(BF16))[:, 0:1]
    pos = before + start_col
    slot1 = jnp.sum(jnp.where(oh1, pos, 0.0), axis=0, keepdims=True)
    slot2 = jnp.sum(jnp.where(oh2, pos, 0.0), axis=0, keepdims=True)

    zero = jnp.zeros((1, tm), I32)
    info_ref[0] = jnp.concatenate(
        [slot1.astype(I32), slot2.astype(I32), i1 - N_GROUPS, i2 - N_GROUPS, zero, zero, zero, zero], axis=0)
    rows = [jnp.broadcast_to(v, (COL_REP, tm)) for v in (w1, w2, slot1, slot2)]
    wts_ref[...] = jnp.concatenate(rows, axis=0).T
    cnt_row = _dot_nt(jnp.ones((8, tm), BF16), ohb)
    lch_ref[0] = jnp.ceil(cnt_row * (1.0 / ROW_CHUNK)).astype(I32)


def _router_call(attn_t, rec, x2, mod3, wo_a, wo_r, g2, wr, br, tm, tiles_per_batch):
    T, D = x2.shape
    nt = T // tm
    full = lambda shape: pl.BlockSpec(shape, lambda i: (0,) * len(shape))
    return pl.pallas_call(
        _router_kernel,
        grid=(nt,),
        in_specs=[
            pl.BlockSpec((1, 1, V_W, tm), lambda i: (i // tiles_per_batch, i % tiles_per_batch, 0, 0)),
            pl.BlockSpec((tm, HG_WIDTH), lambda i: (i, 0)),
            pl.BlockSpec((tm, D), lambda i: (i, 0)),
            pl.BlockSpec((1, 6, D), lambda i: (i // tiles_per_batch, 0, 0)),
            full((V_W, D)),
            full((HG_WIDTH, D)),
            full((1, D)),
            full((D, ROUTE_ROWS)),
            full((1, ROUTE_ROWS)),
        ],
        out_specs=[
            pl.BlockSpec((tm, D), lambda i: (i, 0)),
            pl.BlockSpec((tm, D), lambda i: (i, 0)),
            pl.BlockSpec((1, 8, tm), lambda i: (i, 0, 0)),
            pl.BlockSpec((tm, LANES), lambda i: (i, 0)),
            pl.BlockSpec((1, 8, ROUTE_ROWS), lambda i: (i, 0, 0)),
        ],
        out_shape=[
            jax.ShapeDtypeStruct((T, D), F32),
            jax.ShapeDtypeStruct((T, D), BF16),
            jax.ShapeDtypeStruct((nt, 8, tm), I32),
            jax.ShapeDtypeStruct((T, LANES), F32),
            jax.ShapeDtypeStruct((nt, 8, ROUTE_ROWS), I32),
        ],
        compiler_params=_cparams(("arbitrary",)),
        name="router",
    )(attn_t, rec, x2, mod3, wo_a, wo_r, g2, wr, br)


def _local_rows(tm):
    need = TOP_K * tm + N_EXPERTS * (ROW_CHUNK - 1)
    return -(-need // MXU_DIM) * MXU_DIM


def _dispatch_kernel(dst_ref, info_ref, h2_ref, xb_ref, xs_scr, sem, *, n_real):
    i = pl.program_id(0)
    n = pl.num_programs(0)
    slot = i % 2
    LF = xs_scr.shape[1]
    L = LF // 2
    tm = h2_ref.shape[0]

    def whole(sl):
        return pltpu.make_async_copy(xs_scr.at[sl], xb_ref.at[pl.ds(0, LF)], sem.at[sl])

    @pl.when(i >= 2)
    def _():
        whole(slot).wait()

    @pl.when(i < n_real)
    def _():
        s = lax.broadcasted_iota(I32, (L, tm), 0)
        hit = (s == info_ref[0, 0:1, :]) | (s == info_ref[0, 1:2, :])
        xs_scr[slot] = _fold_rows(_dot(jnp.where(hit, 1.0, 0.0).astype(BF16), h2_ref[...]))

    @pl.when(i >= n_real)
    def _():
        xs_scr[slot] = jnp.zeros((LF, xs_scr.shape[2]), BF16)

    for c in range(LF // FOLD_CHUNK):
        d = pl.multiple_of(dst_ref[i, c] * FOLD_CHUNK, FOLD_CHUNK)
        pltpu.make_async_copy(xs_scr.at[slot, pl.ds(c * FOLD_CHUNK, FOLD_CHUNK)],
                              xb_ref.at[pl.ds(d, FOLD_CHUNK)], sem.at[slot]).start()

    @pl.when(i == n - 1)
    def _():
        whole(slot).wait()
        whole(1 - slot).wait()


def _dispatch_call(dst_table, info, h2, cap_rows, tm):
    T, D = h2.shape
    nt = T // tm
    n_steps = dst_table.shape[0]
    L = _local_rows(tm)
    last = nt - 1
    grid_spec = pltpu.PrefetchScalarGridSpec(
        num_scalar_prefetch=1,
        grid=(n_steps,),
        in_specs=[
            pl.BlockSpec((1, 8, tm), lambda i, dst: (jnp.minimum(i, last), 0, 0)),
            pl.BlockSpec((tm, D), lambda i, dst: (jnp.minimum(i, last), 0)),
        ],
        out_specs=pl.BlockSpec(memory_space=pl.ANY),
        scratch_shapes=[
            pltpu.VMEM((2, 2 * L, D // 2), BF16),
            pltpu.SemaphoreType.DMA((2,)),
        ],
    )
    return pl.pallas_call(
        functools.partial(_dispatch_kernel, n_real=nt),
        grid_spec=grid_spec,
        out_shape=jax.ShapeDtypeStruct((2 * cap_rows, D // 2), BF16),
        compiler_params=_cparams(("arbitrary",)),
        name="dispatch",
    )(dst_table, info, h2)


def _expert_kernel(te_ref, tv_ref, xb_ref, wg_ref, wu_ref, wd_ref, y_ref, wg_b, wu_b, wd_b):
    j = pl.program_id(0)
    prev = te_ref[jnp.maximum(j - 1, 0)]

    @pl.when((j == 0) | (te_ref[j] != prev))
    def _():
        wg_b[...] = wg_ref[0].astype(BF16)
        wu_b[...] = wu_ref[0].astype(BF16)
        wd_b[...] = wd_ref[0].astype(BF16)

    @pl.when(tv_ref[j] > 0)
    def _():
        x = jnp.concatenate(_unfold_rows(xb_ref[...]), axis=1)
        hid = (_silu(_dot(x, wg_b[...])) * _dot(x, wu_b[...])).astype(BF16)
        y_ref[...] = _fold_rows(_dot(hid, wd_b[...]))

    @pl.when(tv_ref[j] == 0)
    def _():
        y_ref[...] = jnp.zeros_like(y_ref)


def _expert_call(tile_expert, tile_valid, xb, w_gate, w_up, w_down):
    cap, DP = xb.shape
    D = 2 * DP
    nt = tile_expert.shape[0]
    grid_spec = pltpu.PrefetchScalarGridSpec(
        num_scalar_prefetch=2,
        grid=(nt,),
        in_specs=[
            pl.BlockSpec((2 * EXPERT_TILE, DP), lambda j, te, tv: (j * tv[j], 0)),
            pl.BlockSpec((1, D, D_EXPERT), lambda j, te, tv: (te[j], 0, 0)),
            pl.BlockSpec((1, D, D_EXPERT), lambda j, te, tv: (te[j], 0, 0)),
            pl.BlockSpec((1, D_EXPERT, D), lambda j, te, tv: (te[j], 0, 0)),
        ],
        out_specs=pl.BlockSpec((2 * EXPERT_TILE, DP), lambda j, te, tv: (j, 0)),
        scratch_shapes=[
            pltpu.VMEM((D, D_EXPERT), BF16),
            pltpu.VMEM((D, D_EXPERT), BF16),
            pltpu.VMEM((D_EXPERT, D), BF16),
        ],
    )
    return pl.pallas_call(
        _expert_kernel,
        grid_spec=grid_spec,
        out_shape=jax.ShapeDtypeStruct((cap, DP), BF16),
        compiler_params=_cparams(("arbitrary",)),
        name="expert",
    )(tile_expert, tile_valid, xb, w_gate, w_up, w_down)


def _combine_kernel(src_ref, x1_ref, mod_ref, wts_ref, yb_ref, o_ref, ys_scr, sem):
    i = pl.program_id(0)
    n = pl.num_programs(0)
    slot = i % 2
    LF = ys_scr.shape[1]
    L = LF // 2
    tm = x1_ref.shape[0]

    def fetch(step, sl):
        for c in range(LF // FOLD_CHUNK):
            d = pl.multiple_of(src_ref[step, c] * FOLD_CHUNK, FOLD_CHUNK)
            pltpu.make_async_copy(yb_ref.at[pl.ds(d, FOLD_CHUNK)],
                                  ys_scr.at[sl, pl.ds(c * FOLD_CHUNK, FOLD_CHUNK)], sem.at[sl]).start()

    @pl.when(i == 0)
    def _():
        fetch(0, 0)

    @pl.when(i + 1 < n)
    def _():
        fetch(i + 1, 1 - slot)

    pltpu.make_async_copy(yb_ref.at[pl.ds(0, LF)], ys_scr.at[slot], sem.at[slot]).wait()

    s = lax.broadcasted_iota(I32, (tm, L), 1).astype(F32)
    w1, w2, slot1, slot2 = (wts_ref[:, j * COL_REP:j * COL_REP + 1] for j in range(4))
    pick = (jnp.where(s == slot1, w1, 0.0) + jnp.where(s == slot2, w2, 0.0)).astype(BF16)
    y_left, y_right = _unfold_rows(ys_scr[slot])
    moe = jnp.concatenate([_dot(pick, y_left), _dot(pick, y_right)], axis=1)
    o_ref[...] = x1_ref[...] + mod_ref[0][5:6] * moe


def _combine_call(src_table, x1, mod3, wts, yb, tm, tiles_per_batch):
    T, D = x1.shape
    nt = T // tm
    L = _local_rows(tm)
    grid_spec = pltpu.PrefetchScalarGridSpec(
        num_scalar_prefetch=1,
        grid=(nt,),
        in_specs=[
            pl.BlockSpec((tm, D), lambda i, src: (i, 0)),
            pl.BlockSpec((1, 6, D), lambda i, src: (i // tiles_per_batch, 0, 0)),
            pl.BlockSpec((tm, LANES), lambda i, src: (i, 0)),
            pl.BlockSpec(memory_space=pl.ANY),
        ],
        out_specs=pl.BlockSpec((tm, D), lambda i, src: (i, 0)),
        scratch_shapes=[
            pltpu.VMEM((2, 2 * L, D // 2), BF16),
            pltpu.SemaphoreType.DMA((2,)),
        ],
    )
    return pl.pallas_call(
        _combine_kernel,
        grid_spec=grid_spec,
        out_shape=jax.ShapeDtypeStruct((T, D), F32),
        compiler_params=_cparams(("arbitrary",)),
        name="combine",
    )(src_table, x1, mod3, wts, yb)


def _rotate_half_matrix():
    half = MLA_ROPE // 2
    p = np.zeros((MLA_ROPE, MLA_ROPE), np.float32)
    for j in range(half):
        p[j + half, j] = -1.0
        p[j, j + half] = 1.0
    return jnp.asarray(p)


def _slot_gain(g, factor):
    half = MLA_ROPE // 2
    g_n, g_r = g[:MLA_NOPE], g[MLA_NOPE:]
    slot = jnp.concatenate([g_n, g_r, g_r[half:], g_r[:half]]) * factor
    return jnp.tile(slot, MLA_HEADS)[None, :].astype(F32)


def _prepare_weights(w_in, wq_up, wkv_up, qn_g, kn_g):
    rot = _rotate_half_matrix()
    s0 = MLA_Q_LORA
    s1 = s0 + MLA_KV_LORA
    s2 = s1 + MLA_ROPE
    w_kr = w_in[:, s1:s2]
    kr_slot = jnp.concatenate([jnp.zeros((D_MODEL, MLA_NOPE), F32), w_kr, w_kr @ rot], axis=1)
    win_p = jnp.concatenate([w_in[:, :s1], kr_slot, w_in[:, s2:]], axis=1).astype(BF16)

    wq = wq_up.reshape(MLA_Q_LORA, MLA_HEADS, MLA_QK)
    wq_r = wq[:, :, MLA_NOPE:]
    wq_ext = jnp.concatenate([wq, jnp.einsum('lhr,rs->lhs', wq_r, rot)], axis=2)
    wq_ext = wq_ext.reshape(MLA_Q_LORA, SLOT_W).astype(BF16)

    wkv = wkv_up.reshape(MLA_KV_LORA, MLA_HEADS, MLA_NOPE + MLA_V)
    wk_slot = jnp.concatenate([wkv[:, :, :MLA_NOPE], jnp.zeros((MLA_KV_LORA, MLA_HEADS, LANES - MLA_NOPE), F32)],
                              axis=2).reshape(MLA_KV_LORA, SLOT_W)
    wv = wkv[:, :, MLA_NOPE:].reshape(MLA_KV_LORA, V_W)
    wkv_ext = jnp.concatenate([wk_slot, wv], axis=1).astype(BF16)

    gq = _slot_gain(qn_g, MLA_QK ** -0.5 * math.log2(math.e))
    gk = _slot_gain(kn_g, 1.0)
    return win_p, wq_ext, wkv_ext, gq, gk


def _chunk_tables(lch, tm):
    nt = lch.shape[0]
    lchunks = _local_rows(tm) // ROW_CHUNK
    tch = EXPERT_TILE // ROW_CHUNK
    n_pad_steps = -(-(N_EXPERTS * (tch - 1)) // lchunks)
    while ((nt + n_pad_steps) * lchunks) % tch:
        n_pad_steps += 1
    n_steps = nt + n_pad_steps
    total = n_steps * lchunks
    n_tiles = total // tch

    nch = lch[:, 0, N_GROUPS:N_GROUPS + N_EXPERTS]
    lend = jnp.cumsum(nch, axis=1)
    lstart = lend - nch
    ltot = lend[:, -1]
    tot = jnp.sum(nch, axis=0)
    ptot = (tot + tch - 1) // tch * tch
    gend = jnp.cumsum(ptot)
    gstart = gend - ptot
    toff = jnp.cumsum(nch, axis=0) - nch
    c = jnp.arange(lchunks, dtype=I32)

    def lookup(idx, table):
        hit = idx[..., None] == jnp.arange(table.shape[-1], dtype=I32)
        return jnp.sum(jnp.where(hit, table, 0), axis=-1)

    e_of = jnp.minimum(jnp.sum(c[None, :, None] >= lend[:, None, :], axis=2), N_EXPERTS - 1)
    dest = lookup(e_of, (gstart[None, :] + toff - lstart)[:, None, :]) + c[None, :]
    valid = c[None, :] < ltot[:, None]
    src_table = jnp.where(valid, dest, 0).astype(I32)

    gap = jnp.concatenate([ptot - tot, total - gend[-1:]])
    gap_first = jnp.concatenate([gstart + tot, gend[-1:]])
    gap_cum = jnp.cumsum(gap)
    unused = lchunks - ltot
    rank = jnp.concatenate([
        (jnp.cumsum(unused) - unused)[:, None] + c[None, :] - ltot[:, None],
        jnp.sum(unused) + jnp.arange(n_pad_steps, dtype=I32)[:, None] * lchunks + c[None, :]])
    g_of = jnp.minimum(jnp.sum(rank[:, :, None] >= gap_cum[None, None, :], axis=2), N_EXPERTS)
    filler = lookup(g_of, gap_first - (gap_cum - gap)) + rank
    routed = jnp.concatenate([valid, jnp.zeros((n_pad_steps, lchunks), bool)])
    dst_table = jnp.where(routed, jnp.concatenate([dest, jnp.zeros((n_pad_steps, lchunks), I32)]), filler).astype(I32)

    ts = jnp.arange(n_tiles, dtype=I32) * tch
    tile_expert = jnp.minimum(jnp.sum(ts[:, None] >= gend[None, :], axis=1), N_EXPERTS - 1).astype(I32)
    tile_valid = (ts < gend[-1]).astype(I32)
    return dst_table, src_table, tile_expert, tile_valid, total * ROW_CHUNK


def _pick(n, pref):
    return pref if n % pref == 0 else n


def kernel(x, c, positions, ada_w, ada_b, norm1_g, w_in, mla_qa_g, mla_wq_up, mla_kva_g, mla_wkv_up, mla_qn_g, mla_kn_g, hg_lb_logits, hg_norm_g, w_out, norm2_g, router_group_w, router_group_b, router_expert_w, router_expert_b, w_gate, w_up, w_down):
    B, S, D = x.shape
    assert D == D_MODEL and ada_w.shape[0] == 1, "one layer of width 1024"
    T = B * S
    tm = _pick(S, 512)
    tiles_per_batch = S // tm

    mod3 = _mod_call(c, ada_w[0], ada_b).reshape(B, 6, D)

    win_p, wq_ext, wkv_ext, gq, gk = _prepare_weights(w_in[0], mla_wq_up[0], mla_wkv_up[0], mla_qn_g[0], mla_kn_g[0])
    q_hm, k_hm, vt, hg = _proj_call(x, mod3, norm1_g, win_p, mla_qa_g, wq_ext, mla_kva_g, wkv_ext, gq, gk,
                                    positions, tm)
    attn_t = _attn_call(q_hm, k_hm, vt, tm)
    rec = _hgrn_call(hg, hg_lb_logits.reshape(2 * hg_lb_logits.shape[1], HG_WIDTH), hg_norm_g, hps=2)

    wo = w_out[0].astype(BF16)
    wr = jnp.concatenate([router_group_w[0], router_expert_w[0],
                          jnp.zeros((D, ROUTE_ROWS - N_GROUPS - N_EXPERTS), F32)], axis=1).astype(BF16)
    br = jnp.concatenate([router_group_b[0], router_expert_b[0],
                          jnp.zeros((ROUTE_ROWS - N_GROUPS - N_EXPERTS,), F32)])[None, :]
    x1, h2, info, wts, lch = _router_call(attn_t, rec.reshape(T, HG_WIDTH), x.reshape(T, D), mod3,
                                          wo[:V_W], wo[V_W:], norm2_g, wr, br, tm, tiles_per_batch)

    dst_table, src_table, tile_expert, tile_valid, cap_rows = _chunk_tables(lch, tm)
    xb = _dispatch_call(dst_table, info, h2, cap_rows, tm)
    yb = _expert_call(tile_expert, tile_valid, xb, w_gate[0], w_up[0], w_down[0])
    out = _combine_call(src_table, x1, mod3, wts, yb, tm, tiles_per_batch)
    return out.reshape(B, S, D)
```

```python
import functools
import math

import numpy as np
import jax
import jax.numpy as jnp
from jax import lax
from jax.experimental import pallas as pl
from jax.experimental.pallas import tpu as pltpu

F32 = jnp.float32
BF16 = jnp.bfloat16
I32 = jnp.int32

D_MODEL = 1024
MLA_HEADS = 8
MLA_NOPE = 64
MLA_ROPE = 32
MLA_QK = MLA_NOPE + MLA_ROPE
MLA_V = 64
MLA_Q_LORA = 384
MLA_KV_LORA = 256
ROPE_THETA = 10000.0
HG_HEADS = 4
HG_DIM = 128
HG_WIDTH = HG_HEADS * HG_DIM
HG_CHUNK = 64
N_GROUPS = 4
EPG = 8
N_EXPERTS = N_GROUPS * EPG
TOP_K = 2
D_EXPERT = 512
EPS = 1e-6

LANES = 128
VMEM_LIMIT = 56 * 1024 * 1024

SLOT_W = MLA_HEADS * LANES
V_W = MLA_HEADS * MLA_V
HG_SEGS = 5
W_IN_COLS = MLA_Q_LORA + MLA_KV_LORA + LANES + HG_SEGS * HG_WIDTH
HG_GROUP_ROWS = 4 * HG_CHUNK
ONES_ROWS = 16
VT_ROWS = MLA_V + ONES_ROWS
ATTN_KEY_BLOCK = 256
ATTN_Q_TILE = 1024
ROUTE_ROWS = 128
EXPERT_TILE = 512
MXU_DIM = 256
ROW_CHUNK = 8
FOLD_CHUNK = 2 * ROW_CHUNK
COL_REP = LANES // 4


def _cparams(sem, vmem=VMEM_LIMIT):
    return pltpu.CompilerParams(dimension_semantics=sem, vmem_limit_bytes=vmem)


def _rms(x):
    return x * lax.rsqrt(jnp.mean(x * x, axis=-1, keepdims=True) + EPS)


def _silu(x):
    return x * jax.nn.sigmoid(x)


def _dot(a, b):
    return jnp.dot(a, b, preferred_element_type=F32)


def _dot_nt(a, b):
    return lax.dot_general(a, b, (((1,), (1,)), ((), ())), preferred_element_type=F32)


def _dot_tn(a, b):
    return lax.dot_general(a, b, (((0,), (0,)), ((), ())), preferred_element_type=F32)


def _fold_rows(v):
    r, c2 = v.shape
    c = c2 // 2
    left = v[:, :c].reshape(r // ROW_CHUNK, 1, ROW_CHUNK, c)
    right = v[:, c:].reshape(r // ROW_CHUNK, 1, ROW_CHUNK, c)
    return jnp.concatenate([left, right], axis=1).reshape(2 * r, c).astype(BF16)


def _unfold_rows(p):
    r2, c = p.shape
    r = r2 // 2
    q = p.astype(F32).reshape(r // ROW_CHUNK, 2, ROW_CHUNK, c)
    return q[:, 0].reshape(r, c).astype(BF16), q[:, 1].reshape(r, c).astype(BF16)


def _mod_kernel(c_ref, w_ref, b_ref, o_ref):
    a = _silu(c_ref[...]).astype(BF16)
    o_ref[...] = _dot(a, w_ref[...].astype(BF16)) + b_ref[...]


def _mod_call(c, ada_w, ada_b):
    B = c.shape[0]
    n = ada_w.shape[1] // D_MODEL
    return pl.pallas_call(
        _mod_kernel,
        grid=(n,),
        in_specs=[
            pl.BlockSpec((B, D_MODEL), lambda j: (0, 0)),
            pl.BlockSpec((D_MODEL, D_MODEL), lambda j: (0, j)),
            pl.BlockSpec((1, D_MODEL), lambda j: (0, j)),
        ],
        out_specs=pl.BlockSpec((B, D_MODEL), lambda j: (0, j)),
        out_shape=jax.ShapeDtypeStruct((B, n * D_MODEL), F32),
        compiler_params=_cparams(("arbitrary",)),
        name="mod",
    )(c, ada_w, ada_b)


def _rope_tables(pos_row):
    half = MLA_ROPE // 2
    tm = pos_row.shape[1]
    idx = lax.broadcasted_iota(I32, (half, tm), 0).astype(F32)
    inv_freq = ROPE_THETA ** (-idx / half)
    ang = pos_row.astype(F32) * inv_freq
    c = jnp.cos(ang)
    s = jnp.sin(ang)
    ct = jnp.concatenate([jnp.ones((MLA_NOPE, tm), F32), c, c, jnp.zeros((MLA_ROPE, tm), F32)], axis=0)
    st = jnp.concatenate([jnp.zeros((MLA_QK, tm), F32), s, s], axis=0)
    return ct.T, st.T


def _head_finish(t, g_row, ct, st, lane_valid):
    ss = jnp.sum(jnp.where(lane_valid, t * t, 0.0), axis=-1, keepdims=True) * (1.0 / MLA_QK)
    tg = t * lax.rsqrt(ss + EPS) * g_row
    return tg * ct + pltpu.roll(tg * st, LANES - MLA_ROPE, 1)


def _proj_kernel(x_ref, mod_ref, g1_ref, win_ref, qag_ref, wq_ref, kvag_ref, wkv_ref, gq_ref, gk_ref,
                 pos_ref, q_out, k_out, vt_out, hg_out):
    x = x_ref[0]
    mod = mod_ref[0]
    h = _rms(x) * g1_ref[...]
    h = h * (1.0 + mod[1:2]) + mod[0:1]
    hb = h.astype(BF16)

    ct, st = _rope_tables(pos_ref[0])
    tm = x.shape[0]
    lane_valid = lax.broadcasted_iota(I32, (tm, LANES), 1) < MLA_QK

    c0 = MLA_Q_LORA
    c1 = c0 + MLA_KV_LORA
    c2 = c1 + LANES

    q_lat = _dot(hb, win_ref[:, 0:c0])
    qn = (_rms(q_lat) * qag_ref[...]).astype(BF16)
    qe = _dot(qn, wq_ref[...])
    for hd in range(MLA_HEADS):
        sl = slice(hd * LANES, (hd + 1) * LANES)
        q_out[0, hd] = _head_finish(qe[:, sl], gq_ref[:, sl], ct, st, lane_valid).astype(BF16)

    kv_lat = _dot(hb, win_ref[:, c0:c1])
    kvn = (_rms(kv_lat) * kvag_ref[...]).astype(BF16)
    ke = _dot(kvn, wkv_ref[:, 0:SLOT_W])
    kr = _dot(hb, win_ref[:, c1:c2])
    for hd in range(MLA_HEADS):
        sl = slice(hd * LANES, (hd + 1) * LANES)
        k_out[0, hd] = _head_finish(ke[:, sl] + kr, gk_ref[:, sl], ct, st, lane_valid).astype(BF16)

    v_t = _dot(kvn, wkv_ref[:, SLOT_W:SLOT_W + V_W]).T.astype(BF16)
    ones = jnp.ones((ONES_ROWS, tm), BF16)
    for hd in range(MLA_HEADS):
        vt_out[0, hd, 0:MLA_V, :] = v_t[hd * MLA_V:(hd + 1) * MLA_V, :]
        vt_out[0, hd, MLA_V:VT_ROWS, :] = ones

    for sg in range(HG_SEGS):
        lo = c2 + sg * HG_WIDTH
        hg_out[0, :, sg * HG_WIDTH:(sg + 1) * HG_WIDTH] = _dot(hb, win_ref[:, lo:lo + HG_WIDTH]).astype(BF16)


def _proj_call(x, mod3, g1, win_p, qag, wq_ext, kvag, wkv_ext, gq, gk, positions, tm):
    B, S, D = x.shape
    nt = S // tm
    pos3 = positions.reshape(B, 1, S)
    full = lambda shape: pl.BlockSpec(shape, lambda b, i: (0,) * len(shape))
    return pl.pallas_call(
        _proj_kernel,
        grid=(B, nt),
        in_specs=[
            pl.BlockSpec((1, tm, D), lambda b, i: (b, i, 0)),
            pl.BlockSpec((1, 6, D), lambda b, i: (b, 0, 0)),
            full((1, D)),
            full((D, W_IN_COLS)),
            full((1, MLA_Q_LORA)),
            full((MLA_Q_LORA, SLOT_W)),
            full((1, MLA_KV_LORA)),
            full((MLA_KV_LORA, SLOT_W + V_W)),
            full((1, SLOT_W)),
            full((1, SLOT_W)),
            pl.BlockSpec((1, 1, tm), lambda b, i: (b, 0, i)),
        ],
        out_specs=[
            pl.BlockSpec((1, MLA_HEADS, tm, LANES), lambda b, i: (b, 0, i, 0)),
            pl.BlockSpec((1, MLA_HEADS, tm, LANES), lambda b, i: (b, 0, i, 0)),
            pl.BlockSpec((1, MLA_HEADS, VT_ROWS, tm), lambda b, i: (b, 0, 0, i)),
            pl.BlockSpec((1, tm, HG_SEGS * HG_WIDTH), lambda b, i: (b, i, 0)),
        ],
        out_shape=[
            jax.ShapeDtypeStruct((B, MLA_HEADS, S, LANES), BF16),
            jax.ShapeDtypeStruct((B, MLA_HEADS, S, LANES), BF16),
            jax.ShapeDtypeStruct((B, MLA_HEADS, VT_ROWS, S), BF16),
            jax.ShapeDtypeStruct((B, S, HG_SEGS * HG_WIDTH), BF16),
        ],
        compiler_params=_cparams(("arbitrary", "arbitrary")),
        name="proj",
    )(x, mod3, g1, win_p, qag, wq_ext, kvag, wkv_ext, gq, gk, pos3)


def _attn_kernel(q_ref, k_ref, vt_ref, o_ref, s_scr):
    S = k_ref.shape[2]
    tq = o_ref.shape[3]
    n_items = o_ref.shape[1] * MLA_HEADS
    kb_rows = min(ATTN_KEY_BLOCK, S)
    blocks = [slice(b * kb_rows, (b + 1) * kb_rows) for b in range(S // kb_rows)]

    def split(i):
        return i // MLA_HEADS, i % MLA_HEADS

    def scores(i, rows):
        qt, hd = split(i)
        q = q_ref[0, hd, pl.ds(pl.multiple_of(qt * tq, tq), tq), :]
        s_t = _dot_nt(k_ref[0, hd, rows, :], q)
        s_scr[rows, :] = s_t
        return jnp.max(s_t, axis=0, keepdims=True)

    def step(i, m, with_next):
        qt, hd = split(i)
        acc = jnp.zeros((VT_ROWS, tq), F32)
        m_next = None
        for rows in blocks:
            p = jnp.exp2(s_scr[rows, :] - m).astype(BF16)
            acc = acc + _dot(vt_ref[0, hd, :, rows], p)
            if with_next:
                bm = scores(i + 1, rows)
                m_next = bm if m_next is None else jnp.maximum(m_next, bm)
        o = acc[:MLA_V] / acc[MLA_V:MLA_V + 1]
        o_ref[0, qt, pl.ds(pl.multiple_of(hd * MLA_V, MLA_V), MLA_V), :] = o.astype(BF16)
        return m_next

    m0 = None
    for rows in blocks:
        bm = scores(0, rows)
        m0 = bm if m0 is None else jnp.maximum(m0, bm)
    m_last = lax.fori_loop(0, n_items - 1, lambda i, m: step(i, m, True), m0)
    step(n_items - 1, m_last, False)


def _attn_call(q_hm, k_hm, vt, tq):
    B, _, S, _ = q_hm.shape
    nq = S // tq
    return pl.pallas_call(
        _attn_kernel,
        grid=(B,),
        in_specs=[
            pl.BlockSpec((1, MLA_HEADS, S, LANES), lambda b: (b, 0, 0, 0)),
            pl.BlockSpec((1, MLA_HEADS, S, LANES), lambda b: (b, 0, 0, 0)),
            pl.BlockSpec((1, MLA_HEADS, VT_ROWS, S), lambda b: (b, 0, 0, 0)),
        ],
        out_specs=pl.BlockSpec((1, nq, V_W, tq), lambda b: (b, 0, 0, 0)),
        out_shape=jax.ShapeDtypeStruct((B, nq, V_W, tq), BF16),
        scratch_shapes=[pltpu.VMEM((S, tq), F32)],
        compiler_params=_cparams(("arbitrary",)),
        name="attn",
    )(q_hm, k_hm, vt)


def _hgrn_kernel(hq_ref, ff_ref, fb_ref, hi_ref, hgate_ref, lbl_ref, ng_ref, o_ref, of_scr, ob_scr, st_scr,
                 bc_scr, k_scr, qd_scr, att_scr, kv_scr, el_scr, *, hps):
    S = hq_ref.shape[1]
    C = HG_CHUNK
    G = HG_GROUP_ROWS
    cpg = G // C
    n_groups = S // G

    def lower_bound(l0, l1):
        m = jnp.maximum(l0, l1)
        e0 = jnp.exp(l0 - m)
        return e0 / (e0 + jnp.exp(l1 - m))

    lb_f = lower_bound(lbl_ref[0:1, :], lbl_ref[1:2, :])
    lb_b = lower_bound(lbl_ref[2:3, :], lbl_ref[3:4, :])

    rr = lax.broadcasted_iota(I32, (G, G), 0)
    cc = lax.broadcasted_iota(I32, (G, G), 1)
    same_chunk = (rr // C) == (cc // C)
    mask_f = same_chunk & (cc <= rr)
    mask_b = same_chunk & (cc >= rr)
    tri_f = jnp.where(mask_f, 1.0, 0.0).astype(BF16)
    tri_b = jnp.where(mask_b, 1.0, 0.0).astype(BF16)

    st_scr[...] = jnp.zeros_like(st_scr)
    W = hps * HG_DIM
    dirs = ((0, lb_f, ff_ref, of_scr, mask_f, tri_f), (1, lb_b, fb_ref, ob_scr, mask_b, tri_b))

    def rows_of(d, g):
        grp = g if d == 0 else n_groups - 1 - g
        return pl.ds(grp * G if isinstance(grp, int) else pl.multiple_of(grp * G, G), G)

    def stage_gates(g, slot):
        for d, lb, f_ref, _, _, tri in dirs:
            f = lb + (1.0 - lb) * jax.nn.sigmoid(f_ref[0, rows_of(d, g), :].astype(F32))
            lf = jnp.log(f)
            hi = lf.astype(BF16)
            lo = (lf - hi.astype(F32)).astype(BF16)
            both = _dot(tri, jnp.concatenate([hi, lo], axis=1))
            bc_scr[slot, d] = both[:, :W] + both[:, W:]
            k_scr[slot, d] = 1.0 - f

    def stage_decay(g, slot):
        for d, _, _, _, mask, _ in dirs:
            rows = rows_of(d, g)
            bc = bc_scr[slot, d]
            eb = jnp.exp(bc)
            qd = (_silu(hq_ref[0, rows, :].astype(F32)) * eb).astype(BF16)
            kd = k_scr[slot, d] * jnp.exp(-bc)
            kdb = kd.astype(BF16)
            qd_scr[slot, d] = qd
            v = hi_ref[0, rows, :]
            for j in range(hps):
                ls = slice(j * HG_DIM, (j + 1) * HG_DIM)
                ch = d * hps + j
                att_scr[slot, ch] = jnp.where(mask, _dot_nt(qd[:, ls], kdb[:, ls]), 0.0).astype(BF16)
                lasts = []
                for c in range(cpg):
                    cs = slice(c * C, (c + 1) * C)
                    last = (c + 1) * C - 1 if d == 0 else c * C
                    e_last = eb[last:last + 1, ls]
                    lasts.append(e_last)
                    k_rem = (kd[cs, ls] * e_last).astype(BF16)
                    kv_scr[slot, ch * cpg + c] = _dot_tn(v[cs, ls], k_rem)
                el_scr[slot, ch] = jnp.concatenate(lasts + [jnp.zeros((8 - cpg, HG_DIM), F32)], axis=0)

    def stage_state(g, slot):
        for d, _, _, o_scr, _, _ in dirs:
            rows = rows_of(d, g)
            qd = qd_scr[slot, d]
            v = hi_ref[0, rows, :]
            for j in range(hps):
                ls = slice(j * HG_DIM, (j + 1) * HG_DIM)
                ch = d * hps + j
                o_intra = _dot(att_scr[slot, ch], v[:, ls])
                el = el_scr[slot, ch]
                st_t = st_scr[ch]
                o_inter = [None] * cpg
                for c in (range(cpg) if d == 0 else reversed(range(cpg))):
                    o_inter[c] = _dot_nt(qd[c * C:(c + 1) * C, ls], st_t.astype(BF16))
                    st_t = st_t * el[c:c + 1, :] + kv_scr[slot, ch * cpg + c]
                st_scr[ch] = st_t
                o_scr[rows, ls] = o_intra + jnp.concatenate(o_inter, axis=0)

    def step(t, parity):
        if not isinstance(t, int) or t < n_groups:
            stage_gates(t, parity)
        if not isinstance(t, int) or 1 <= t <= n_groups:
            stage_decay(t - 1, 1 - parity)
        if not isinstance(t, int) or t >= 2:
            stage_state(t - 2, parity)

    n_steps = n_groups + 2
    head = min(2, n_steps)
    tail_start = max(head, n_groups)
    for t in range(head):
        step(t, t % 2)
    n_mid = tail_start - head
    if n_mid % 2 == 0 and n_mid > 4:
        def body(i, carry):
            t = 2 + 2 * i
            step(t, 0)
            step(t + 1, 1)
            return carry
        lax.fori_loop(0, n_mid // 2, body, 0)
    else:
        for t in range(head, tail_start):
            step(t, t % 2)
    for t in range(tail_start, n_steps):
        step(t, t % 2)

    for j in range(hps):
        ls = slice(j * HG_DIM, (j + 1) * HG_DIM)
        o = of_scr[:, ls] + ob_scr[:, ls]
        gate = _silu(hgate_ref[0, :, ls].astype(F32))
        o_ref[0, :, ls] = (_rms(o) * ng_ref[...] * gate).astype(BF16)


def _hgrn_call(hg, lb_logits, norm_g, hps):
    B, S, _ = hg.shape
    W = hps * HG_DIM
    nh = HG_HEADS // hps
    G = HG_GROUP_ROWS
    cpg = G // HG_CHUNK
    seg = lambda sg: pl.BlockSpec((1, S, W), lambda b, j, sg=sg: (b, 0, sg * nh + j))
    return pl.pallas_call(
        functools.partial(_hgrn_kernel, hps=hps),
        grid=(B, nh),
        in_specs=[
            seg(0), seg(1), seg(2), seg(3), seg(4),
            pl.BlockSpec((4, W), lambda b, j: (0, j)),
            pl.BlockSpec((1, HG_DIM), lambda b, j: (0, 0)),
        ],
        out_specs=pl.BlockSpec((1, S, W), lambda b, j: (b, 0, j)),
        out_shape=jax.ShapeDtypeStruct((B, S, HG_WIDTH), BF16),
        scratch_shapes=[
            pltpu.VMEM((S, W), F32),
            pltpu.VMEM((S, W), F32),
            pltpu.VMEM((2 * hps, HG_DIM, HG_DIM), F32),
            pltpu.VMEM((2, 2, G, W), F32),
            pltpu.VMEM((2, 2, G, W), F32),
            pltpu.VMEM((2, 2, G, W), BF16),
            pltpu.VMEM((2, 2 * hps, G, G), BF16),
            pltpu.VMEM((2, 2 * hps * cpg, HG_DIM, HG_DIM), F32),
            pltpu.VMEM((2, 2 * hps, 8, HG_DIM), F32),
        ],
        compiler_params=_cparams(("arbitrary", "arbitrary")),
        name="hgrn",
    )(hg, hg, hg, hg, hg, lb_logits, norm_g)


def _router_kernel(attn_ref, rec_ref, x_ref, mod_ref, woa_ref, wor_ref, g2_ref, wr_ref, br_ref,
                   x1_ref, h2_ref, info_ref, wts_ref, lch_ref):
    mod = mod_ref[0]
    mixed = _dot_tn(attn_ref[0, 0], woa_ref[...]) + _dot(rec_ref[...], wor_ref[...])
    x1 = x_ref[...] + mod[2:3] * mixed
    x1_ref[...] = x1
    h2 = _rms(x1) * g2_ref[...]
    h2 = (h2 * (1.0 + mod[4:5]) + mod[3:4]).astype(BF16)
    h2_ref[...] = h2

    logits = _dot(h2, wr_ref[...]) + br_ref[...]
    lt = logits.T
    tm = lt.shape[1]
    r = lax.broadcasted_iota(I32, (ROUTE_ROWS, tm), 0)
    neg = -jnp.inf
    big = ROUTE_ROWS

    isg = r < N_GROUPS
    gmax = jnp.max(jnp.where(isg, lt, neg), axis=0, keepdims=True)
    gidx = jnp.min(jnp.where(isg & (lt == gmax), r, big), axis=0, keepdims=True)
    gp = 1.0 / jnp.sum(jnp.where(isg, jnp.exp(lt - gmax), 0.0), axis=0, keepdims=True)

    ing = (r >= N_GROUPS) & (r < N_GROUPS + N_EXPERTS) & (((r - N_GROUPS) >> 3) == gidx)
    v1 = jnp.max(jnp.where(ing, lt, neg), axis=0, keepdims=True)
    i1 = jnp.min(jnp.where(ing & (lt == v1), r, big), axis=0, keepdims=True)
    ing2 = ing & (r != i1)
    v2 = jnp.max(jnp.where(ing2, lt, neg), axis=0, keepdims=True)
    i2 = jnp.min(jnp.where(ing2 & (lt == v2), r, big), axis=0, keepdims=True)
    e21 = jnp.exp(v2 - v1)
    w1 = gp / (1.0 + e21)
    w2 = gp * e21 / (1.0 + e21)

    oh1 = r == i1
    oh2 = r == i2
    oh = jnp.where(oh1 | oh2, 1.0, 0.0)
    ohb = oh.astype(BF16)
    tt = lax.broadcasted_iota(I32, (tm, tm), 0)
    tc = lax.broadcasted_iota(I32, (tm, tm), 1)
    upper = jnp.where(tt < tc, 1.0, 0.0).astype(BF16)
    before = _dot(ohb, upper)
    cnt_col = jnp.sum(oh, axis=1, keepdims=True)
    pad_col = jnp.ceil(cnt_col * (1.0 / ROW_CHUNK)) * ROW_CHUNK
    er = lax.broadcasted_iota(I32, (ROUTE_ROWS, ROUTE_ROWS), 0)
    ec = lax.broadcasted_iota(I32, (ROUTE_ROWS, ROUTE_ROWS), 1)
    lower = jnp.where(ec < er, 1.0, 0.0).astype(BF16)
    start_col = _dot(lower, jnp.broadcast_to(pad_col, (ROUTE_ROWS, ROUTE_ROWS)).astype(BF16))[:, 0:1]
    pos = before + start_col
    slot1 = jnp.sum(jnp.where(oh1, pos, 0.0), axis=0, keepdims=True)
    slot2 = jnp.sum(jnp.where(oh2, pos, 0.0), axis=0, keepdims=True)

    zero = jnp.zeros((1, tm), I32)
    info_ref[0] = jnp.concatenate(
        [slot1.astype(I32), slot2.astype(I32), i1 - N_GROUPS, i2 - N_GROUPS, zero, zero, zero, zero], axis=0)
    rows = [jnp.broadcast_to(v, (COL_REP, tm)) for v in (w1, w2, slot1, slot2)]
    wts_ref[...] = jnp.concatenate(rows, axis=0).T
    cnt_row = _dot_nt(jnp.ones((8, tm), BF16), ohb)
    lch_ref[0] = jnp.ceil(cnt_row * (1.0 / ROW_CHUNK)).astype(I32)


def _router_call(attn_t, rec, x2, mod3, wo_a, wo_r, g2, wr, br, tm, tiles_per_batch):
    T, D = x2.shape
    nt = T // tm
    tiles_per_q = attn_t.shape[3] // tm
    full = lambda shape: pl.BlockSpec(shape, lambda i: (0,) * len(shape))
    return pl.pallas_call(
        _router_kernel,
        grid=(nt,),
        in_specs=[
            pl.BlockSpec((1, 1, V_W, tm), lambda i: (i // tiles_per_batch, (i % tiles_per_batch) // tiles_per_q,
                                                     0, (i % tiles_per_batch) % tiles_per_q)),
            pl.BlockSpec((tm, HG_WIDTH), lambda i: (i, 0)),
            pl.BlockSpec((tm, D), lambda i: (i, 0)),
            pl.BlockSpec((1, 6, D), lambda i: (i // tiles_per_batch, 0, 0)),
            full((V_W, D)),
            full((HG_WIDTH, D)),
            full((1, D)),
            full((D, ROUTE_ROWS)),
            full((1, ROUTE_ROWS)),
        ],
        out_specs=[
            pl.BlockSpec((tm, D), lambda i: (i, 0)),
            pl.BlockSpec((tm, D), lambda i: (i, 0)),
            pl.BlockSpec((1, 8, tm), lambda i: (i, 0, 0)),
            pl.BlockSpec((tm, LANES), lambda i: (i, 0)),
            pl.BlockSpec((1, 8, ROUTE_ROWS), lambda i: (i, 0, 0)),
        ],
        out_shape=[
            jax.ShapeDtypeStruct((T, D), F32),
            jax.ShapeDtypeStruct((T, D), BF16),
            jax.ShapeDtypeStruct((nt, 8, tm), I32),
            jax.ShapeDtypeStruct((T, LANES), F32),
            jax.ShapeDtypeStruct((nt, 8, ROUTE_ROWS), I32),
        ],
        compiler_params=_cparams(("arbitrary",)),
        name="router",
    )(attn_t, rec, x2, mod3, wo_a, wo_r, g2, wr, br)


def _local_rows(tm):
    need = TOP_K * tm + N_EXPERTS * (ROW_CHUNK - 1)
    return -(-need // MXU_DIM) * MXU_DIM


def _dispatch_kernel(dst_ref, info_ref, h2_ref, xb_ref, xs_scr, sem, *, n_real):
    i = pl.program_id(0)
    n = pl.num_programs(0)
    slot = i % 2
    LF = xs_scr.shape[1]
    L = LF // 2
    tm = h2_ref.shape[0]

    def whole(sl):
        return pltpu.make_async_copy(xs_scr.at[sl], xb_ref.at[pl.ds(0, LF)], sem.at[sl])

    @pl.when(i >= 2)
    def _():
        whole(slot).wait()

    @pl.when(i < n_real)
    def _():
        s = lax.broadcasted_iota(I32, (L, tm), 0)
        hit = (s == info_ref[0, 0:1, :]) | (s == info_ref[0, 1:2, :])
        xs_scr[slot] = _fold_rows(_dot(jnp.where(hit, 1.0, 0.0).astype(BF16), h2_ref[...]))

    @pl.when(i >= n_real)
    def _():
        xs_scr[slot] = jnp.zeros((LF, xs_scr.shape[2]), BF16)

    for c in range(LF // FOLD_CHUNK):
        d = pl.multiple_of(dst_ref[i, c] * FOLD_CHUNK, FOLD_CHUNK)
        pltpu.make_async_copy(xs_scr.at[slot, pl.ds(c * FOLD_CHUNK, FOLD_CHUNK)],
                              xb_ref.at[pl.ds(d, FOLD_CHUNK)], sem.at[slot]).start()

    @pl.when(i == n - 1)
    def _():
        whole(slot).wait()
        whole(1 - slot).wait()


def _dispatch_call(dst_table, info, h2, cap_rows, tm):
    T, D = h2.shape
    nt = T // tm
    n_steps = dst_table.shape[0]
    L = _local_rows(tm)
    last = nt - 1
    grid_spec = pltpu.PrefetchScalarGridSpec(
        num_scalar_prefetch=1,
        grid=(n_steps,),
        in_specs=[
            pl.BlockSpec((1, 8, tm), lambda i, dst: (jnp.minimum(i, last), 0, 0)),
            pl.BlockSpec((tm, D), lambda i, dst: (jnp.minimum(i, last), 0)),
        ],
        out_specs=pl.BlockSpec(memory_space=pl.ANY),
        scratch_shapes=[
            pltpu.VMEM((2, 2 * L, D // 2), BF16),
            pltpu.SemaphoreType.DMA((2,)),
        ],
    )
    return pl.pallas_call(
        functools.partial(_dispatch_kernel, n_real=nt),
        grid_spec=grid_spec,
        out_shape=jax.ShapeDtypeStruct((2 * cap_rows, D // 2), BF16),
        compiler_params=_cparams(("arbitrary",)),
        name="dispatch",
    )(dst_table, info, h2)


def _expert_kernel(te_ref, tv_ref, xb_ref, wg_ref, wu_ref, wd_ref, y_ref, wg_b, wu_b, wd_b):
    j = pl.program_id(0)
    prev = te_ref[jnp.maximum(j - 1, 0)]

    @pl.when((j == 0) | (te_ref[j] != prev))
    def _():
        wg_b[...] = wg_ref[0].astype(BF16)
        wu_b[...] = wu_ref[0].astype(BF16)
        wd_b[...] = wd_ref[0].astype(BF16)

    @pl.when(tv_ref[j] > 0)
    def _():
        x = jnp.concatenate(_unfold_rows(xb_ref[...]), axis=1)
        hid = (_silu(_dot(x, wg_b[...])) * _dot(x, wu_b[...])).astype(BF16)
        y_ref[...] = _fold_rows(_dot(hid, wd_b[...]))

    @pl.when(tv_ref[j] == 0)
    def _():
        y_ref[...] = jnp.zeros_like(y_ref)


def _expert_call(tile_expert, tile_valid, xb, w_gate, w_up, w_down):
    cap, DP = xb.shape
    D = 2 * DP
    nt = tile_expert.shape[0]
    grid_spec = pltpu.PrefetchScalarGridSpec(
        num_scalar_prefetch=2,
        grid=(nt,),
        in_specs=[
            pl.BlockSpec((2 * EXPERT_TILE, DP), lambda j, te, tv: (j * tv[j], 0)),
            pl.BlockSpec((1, D, D_EXPERT), lambda j, te, tv: (te[j], 0, 0)),
            pl.BlockSpec((1, D, D_EXPERT), lambda j, te, tv: (te[j], 0, 0)),
            pl.BlockSpec((1, D_EXPERT, D), lambda j, te, tv: (te[j], 0, 0)),
        ],
        out_specs=pl.BlockSpec((2 * EXPERT_TILE, DP), lambda j, te, tv: (j, 0)),
        scratch_shapes=[
            pltpu.VMEM((D, D_EXPERT), BF16),
            pltpu.VMEM((D, D_EXPERT), BF16),
            pltpu.VMEM((D_EXPERT, D), BF16),
        ],
    )
    return pl.pallas_call(
        _expert_kernel,
        grid_spec=grid_spec,
        out_shape=jax.ShapeDtypeStruct((cap, DP), BF16),
        compiler_params=_cparams(("arbitrary",)),
        name="expert",
    )(tile_expert, tile_valid, xb, w_gate, w_up, w_down)


def _combine_kernel(src_ref, x1_ref, mod_ref, wts_ref, yb_ref, o_ref, ys_scr, sem):
    i = pl.program_id(0)
    n = pl.num_programs(0)
    slot = i % 2
    LF = ys_scr.shape[1]
    L = LF // 2
    tm = x1_ref.shape[0]

    def fetch(step, sl):
        for c in range(LF // FOLD_CHUNK):
            d = pl.multiple_of(src_ref[step, c] * FOLD_CHUNK, FOLD_CHUNK)
            pltpu.make_async_copy(yb_ref.at[pl.ds(d, FOLD_CHUNK)],
                                  ys_scr.at[sl, pl.ds(c * FOLD_CHUNK, FOLD_CHUNK)], sem.at[sl]).start()

    @pl.when(i == 0)
    def _():
        fetch(0, 0)

    @pl.when(i + 1 < n)
    def _():
        fetch(i + 1, 1 - slot)

    pltpu.make_async_copy(yb_ref.at[pl.ds(0, LF)], ys_scr.at[slot], sem.at[slot]).wait()

    s = lax.broadcasted_iota(I32, (tm, L), 1).astype(F32)
    w1, w2, slot1, slot2 = (wts_ref[:, j * COL_REP:j * COL_REP + 1] for j in range(4))
    pick = (jnp.where(s == slot1, w1, 0.0) + jnp.where(s == slot2, w2, 0.0)).astype(BF16)
    y_left, y_right = _unfold_rows(ys_scr[slot])
    moe = jnp.concatenate([_dot(pick, y_left), _dot(pick, y_right)], axis=1)
    o_ref[...] = x1_ref[...] + mod_ref[0][5:6] * moe


def _combine_call(src_table, x1, mod3, wts, yb, tm, tiles_per_batch):
    T, D = x1.shape
    nt = T // tm
    L = _local_rows(tm)
    grid_spec = pltpu.PrefetchScalarGridSpec(
        num_scalar_prefetch=1,
        grid=(nt,),
        in_specs=[
            pl.BlockSpec((tm, D), lambda i, src: (i, 0)),
            pl.BlockSpec((1, 6, D), lambda i, src: (i // tiles_per_batch, 0, 0)),
            pl.BlockSpec((tm, LANES), lambda i, src: (i, 0)),
            pl.BlockSpec(memory_space=pl.ANY),
        ],
        out_specs=pl.BlockSpec((tm, D), lambda i, src: (i, 0)),
        scratch_shapes=[
            pltpu.VMEM((2, 2 * L, D // 2), BF16),
            pltpu.SemaphoreType.DMA((2,)),
        ],
    )
    return pl.pallas_call(
        _combine_kernel,
        grid_spec=grid_spec,
        out_shape=jax.ShapeDtypeStruct((T, D), F32),
        compiler_params=_cparams(("arbitrary",)),
        name="combine",
    )(src_table, x1, mod3, wts, yb)


def _rotate_half_matrix():
    half = MLA_ROPE // 2
    p = np.zeros((MLA_ROPE, MLA_ROPE), np.float32)
    for j in range(half):
        p[j + half, j] = -1.0
        p[j, j + half] = 1.0
    return jnp.asarray(p)


def _slot_gain(g, factor):
    half = MLA_ROPE // 2
    g_n, g_r = g[:MLA_NOPE], g[MLA_NOPE:]
    slot = jnp.concatenate([g_n, g_r, g_r[half:], g_r[:half]]) * factor
    return jnp.tile(slot, MLA_HEADS)[None, :].astype(F32)


def _prepare_weights(w_in, wq_up, wkv_up, qn_g, kn_g):
    rot = _rotate_half_matrix()
    s0 = MLA_Q_LORA
    s1 = s0 + MLA_KV_LORA
    s2 = s1 + MLA_ROPE
    w_kr = w_in[:, s1:s2]
    kr_slot = jnp.concatenate([jnp.zeros((D_MODEL, MLA_NOPE), F32), w_kr, w_kr @ rot], axis=1)
    win_p = jnp.concatenate([w_in[:, :s1], kr_slot, w_in[:, s2:]], axis=1).astype(BF16)

    wq = wq_up.reshape(MLA_Q_LORA, MLA_HEADS, MLA_QK)
    wq_r = wq[:, :, MLA_NOPE:]
    wq_ext = jnp.concatenate([wq, jnp.einsum('lhr,rs->lhs', wq_r, rot)], axis=2)
    wq_ext = wq_ext.reshape(MLA_Q_LORA, SLOT_W).astype(BF16)

    wkv = wkv_up.reshape(MLA_KV_LORA, MLA_HEADS, MLA_NOPE + MLA_V)
    wk_slot = jnp.concatenate([wkv[:, :, :MLA_NOPE], jnp.zeros((MLA_KV_LORA, MLA_HEADS, LANES - MLA_NOPE), F32)],
                              axis=2).reshape(MLA_KV_LORA, SLOT_W)
    wv = wkv[:, :, MLA_NOPE:].reshape(MLA_KV_LORA, V_W)
    wkv_ext = jnp.concatenate([wk_slot, wv], axis=1).astype(BF16)

    gq = _slot_gain(qn_g, MLA_QK ** -0.5 * math.log2(math.e))
    gk = _slot_gain(kn_g, 1.0)
    return win_p, wq_ext, wkv_ext, gq, gk


def _chunk_tables(lch, tm):
    nt = lch.shape[0]
    lchunks = _local_rows(tm) // ROW_CHUNK
    tch = EXPERT_TILE // ROW_CHUNK
    n_pad_steps = -(-(N_EXPERTS * (tch - 1)) // lchunks)
    while ((nt + n_pad_steps) * lchunks) % tch:
        n_pad_steps += 1
    n_steps = nt + n_pad_steps
    total = n_steps * lchunks
    n_tiles = total // tch

    nch = lch[:, 0, N_GROUPS:N_GROUPS + N_EXPERTS]
    lend = jnp.cumsum(nch, axis=1)
    lstart = lend - nch
    ltot = lend[:, -1]
    tot = jnp.sum(nch, axis=0)
    ptot = (tot + tch - 1) // tch * tch
    gend = jnp.cumsum(ptot)
    gstart = gend - ptot
    toff = jnp.cumsum(nch, axis=0) - nch
    c = jnp.arange(lchunks, dtype=I32)

    def lookup(idx, table):
        hit = idx[..., None] == jnp.arange(table.shape[-1], dtype=I32)
        return jnp.sum(jnp.where(hit, table, 0), axis=-1)

    e_of = jnp.minimum(jnp.sum(c[None, :, None] >= lend[:, None, :], axis=2), N_EXPERTS - 1)
    dest = lookup(e_of, (gstart[None, :] + toff - lstart)[:, None, :]) + c[None, :]
    valid = c[None, :] < ltot[:, None]
    src_table = jnp.where(valid, dest, 0).astype(I32)

    gap = jnp.concatenate([ptot - tot, total - gend[-1:]])
    gap_first = jnp.concatenate([gstart + tot, gend[-1:]])
    gap_cum = jnp.cumsum(gap)
    unused = lchunks - ltot
    rank = jnp.concatenate([
        (jnp.cumsum(unused) - unused)[:, None] + c[None, :] - ltot[:, None],
        jnp.sum(unused) + jnp.arange(n_pad_steps, dtype=I32)[:, None] * lchunks + c[None, :]])
    g_of = jnp.minimum(jnp.sum(rank[:, :, None] >= gap_cum[None, None, :], axis=2), N_EXPERTS)
    filler = lookup(g_of, gap_first - (gap_cum - gap)) + rank
    routed = jnp.concatenate([valid, jnp.zeros((n_pad_steps, lchunks), bool)])
    dst_table = jnp.where(routed, jnp.concatenate([dest, jnp.zeros((n_pad_steps, lchunks), I32)]), filler).astype(I32)

    ts = jnp.arange(n_tiles, dtype=I32) * tch
    tile_expert = jnp.minimum(jnp.sum(ts[:, None] >= gend[None, :], axis=1), N_EXPERTS - 1).astype(I32)
    tile_valid = (ts < gend[-1]).astype(I32)
    return dst_table, src_table, tile_expert, tile_valid, total * ROW_CHUNK


def _pick(n, pref):
    return pref if n % pref == 0 else n


def kernel(x, c, positions, ada_w, ada_b, norm1_g, w_in, mla_qa_g, mla_wq_up, mla_kva_g, mla_wkv_up, mla_qn_g, mla_kn_g, hg_lb_logits, hg_norm_g, w_out, norm2_g, router_group_w, router_group_b, router_expert_w, router_expert_b, w_gate, w_up, w_down):
    B, S, D = x.shape
    assert D == D_MODEL and ada_w.shape[0] == 1, "one layer of width 1024"
    T = B * S
    tm = _pick(S, 512)
    tiles_per_batch = S // tm

    mod3 = _mod_call(c, ada_w[0], ada_b).reshape(B, 6, D)

    win_p, wq_ext, wkv_ext, gq, gk = _prepare_weights(w_in[0], mla_wq_up[0], mla_wkv_up[0], mla_qn_g[0], mla_kn_g[0])
    q_hm, k_hm, vt, hg = _proj_call(x, mod3, norm1_g, win_p, mla_qa_g, wq_ext, mla_kva_g, wkv_ext, gq, gk,
                                    positions, tm)
    tq = _pick(S, ATTN_Q_TILE)
    attn_t = _attn_call(q_hm, k_hm, vt, tq)
    rec = _hgrn_call(hg, hg_lb_logits.reshape(2 * hg_lb_logits.shape[1], HG_WIDTH), hg_norm_g, hps=2)

    wo = w_out[0].astype(BF16)
    wr = jnp.concatenate([router_group_w[0], router_expert_w[0],
                          jnp.zeros((D, ROUTE_ROWS - N_GROUPS - N_EXPERTS), F32)], axis=1).astype(BF16)
    br = jnp.concatenate([router_group_b[0], router_expert_b[0],
                          jnp.zeros((ROUTE_ROWS - N_GROUPS - N_EXPERTS,), F32)])[None, :]
    x1, h2, info, wts, lch = _router_call(attn_t, rec.reshape(T, HG_WIDTH), x.reshape(T, D), mod3,
                                          wo[:V_W], wo[V_W:], norm2_g, wr, br, tm, tiles_per_batch)

    dst_table, src_table, tile_expert, tile_valid, cap_rows = _chunk_tables(lch, tm)
    xb = _dispatch_call(dst_table, info, h2, cap_rows, tm)
    yb = _expert_call(tile_expert, tile_valid, xb, w_gate[0], w_up[0], w_down[0])
    out = _combine_call(src_table, x1, mod3, wts, yb, tm, tiles_per_batch)
    return out.reshape(B, S, D)
```

```python
import functools
import math

import numpy as np
import jax
import jax.numpy as jnp
from jax import lax
from jax.experimental import pallas as pl
from jax.experimental.pallas import tpu as pltpu

F32 = jnp.float32
BF16 = jnp.bfloat16
I32 = jnp.int32

D_MODEL = 1024
MLA_HEADS = 8
MLA_NOPE = 64
MLA_ROPE = 32
MLA_QK = MLA_NOPE + MLA_ROPE
MLA_V = 64
MLA_Q_LORA = 384
MLA_KV_LORA = 256
ROPE_THETA = 10000.0
HG_HEADS = 4
HG_DIM = 128
HG_WIDTH = HG_HEADS * HG_DIM
HG_CHUNK = 64
N_GROUPS = 4
EPG = 8
N_EXPERTS = N_GROUPS * EPG
TOP_K = 2
D_EXPERT = 512
EPS = 1e-6

LANES = 128
VMEM_LIMIT = 56 * 1024 * 1024

SLOT_W = MLA_HEADS * LANES
V_W = MLA_HEADS * MLA_V
HG_SEGS = 5
W_IN_COLS = MLA_Q_LORA + MLA_KV_LORA + LANES + HG_SEGS * HG_WIDTH
HG_GROUP_ROWS = 4 * HG_CHUNK
ONES_ROWS = 16
VT_ROWS = MLA_V + ONES_ROWS
ATTN_KEY_BLOCK = 256
ATTN_Q_TILE = 1024
ROUTE_ROWS = 128
ROUTER_SUB_ROWS = 256
ROUTE_USED = 48
EXPERT_TILE = 512
MXU_DIM = 256
ROW_CHUNK = 8
FOLD_CHUNK = 2 * ROW_CHUNK
COL_REP = LANES // 4


def _cparams(sem, vmem=VMEM_LIMIT):
    return pltpu.CompilerParams(dimension_semantics=sem, vmem_limit_bytes=vmem)


def _rms(x):
    return x * lax.rsqrt(jnp.mean(x * x, axis=-1, keepdims=True) + EPS)


def _silu(x):
    return x * jax.nn.sigmoid(x)


def _dot(a, b):
    return jnp.dot(a, b, preferred_element_type=F32)


def _dot_nt(a, b):
    return lax.dot_general(a, b, (((1,), (1,)), ((), ())), preferred_element_type=F32)


def _dot_tn(a, b):
    return lax.dot_general(a, b, (((0,), (0,)), ((), ())), preferred_element_type=F32)


def _fold_rows(v):
    r, c2 = v.shape
    c = c2 // 2
    left = v[:, :c].reshape(r // ROW_CHUNK, 1, ROW_CHUNK, c)
    right = v[:, c:].reshape(r // ROW_CHUNK, 1, ROW_CHUNK, c)
    return jnp.concatenate([left, right], axis=1).reshape(2 * r, c).astype(BF16)


def _unfold_rows(p):
    r2, c = p.shape
    r = r2 // 2
    q = p.astype(F32).reshape(r // ROW_CHUNK, 2, ROW_CHUNK, c)
    return q[:, 0].reshape(r, c).astype(BF16), q[:, 1].reshape(r, c).astype(BF16)


def _mod_kernel(c_ref, w_ref, b_ref, o_ref):
    a = _silu(c_ref[...]).astype(BF16)
    o_ref[...] = _dot(a, w_ref[...].astype(BF16)) + b_ref[...]


def _mod_call(c, ada_w, ada_b):
    B = c.shape[0]
    n = ada_w.shape[1] // D_MODEL
    return pl.pallas_call(
        _mod_kernel,
        grid=(n,),
        in_specs=[
            pl.BlockSpec((B, D_MODEL), lambda j: (0, 0)),
            pl.BlockSpec((D_MODEL, D_MODEL), lambda j: (0, j)),
            pl.BlockSpec((1, D_MODEL), lambda j: (0, j)),
        ],
        out_specs=pl.BlockSpec((B, D_MODEL), lambda j: (0, j)),
        out_shape=jax.ShapeDtypeStruct((B, n * D_MODEL), F32),
        compiler_params=_cparams(("arbitrary",)),
        name="mod",
    )(c, ada_w, ada_b)


def _rope_tables(pos_row):
    half = MLA_ROPE // 2
    tm = pos_row.shape[1]
    idx = lax.broadcasted_iota(I32, (half, tm), 0).astype(F32)
    inv_freq = ROPE_THETA ** (-idx / half)
    ang = pos_row.astype(F32) * inv_freq
    c = jnp.cos(ang)
    s = jnp.sin(ang)
    ct = jnp.concatenate([jnp.ones((MLA_NOPE, tm), F32), c, c, jnp.zeros((MLA_ROPE, tm), F32)], axis=0)
    st = jnp.concatenate([jnp.zeros((MLA_QK, tm), F32), s, s], axis=0)
    return ct.T, st.T


def _head_finish(t, g_row, ct, st, lane_valid):
    ss = jnp.sum(jnp.where(lane_valid, t * t, 0.0), axis=-1, keepdims=True) * (1.0 / MLA_QK)
    tg = t * lax.rsqrt(ss + EPS) * g_row
    return tg * ct + pltpu.roll(tg * st, LANES - MLA_ROPE, 1)


def _proj_kernel(x_ref, mod_ref, g1_ref, win_ref, qag_ref, wq_ref, kvag_ref, wkv_ref, gq_ref, gk_ref,
                 pos_ref, q_out, k_out, vt_out, hg_out):
    x = x_ref[0]
    mod = mod_ref[0]
    h = _rms(x) * g1_ref[...]
    h = h * (1.0 + mod[1:2]) + mod[0:1]
    hb = h.astype(BF16)

    ct, st = _rope_tables(pos_ref[0])
    tm = x.shape[0]
    lane_valid = lax.broadcasted_iota(I32, (tm, LANES), 1) < MLA_QK

    c0 = MLA_Q_LORA
    c1 = c0 + MLA_KV_LORA
    c2 = c1 + LANES

    q_lat = _dot(hb, win_ref[:, 0:c0])
    qn = (_rms(q_lat) * qag_ref[...]).astype(BF16)
    qe = _dot(qn, wq_ref[...])
    for hd in range(MLA_HEADS):
        sl = slice(hd * LANES, (hd + 1) * LANES)
        q_out[0, hd] = _head_finish(qe[:, sl], gq_ref[:, sl], ct, st, lane_valid).astype(BF16)

    kv_lat = _dot(hb, win_ref[:, c0:c1])
    kvn = (_rms(kv_lat) * kvag_ref[...]).astype(BF16)
    ke = _dot(kvn, wkv_ref[:, 0:SLOT_W])
    kr = _dot(hb, win_ref[:, c1:c2])
    for hd in range(MLA_HEADS):
        sl = slice(hd * LANES, (hd + 1) * LANES)
        k_out[0, hd] = _head_finish(ke[:, sl] + kr, gk_ref[:, sl], ct, st, lane_valid).astype(BF16)

    v_t = _dot(kvn, wkv_ref[:, SLOT_W:SLOT_W + V_W]).T.astype(BF16)
    ones = jnp.ones((ONES_ROWS, tm), BF16)
    for hd in range(MLA_HEADS):
        vt_out[0, hd, 0:MLA_V, :] = v_t[hd * MLA_V:(hd + 1) * MLA_V, :]
        vt_out[0, hd, MLA_V:VT_ROWS, :] = ones

    for sg in range(HG_SEGS):
        lo = c2 + sg * HG_WIDTH
        hg_out[0, :, sg * HG_WIDTH:(sg + 1) * HG_WIDTH] = _dot(hb, win_ref[:, lo:lo + HG_WIDTH]).astype(BF16)


def _proj_call(x, mod3, g1, win_p, qag, wq_ext, kvag, wkv_ext, gq, gk, positions, tm):
    B, S, D = x.shape
    nt = S // tm
    pos3 = positions.reshape(B, 1, S)
    full = lambda shape: pl.BlockSpec(shape, lambda b, i: (0,) * len(shape))
    return pl.pallas_call(
        _proj_kernel,
        grid=(B, nt),
        in_specs=[
            pl.BlockSpec((1, tm, D), lambda b, i: (b, i, 0)),
            pl.BlockSpec((1, 6, D), lambda b, i: (b, 0, 0)),
            full((1, D)),
            full((D, W_IN_COLS)),
            full((1, MLA_Q_LORA)),
            full((MLA_Q_LORA, SLOT_W)),
            full((1, MLA_KV_LORA)),
            full((MLA_KV_LORA, SLOT_W + V_W)),
            full((1, SLOT_W)),
            full((1, SLOT_W)),
            pl.BlockSpec((1, 1, tm), lambda b, i: (b, 0, i)),
        ],
        out_specs=[
            pl.BlockSpec((1, MLA_HEADS, tm, LANES), lambda b, i: (b, 0, i, 0)),
            pl.BlockSpec((1, MLA_HEADS, tm, LANES), lambda b, i: (b, 0, i, 0)),
            pl.BlockSpec((1, MLA_HEADS, VT_ROWS, tm), lambda b, i: (b, 0, 0, i)),
            pl.BlockSpec((1, tm, HG_SEGS * HG_WIDTH), lambda b, i: (b, i, 0)),
        ],
        out_shape=[
            jax.ShapeDtypeStruct((B, MLA_HEADS, S, LANES), BF16),
            jax.ShapeDtypeStruct((B, MLA_HEADS, S, LANES), BF16),
            jax.ShapeDtypeStruct((B, MLA_HEADS, VT_ROWS, S), BF16),
            jax.ShapeDtypeStruct((B, S, HG_SEGS * HG_WIDTH), BF16),
        ],
        compiler_params=_cparams(("arbitrary", "arbitrary")),
        name="proj",
    )(x, mod3, g1, win_p, qag, wq_ext, kvag, wkv_ext, gq, gk, pos3)


def _attn_kernel(q_ref, k_ref, vt_ref, o_ref, s_scr):
    S = k_ref.shape[2]
    tq = o_ref.shape[3]
    n_items = o_ref.shape[1] * MLA_HEADS
    kb_rows = min(ATTN_KEY_BLOCK, S)
    blocks = [slice(b * kb_rows, (b + 1) * kb_rows) for b in range(S // kb_rows)]

    def split(i):
        return i // MLA_HEADS, i % MLA_HEADS

    def scores(i, rows):
        qt, hd = split(i)
        q = q_ref[0, hd, pl.ds(pl.multiple_of(qt * tq, tq), tq), :]
        s_t = _dot_nt(k_ref[0, hd, rows, :], q)
        s_scr[rows, :] = s_t
        return jnp.max(s_t, axis=0, keepdims=True)

    def step(i, m, with_next):
        qt, hd = split(i)
        acc = jnp.zeros((VT_ROWS, tq), F32)
        m_next = None
        for rows in blocks:
            p = jnp.exp2(s_scr[rows, :] - m).astype(BF16)
            acc = acc + _dot(vt_ref[0, hd, :, rows], p)
            if with_next:
                bm = scores(i + 1, rows)
                m_next = bm if m_next is None else jnp.maximum(m_next, bm)
        o = acc[:MLA_V] / acc[MLA_V:MLA_V + 1]
        o_ref[0, qt, pl.ds(pl.multiple_of(hd * MLA_V, MLA_V), MLA_V), :] = o.astype(BF16)
        return m_next

    m0 = None
    for rows in blocks:
        bm = scores(0, rows)
        m0 = bm if m0 is None else jnp.maximum(m0, bm)
    m_last = lax.fori_loop(0, n_items - 1, lambda i, m: step(i, m, True), m0)
    step(n_items - 1, m_last, False)


def _attn_call(q_hm, k_hm, vt, tq):
    B, _, S, _ = q_hm.shape
    nq = S // tq
    return pl.pallas_call(
        _attn_kernel,
        grid=(B,),
        in_specs=[
            pl.BlockSpec((1, MLA_HEADS, S, LANES), lambda b: (b, 0, 0, 0)),
            pl.BlockSpec((1, MLA_HEADS, S, LANES), lambda b: (b, 0, 0, 0)),
            pl.BlockSpec((1, MLA_HEADS, VT_ROWS, S), lambda b: (b, 0, 0, 0)),
        ],
        out_specs=pl.BlockSpec((1, nq, V_W, tq), lambda b: (b, 0, 0, 0)),
        out_shape=jax.ShapeDtypeStruct((B, nq, V_W, tq), BF16),
        scratch_shapes=[pltpu.VMEM((S, tq), F32)],
        compiler_params=_cparams(("arbitrary",)),
        name="attn",
    )(q_hm, k_hm, vt)


def _hgrn_kernel(hq_ref, ff_ref, fb_ref, hi_ref, hgate_ref, lbl_ref, ng_ref, o_ref, of_scr, ob_scr, st_scr,
                 bc_scr, k_scr, qd_scr, att_scr, kv_scr, el_scr, *, hps):
    S = hq_ref.shape[1]
    C = HG_CHUNK
    G = HG_GROUP_ROWS
    cpg = G // C
    n_groups = S // G

    def lower_bound(l0, l1):
        m = jnp.maximum(l0, l1)
        e0 = jnp.exp(l0 - m)
        return e0 / (e0 + jnp.exp(l1 - m))

    lb_f = lower_bound(lbl_ref[0:1, :], lbl_ref[1:2, :])
    lb_b = lower_bound(lbl_ref[2:3, :], lbl_ref[3:4, :])

    rr = lax.broadcasted_iota(I32, (G, G), 0)
    cc = lax.broadcasted_iota(I32, (G, G), 1)
    same_chunk = (rr // C) == (cc // C)
    mask_f = same_chunk & (cc <= rr)
    mask_b = same_chunk & (cc >= rr)
    tri_f = jnp.where(mask_f, 1.0, 0.0).astype(BF16)
    tri_b = jnp.where(mask_b, 1.0, 0.0).astype(BF16)

    st_scr[...] = jnp.zeros_like(st_scr)
    W = hps * HG_DIM
    dirs = ((0, lb_f, ff_ref, of_scr, mask_f, tri_f), (1, lb_b, fb_ref, ob_scr, mask_b, tri_b))

    def rows_of(d, g):
        grp = g if d == 0 else n_groups - 1 - g
        return pl.ds(grp * G if isinstance(grp, int) else pl.multiple_of(grp * G, G), G)

    def stage_gates(g, slot):
        for d, lb, f_ref, _, _, tri in dirs:
            f = lb + (1.0 - lb) * jax.nn.sigmoid(f_ref[0, rows_of(d, g), :].astype(F32))
            lf = jnp.log(f)
            hi = lf.astype(BF16)
            lo = (lf - hi.astype(F32)).astype(BF16)
            both = _dot(tri, jnp.concatenate([hi, lo], axis=1))
            bc_scr[slot, d] = both[:, :W] + both[:, W:]
            k_scr[slot, d] = 1.0 - f

    def stage_decay(g, slot):
        for d, _, _, _, mask, _ in dirs:
            rows = rows_of(d, g)
            bc = bc_scr[slot, d]
            eb = jnp.exp(bc)
            qd = (_silu(hq_ref[0, rows, :].astype(F32)) * eb).astype(BF16)
            kd = k_scr[slot, d] * jnp.exp(-bc)
            kdb = kd.astype(BF16)
            qd_scr[slot, d] = qd
            v = hi_ref[0, rows, :]
            for j in range(hps):
                ls = slice(j * HG_DIM, (j + 1) * HG_DIM)
                ch = d * hps + j
                att_scr[slot, ch] = jnp.where(mask, _dot_nt(qd[:, ls], kdb[:, ls]), 0.0).astype(BF16)
                lasts = []
                for c in range(cpg):
                    cs = slice(c * C, (c + 1) * C)
                    last = (c + 1) * C - 1 if d == 0 else c * C
                    e_last = eb[last:last + 1, ls]
                    lasts.append(e_last)
                    k_rem = (kd[cs, ls] * e_last).astype(BF16)
                    kv_scr[slot, ch * cpg + c] = _dot_tn(v[cs, ls], k_rem)
                el_scr[slot, ch] = jnp.concatenate(lasts + [jnp.zeros((8 - cpg, HG_DIM), F32)], axis=0)

    def stage_state(g, slot):
        for d, _, _, o_scr, _, _ in dirs:
            rows = rows_of(d, g)
            qd = qd_scr[slot, d]
            v = hi_ref[0, rows, :]
            for j in range(hps):
                ls = slice(j * HG_DIM, (j + 1) * HG_DIM)
                ch = d * hps + j
                o_intra = _dot(att_scr[slot, ch], v[:, ls])
                el = el_scr[slot, ch]
                st_t = st_scr[ch]
                o_inter = [None] * cpg
                for c in (range(cpg) if d == 0 else reversed(range(cpg))):
                    o_inter[c] = _dot_nt(qd[c * C:(c + 1) * C, ls], st_t.astype(BF16))
                    st_t = st_t * el[c:c + 1, :] + kv_scr[slot, ch * cpg + c]
                st_scr[ch] = st_t
                o_scr[rows, ls] = o_intra + jnp.concatenate(o_inter, axis=0)

    def step(t, parity):
        if not isinstance(t, int) or t < n_groups:
            stage_gates(t, parity)
        if not isinstance(t, int) or 1 <= t <= n_groups:
            stage_decay(t - 1, 1 - parity)
        if not isinstance(t, int) or t >= 2:
            stage_state(t - 2, parity)

    n_steps = n_groups + 2
    head = min(2, n_steps)
    tail_start = max(head, n_groups)
    for t in range(head):
        step(t, t % 2)
    n_mid = tail_start - head
    if n_mid % 2 == 0 and n_mid > 4:
        def body(i, carry):
            t = 2 + 2 * i
            step(t, 0)
            step(t + 1, 1)
            return carry
        lax.fori_loop(0, n_mid // 2, body, 0)
    else:
        for t in range(head, tail_start):
            step(t, t % 2)
    for t in range(tail_start, n_steps):
        step(t, t % 2)

    for j in range(hps):
        ls = slice(j * HG_DIM, (j + 1) * HG_DIM)
        o = of_scr[:, ls] + ob_scr[:, ls]
        gate = _silu(hgate_ref[0, :, ls].astype(F32))
        o_ref[0, :, ls] = (_rms(o) * ng_ref[...] * gate).astype(BF16)


def _hgrn_call(hg, lb_logits, norm_g, hps):
    B, S, _ = hg.shape
    W = hps * HG_DIM
    nh = HG_HEADS // hps
    G = HG_GROUP_ROWS
    cpg = G // HG_CHUNK
    seg = lambda sg: pl.BlockSpec((1, S, W), lambda b, j, sg=sg: (b, 0, sg * nh + j))
    return pl.pallas_call(
        functools.partial(_hgrn_kernel, hps=hps),
        grid=(B, nh),
        in_specs=[
            seg(0), seg(1), seg(2), seg(3), seg(4),
            pl.BlockSpec((4, W), lambda b, j: (0, j)),
            pl.BlockSpec((1, HG_DIM), lambda b, j: (0, 0)),
        ],
        out_specs=pl.BlockSpec((1, S, W), lambda b, j: (b, 0, j)),
        out_shape=jax.ShapeDtypeStruct((B, S, HG_WIDTH), BF16),
        scratch_shapes=[
            pltpu.VMEM((S, W), F32),
            pltpu.VMEM((S, W), F32),
            pltpu.VMEM((2 * hps, HG_DIM, HG_DIM), F32),
            pltpu.VMEM((2, 2, G, W), F32),
            pltpu.VMEM((2, 2, G, W), F32),
            pltpu.VMEM((2, 2, G, W), BF16),
            pltpu.VMEM((2, 2 * hps, G, G), BF16),
            pltpu.VMEM((2, 2 * hps * cpg, HG_DIM, HG_DIM), F32),
            pltpu.VMEM((2, 2 * hps, 8, HG_DIM), F32),
        ],
        compiler_params=_cparams(("arbitrary", "arbitrary")),
        name="hgrn",
    )(hg, hg, hg, hg, hg, lb_logits, norm_g)


def _router_kernel(attn_ref, rec_ref, x_ref, mod_ref, woa_ref, wor_ref, g2_ref, wr_ref, br_ref,
                   x1_ref, h2_ref, info_ref, wts_ref, lch_ref, lg_scr):
    @pl.when(pl.program_id(0) == 0)
    def _():
        lg_scr[...] = jnp.zeros_like(lg_scr)

    _route(lg_scr[...], info_ref, wts_ref, lch_ref)

    mod = mod_ref[0]
    tm = x_ref.shape[0]
    sub = min(ROUTER_SUB_ROWS, tm)
    for r0 in range(0, tm, sub):
        rs = slice(r0, r0 + sub)
        mixed = _dot_tn(attn_ref[0, 0, :, rs], woa_ref[...]) + _dot(rec_ref[rs, :], wor_ref[...])
        x1 = x_ref[rs, :] + mod[2:3] * mixed
        x1_ref[rs, :] = x1
        h2 = _rms(x1) * g2_ref[...]
        h2 = (h2 * (1.0 + mod[4:5]) + mod[3:4]).astype(BF16)
        h2_ref[rs, :] = h2
        lg_scr[rs, :] = _dot(h2, wr_ref[...]) + br_ref[...]


def _route(logits, info_ref, wts_ref, lch_ref):
    lt = logits.T[:ROUTE_USED]
    tm = lt.shape[1]
    r = lax.broadcasted_iota(I32, (ROUTE_USED, tm), 0)
    neg = -jnp.inf
    big = ROUTE_ROWS

    isg = r < N_GROUPS
    gmax = jnp.max(jnp.where(isg, lt, neg), axis=0, keepdims=True)
    gidx = jnp.min(jnp.where(isg & (lt == gmax), r, big), axis=0, keepdims=True)
    gp = 1.0 / jnp.sum(jnp.where(isg, jnp.exp(lt - gmax), 0.0), axis=0, keepdims=True)

    ing = (r >= N_GROUPS) & (r < N_GROUPS + N_EXPERTS) & (((r - N_GROUPS) >> 3) == gidx)
    v1 = jnp.max(jnp.where(ing, lt, neg), axis=0, keepdims=True)
    i1 = jnp.min(jnp.where(ing & (lt == v1), r, big), axis=0, keepdims=True)
    ing2 = ing & (r != i1)
    v2 = jnp.max(jnp.where(ing2, lt, neg), axis=0, keepdims=True)
    i2 = jnp.min(jnp.where(ing2 & (lt == v2), r, big), axis=0, keepdims=True)
    e21 = jnp.exp(v2 - v1)
    w1 = gp / (1.0 + e21)
    w2 = gp * e21 / (1.0 + e21)

    oh1 = r == i1
    oh2 = r == i2
    oh = jnp.where(oh1 | oh2, 1.0, 0.0)
    ohb = oh.astype(BF16)
    tt = lax.broadcasted_iota(I32, (tm, tm), 0)
    tc = lax.broadcasted_iota(I32, (tm, tm), 1)
    upper = jnp.where(tt < tc, 1.0, 0.0).astype(BF16)
    before = _dot(ohb, upper)
    cnt_col = jnp.sum(oh, axis=1, keepdims=True)
    pad_col = jnp.ceil(cnt_col * (1.0 / ROW_CHUNK)) * ROW_CHUNK
    er = lax.broadcasted_iota(I32, (ROUTE_USED, ROUTE_USED), 0)
    ec = lax.broadcasted_iota(I32, (ROUTE_USED, ROUTE_USED), 1)
    lower = jnp.where(ec < er, 1.0, 0.0).astype(BF16)
    start_col = _dot(lower, jnp.broadcast_to(pad_col, (ROUTE_USED, LANES)).astype(BF16))[:, 0:1]
    pos = before + start_col
    slot1 = jnp.sum(jnp.where(oh1, pos, 0.0), axis=0, keepdims=True)
    slot2 = jnp.sum(jnp.where(oh2, pos, 0.0), axis=0, keepdims=True)

    zero = jnp.zeros((1, tm), I32)
    info_ref[0] = jnp.concatenate(
        [slot1.astype(I32), slot2.astype(I32), i1 - N_GROUPS, i2 - N_GROUPS, zero, zero, zero, zero], axis=0)
    rows = [jnp.broadcast_to(v, (COL_REP, tm)) for v in (w1, w2, slot1, slot2)]
    cols = jnp.concatenate(rows, axis=0).T
    wts_ref[...] = cols
    ohb_all = jnp.concatenate([ohb, jnp.zeros((ROUTE_ROWS - ROUTE_USED, tm), BF16)], axis=0)
    cnt_row = _dot_nt(jnp.ones((8, tm), BF16), ohb_all)
    lch_ref[0] = jnp.ceil(cnt_row * (1.0 / ROW_CHUNK)).astype(I32)


def _router_call(attn_t, rec, x2, mod3, wo_a, wo_r, g2, wr, br, tm, tiles_per_batch):
    T, D = x2.shape
    nt = T // tm
    tiles_per_q = attn_t.shape[3] // tm
    full = lambda shape: pl.BlockSpec(shape, lambda i: (0,) * len(shape))
    cur = lambda i: jnp.minimum(i, nt - 1)
    prev = lambda i: jnp.maximum(i - 1, 0)
    return pl.pallas_call(
        _router_kernel,
        grid=(nt + 1,),
        in_specs=[
            pl.BlockSpec((1, 1, V_W, tm), lambda i: (cur(i) // tiles_per_batch,
                                                     (cur(i) % tiles_per_batch) // tiles_per_q,
                                                     0, (cur(i) % tiles_per_batch) % tiles_per_q)),
            pl.BlockSpec((tm, HG_WIDTH), lambda i: (cur(i), 0)),
            pl.BlockSpec((tm, D), lambda i: (cur(i), 0)),
            pl.BlockSpec((1, 6, D), lambda i: (cur(i) // tiles_per_batch, 0, 0)),
            full((V_W, D)),
            full((HG_WIDTH, D)),
            full((1, D)),
            full((D, ROUTE_ROWS)),
            full((1, ROUTE_ROWS)),
        ],
        out_specs=[
            pl.BlockSpec((tm, D), lambda i: (cur(i), 0)),
            pl.BlockSpec((tm, D), lambda i: (cur(i), 0)),
            pl.BlockSpec((1, 8, tm), lambda i: (prev(i), 0, 0)),
            pl.BlockSpec((tm, LANES), lambda i: (prev(i), 0)),
            pl.BlockSpec((1, 8, ROUTE_ROWS), lambda i: (prev(i), 0, 0)),
        ],
        out_shape=[
            jax.ShapeDtypeStruct((T, D), F32),
            jax.ShapeDtypeStruct((T, D), BF16),
            jax.ShapeDtypeStruct((nt, 8, tm), I32),
            jax.ShapeDtypeStruct((T, LANES), F32),
            jax.ShapeDtypeStruct((nt, 8, ROUTE_ROWS), I32),
        ],
        scratch_shapes=[pltpu.VMEM((tm, ROUTE_ROWS), F32)],
        compiler_params=_cparams(("arbitrary",)),
        name="router",
    )(attn_t, rec, x2, mod3, wo_a, wo_r, g2, wr, br)


def _local_rows(tm):
    need = TOP_K * tm + N_EXPERTS * (ROW_CHUNK - 1)
    return -(-need // MXU_DIM) * MXU_DIM


def _dispatch_kernel(dst_ref, info_ref, h2_ref, xb_ref, xs_scr, sem, *, n_real):
    i = pl.program_id(0)
    n = pl.num_programs(0)
    slot = i % 2
    LF = xs_scr.shape[1]
    L = LF // 2
    tm = h2_ref.shape[0]

    def whole(sl):
        return pltpu.make_async_copy(xs_scr.at[sl], xb_ref.at[pl.ds(0, LF)], sem.at[sl])

    @pl.when(i >= 2)
    def _():
        whole(slot).wait()

    @pl.when(i < n_real)
    def _():
        s = lax.broadcasted_iota(I32, (L, tm), 0)
        hit = (s == info_ref[0, 0:1, :]) | (s == info_ref[0, 1:2, :])
        xs_scr[slot] = _fold_rows(_dot(jnp.where(hit, 1.0, 0.0).astype(BF16), h2_ref[...]))

    @pl.when(i >= n_real)
    def _():
        xs_scr[slot] = jnp.zeros((LF, xs_scr.shape[2]), BF16)

    for c in range(LF // FOLD_CHUNK):
        d = pl.multiple_of(dst_ref[i, c] * FOLD_CHUNK, FOLD_CHUNK)
        pltpu.make_async_copy(xs_scr.at[slot, pl.ds(c * FOLD_CHUNK, FOLD_CHUNK)],
                              xb_ref.at[pl.ds(d, FOLD_CHUNK)], sem.at[slot]).start()

    @pl.when(i == n - 1)
    def _():
        whole(slot).wait()
        whole(1 - slot).wait()


def _dispatch_call(dst_table, info, h2, cap_rows, tm):
    T, D = h2.shape
    nt = T // tm
    n_steps = dst_table.shape[0]
    L = _local_rows(tm)
    last = nt - 1
    grid_spec = pltpu.PrefetchScalarGridSpec(
        num_scalar_prefetch=1,
        grid=(n_steps,),
        in_specs=[
            pl.BlockSpec((1, 8, tm), lambda i, dst: (jnp.minimum(i, last), 0, 0)),
            pl.BlockSpec((tm, D), lambda i, dst: (jnp.minimum(i, last), 0)),
        ],
        out_specs=pl.BlockSpec(memory_space=pl.ANY),
        scratch_shapes=[
            pltpu.VMEM((2, 2 * L, D // 2), BF16),
            pltpu.SemaphoreType.DMA((2,)),
        ],
    )
    return pl.pallas_call(
        functools.partial(_dispatch_kernel, n_real=nt),
        grid_spec=grid_spec,
        out_shape=jax.ShapeDtypeStruct((2 * cap_rows, D // 2), BF16),
        compiler_params=_cparams(("arbitrary",)),
        name="dispatch",
    )(dst_table, info, h2)


def _expert_kernel(te_ref, tv_ref, xb_ref, wg_ref, wu_ref, wd_ref, y_ref, wg_b, wu_b, wd_b):
    j = pl.program_id(0)
    prev = te_ref[jnp.maximum(j - 1, 0)]

    @pl.when((j == 0) | (te_ref[j] != prev))
    def _():
        wg_b[...] = wg_ref[0].astype(BF16)
        wu_b[...] = wu_ref[0].astype(BF16)
        wd_b[...] = wd_ref[0].astype(BF16)

    @pl.when(tv_ref[j] > 0)
    def _():
        x = jnp.concatenate(_unfold_rows(xb_ref[...]), axis=1)
        hid = (_silu(_dot(x, wg_b[...])) * _dot(x, wu_b[...])).astype(BF16)
        y_ref[...] = _fold_rows(_dot(hid, wd_b[...]))

    @pl.when(tv_ref[j] == 0)
    def _():
        y_ref[...] = jnp.zeros_like(y_ref)


def _expert_call(tile_expert, tile_valid, xb, w_gate, w_up, w_down):
    cap, DP = xb.shape
    D = 2 * DP
    nt = tile_expert.shape[0]
    grid_spec = pltpu.PrefetchScalarGridSpec(
        num_scalar_prefetch=2,
        grid=(nt,),
        in_specs=[
            pl.BlockSpec((2 * EXPERT_TILE, DP), lambda j, te, tv: (j * tv[j], 0)),
            pl.BlockSpec((1, D, D_EXPERT), lambda j, te, tv: (te[j], 0, 0)),
            pl.BlockSpec((1, D, D_EXPERT), lambda j, te, tv: (te[j], 0, 0)),
            pl.BlockSpec((1, D_EXPERT, D), lambda j, te, tv: (te[j], 0, 0)),
        ],
        out_specs=pl.BlockSpec((2 * EXPERT_TILE, DP), lambda j, te, tv: (j, 0)),
        scratch_shapes=[
            pltpu.VMEM((D, D_EXPERT), BF16),
            pltpu.VMEM((D, D_EXPERT), BF16),
            pltpu.VMEM((D_EXPERT, D), BF16),
        ],
    )
    return pl.pallas_call(
        _expert_kernel,
        grid_spec=grid_spec,
        out_shape=jax.ShapeDtypeStruct((cap, DP), BF16),
        compiler_params=_cparams(("arbitrary",)),
        name="expert",
    )(tile_expert, tile_valid, xb, w_gate, w_up, w_down)


def _combine_kernel(src_ref, x1_ref, mod_ref, wts_ref, yb_ref, o_ref, ys_scr, sem):
    i = pl.program_id(0)
    n = pl.num_programs(0)
    slot = i % 2
    LF = ys_scr.shape[1]
    L = LF // 2
    tm = x1_ref.shape[0]

    def fetch(step, sl):
        for c in range(LF // FOLD_CHUNK):
            d = pl.multiple_of(src_ref[step, c] * FOLD_CHUNK, FOLD_CHUNK)
            pltpu.make_async_copy(yb_ref.at[pl.ds(d, FOLD_CHUNK)],
                                  ys_scr.at[sl, pl.ds(c * FOLD_CHUNK, FOLD_CHUNK)], sem.at[sl]).start()

    @pl.when(i == 0)
    def _():
        fetch(0, 0)

    @pl.when(i + 1 < n)
    def _():
        fetch(i + 1, 1 - slot)

    pltpu.make_async_copy(yb_ref.at[pl.ds(0, LF)], ys_scr.at[slot], sem.at[slot]).wait()

    s = lax.broadcasted_iota(I32, (tm, L), 1).astype(F32)
    w1, w2, slot1, slot2 = (wts_ref[:, j * COL_REP:j * COL_REP + 1] for j in range(4))
    pick = (jnp.where(s == slot1, w1, 0.0) + jnp.where(s == slot2, w2, 0.0)).astype(BF16)
    y_left, y_right = _unfold_rows(ys_scr[slot])
    moe = jnp.concatenate([_dot(pick, y_left), _dot(pick, y_right)], axis=1)
    o_ref[...] = x1_ref[...] + mod_ref[0][5:6] * moe


def _combine_call(src_table, x1, mod3, wts, yb, tm, tiles_per_batch):
    T, D = x1.shape
    nt = T // tm
    L = _local_rows(tm)
    grid_spec = pltpu.PrefetchScalarGridSpec(
        num_scalar_prefetch=1,
        grid=(nt,),
        in_specs=[
            pl.BlockSpec((tm, D), lambda i, src: (i, 0)),
            pl.BlockSpec((1, 6, D), lambda i, src: (i // tiles_per_batch, 0, 0)),
            pl.BlockSpec((tm, LANES), lambda i, src: (i, 0)),
            pl.BlockSpec(memory_space=pl.ANY),
        ],
        out_specs=pl.BlockSpec((tm, D), lambda i, src: (i, 0)),
        scratch_shapes=[
            pltpu.VMEM((2, 2 * L, D // 2), BF16),
            pltpu.SemaphoreType.DMA((2,)),
        ],
    )
    return pl.pallas_call(
        _combine_kernel,
        grid_spec=grid_spec,
        out_shape=jax.ShapeDtypeStruct((T, D), F32),
        compiler_params=_cparams(("arbitrary",)),
        name="combine",
    )(src_table, x1, mod3, wts, yb)


def _rotate_half_matrix():
    half = MLA_ROPE // 2
    p = np.zeros((MLA_ROPE, MLA_ROPE), np.float32)
    for j in range(half):
        p[j + half, j] = -1.0
        p[j, j + half] = 1.0
    return jnp.asarray(p)


def _slot_gain(g, factor):
    half = MLA_ROPE // 2
    g_n, g_r = g[:MLA_NOPE], g[MLA_NOPE:]
    slot = jnp.concatenate([g_n, g_r, g_r[half:], g_r[:half]]) * factor
    return jnp.tile(slot, MLA_HEADS)[None, :].astype(F32)


def _prepare_weights(w_in, wq_up, wkv_up, qn_g, kn_g):
    rot = _rotate_half_matrix()
    s0 = MLA_Q_LORA
    s1 = s0 + MLA_KV_LORA
    s2 = s1 + MLA_ROPE
    w_kr = w_in[:, s1:s2]
    kr_slot = jnp.concatenate([jnp.zeros((D_MODEL, MLA_NOPE), F32), w_kr, w_kr @ rot], axis=1)
    win_p = jnp.concatenate([w_in[:, :s1], kr_slot, w_in[:, s2:]], axis=1).astype(BF16)

    wq = wq_up.reshape(MLA_Q_LORA, MLA_HEADS, MLA_QK)
    wq_r = wq[:, :, MLA_NOPE:]
    wq_ext = jnp.concatenate([wq, jnp.einsum('lhr,rs->lhs', wq_r, rot)], axis=2)
    wq_ext = wq_ext.reshape(MLA_Q_LORA, SLOT_W).astype(BF16)

    wkv = wkv_up.reshape(MLA_KV_LORA, MLA_HEADS, MLA_NOPE + MLA_V)
    wk_slot = jnp.concatenate([wkv[:, :, :MLA_NOPE], jnp.zeros((MLA_KV_LORA, MLA_HEADS, LANES - MLA_NOPE), F32)],
                              axis=2).reshape(MLA_KV_LORA, SLOT_W)
    wv = wkv[:, :, MLA_NOPE:].reshape(MLA_KV_LORA, V_W)
    wkv_ext = jnp.concatenate([wk_slot, wv], axis=1).astype(BF16)

    gq = _slot_gain(qn_g, MLA_QK ** -0.5 * math.log2(math.e))
    gk = _slot_gain(kn_g, 1.0)
    return win_p, wq_ext, wkv_ext, gq, gk


def _chunk_tables(lch, tm):
    nt = lch.shape[0]
    lchunks = _local_rows(tm) // ROW_CHUNK
    tch = EXPERT_TILE // ROW_CHUNK
    n_pad_steps = -(-(N_EXPERTS * (tch - 1)) // lchunks)
    while ((nt + n_pad_steps) * lchunks) % tch:
        n_pad_steps += 1
    n_steps = nt + n_pad_steps
    total = n_steps * lchunks
    n_tiles = total // tch

    nch = lch[:, 0, N_GROUPS:N_GROUPS + N_EXPERTS]
    lend = jnp.cumsum(nch, axis=1)
    lstart = lend - nch
    ltot = lend[:, -1]
    tot = jnp.sum(nch, axis=0)
    ptot = (tot + tch - 1) // tch * tch
    gend = jnp.cumsum(ptot)
    gstart = gend - ptot
    toff = jnp.cumsum(nch, axis=0) - nch
    c = jnp.arange(lchunks, dtype=I32)

    def lookup(idx, table):
        hit = idx[..., None] == jnp.arange(table.shape[-1], dtype=I32)
        return jnp.sum(jnp.where(hit, table, 0), axis=-1)

    e_of = jnp.minimum(jnp.sum(c[None, :, None] >= lend[:, None, :], axis=2), N_EXPERTS - 1)
    dest = lookup(e_of, (gstart[None, :] + toff - lstart)[:, None, :]) + c[None, :]
    valid = c[None, :] < ltot[:, None]
    src_table = jnp.where(valid, dest, 0).astype(I32)

    gap = jnp.concatenate([ptot - tot, total - gend[-1:]])
    gap_first = jnp.concatenate([gstart + tot, gend[-1:]])
    gap_cum = jnp.cumsum(gap)
    unused = lchunks - ltot
    rank = jnp.concatenate([
        (jnp.cumsum(unused) - unused)[:, None] + c[None, :] - ltot[:, None],
        jnp.sum(unused) + jnp.arange(n_pad_steps, dtype=I32)[:, None] * lchunks + c[None, :]])
    g_of = jnp.minimum(jnp.sum(rank[:, :, None] >= gap_cum[None, None, :], axis=2), N_EXPERTS)
    filler = lookup(g_of, gap_first - (gap_cum - gap)) + rank
    routed = jnp.concatenate([valid, jnp.zeros((n_pad_steps, lchunks), bool)])
    dst_table = jnp.where(routed, jnp.concatenate([dest, jnp.zeros((n_pad_steps, lchunks), I32)]), filler).astype(I32)

    ts = jnp.arange(n_tiles, dtype=I32) * tch
    tile_expert = jnp.minimum(jnp.sum(ts[:, None] >= gend[None, :], axis=1), N_EXPERTS - 1).astype(I32)
    tile_valid = (ts < gend[-1]).astype(I32)
    return dst_table, src_table, tile_expert, tile_valid, total * ROW_CHUNK


def _pick(n, pref):
    return pref if n % pref == 0 else n


def kernel(x, c, positions, ada_w, ada_b, norm1_g, w_in, mla_qa_g, mla_wq_up, mla_kva_g, mla_wkv_up, mla_qn_g, mla_kn_g, hg_lb_logits, hg_norm_g, w_out, norm2_g, router_group_w, router_group_b, router_expert_w, router_expert_b, w_gate, w_up, w_down):
    B, S, D = x.shape
    assert D == D_MODEL and ada_w.shape[0] == 1, "one layer of width 1024"
    T = B * S
    tm = _pick(S, 512)
    tiles_per_batch = S // tm

    mod3 = _mod_call(c, ada_w[0], ada_b).reshape(B, 6, D)

    win_p, wq_ext, wkv_ext, gq, gk = _prepare_weights(w_in[0], mla_wq_up[0], mla_wkv_up[0], mla_qn_g[0], mla_kn_g[0])
    q_hm, k_hm, vt, hg = _proj_call(x, mod3, norm1_g, win_p, mla_qa_g, wq_ext, mla_kva_g, wkv_ext, gq, gk,
                                    positions, tm)
    tq = _pick(S, ATTN_Q_TILE)
    attn_t = _attn_call(q_hm, k_hm, vt, tq)
    rec = _hgrn_call(hg, hg_lb_logits.reshape(2 * hg_lb_logits.shape[1], HG_WIDTH), hg_norm_g, hps=4)

    wo = w_out[0].astype(BF16)
    wr = jnp.concatenate([router_group_w[0], router_expert_w[0],
                          jnp.zeros((D, ROUTE_ROWS - N_GROUPS - N_EXPERTS), F32)], axis=1).astype(BF16)
    br = jnp.concatenate([router_group_b[0], router_expert_b[0],
                          jnp.zeros((ROUTE_ROWS - N_GROUPS - N_EXPERTS,), F32)])[None, :]
    x1, h2, info, wts, lch = _router_call(attn_t, rec.reshape(T, HG_WIDTH), x.reshape(T, D), mod3,
                                          wo[:V_W], wo[V_W:], norm2_g, wr, br, tm, tiles_per_batch)

    dst_table, src_table, tile_expert, tile_valid, cap_rows = _chunk_tables(lch, tm)
    xb = _dispatch_call(dst_table, info, h2, cap_rows, tm)
    yb = _expert_call(tile_expert, tile_valid, xb, w_gate[0], w_up[0], w_down[0])
    out = _combine_call(src_table, x1, mod3, wts, yb, tm, tiles_per_batch)
    return out.reshape(B, S, D)
```

```python
import functools
import math

import numpy as np
import jax
import jax.numpy as jnp
from jax import lax
from jax.experimental import pallas as pl
from jax.experimental.pallas import tpu as pltpu

F32 = jnp.float32
BF16 = jnp.bfloat16
I32 = jnp.int32

D_MODEL = 1024
MLA_HEADS = 8
MLA_NOPE = 64
MLA_ROPE = 32
MLA_QK = MLA_NOPE + MLA_ROPE
MLA_V = 64
MLA_Q_LORA = 384
MLA_KV_LORA = 256
ROPE_THETA = 10000.0
HG_HEADS = 4
HG_DIM = 128
HG_WIDTH = HG_HEADS * HG_DIM
HG_CHUNK = 64
N_GROUPS = 4
EPG = 8
N_EXPERTS = N_GROUPS * EPG
TOP_K = 2
D_EXPERT = 512
EPS = 1e-6

LANES = 128
VMEM_LIMIT = 56 * 1024 * 1024

SLOT_W = MLA_HEADS * LANES
V_W = MLA_HEADS * MLA_V
HG_SEGS = 5
W_IN_COLS = MLA_Q_LORA + MLA_KV_LORA + LANES + HG_SEGS * HG_WIDTH
HG_GROUP_ROWS = 4 * HG_CHUNK
ONES_ROWS = 16
VT_ROWS = MLA_V + ONES_ROWS
ATTN_KEY_BLOCK = 256
ATTN_Q_TILE = 1024
ROUTE_ROWS = 128
ROUTER_SUB_ROWS = 256
ROUTE_USED = 48
EXPERT_TILE = 1024
MXU_DIM = 256
ROW_CHUNK = 8
FOLD_CHUNK = 2 * ROW_CHUNK
COL_REP = LANES // 4


def _cparams(sem, vmem=VMEM_LIMIT):
    return pltpu.CompilerParams(dimension_semantics=sem, vmem_limit_bytes=vmem)


def _rms(x):
    return x * lax.rsqrt(jnp.mean(x * x, axis=-1, keepdims=True) + EPS)


def _silu(x):
    return x * jax.nn.sigmoid(x)


def _dot(a, b):
    return jnp.dot(a, b, preferred_element_type=F32)


def _dot_nt(a, b):
    return lax.dot_general(a, b, (((1,), (1,)), ((), ())), preferred_element_type=F32)


def _dot_tn(a, b):
    return lax.dot_general(a, b, (((0,), (0,)), ((), ())), preferred_element_type=F32)


def _fold_rows(v):
    r, c2 = v.shape
    c = c2 // 2
    left = v[:, :c].reshape(r // ROW_CHUNK, 1, ROW_CHUNK, c)
    right = v[:, c:].reshape(r // ROW_CHUNK, 1, ROW_CHUNK, c)
    return jnp.concatenate([left, right], axis=1).reshape(2 * r, c).astype(BF16)


def _unfold_rows(p):
    r2, c = p.shape
    r = r2 // 2
    q = p.astype(F32).reshape(r // ROW_CHUNK, 2, ROW_CHUNK, c)
    return q[:, 0].reshape(r, c).astype(BF16), q[:, 1].reshape(r, c).astype(BF16)


def _mod_kernel(c_ref, w_ref, b_ref, o_ref):
    a = _silu(c_ref[...]).astype(BF16)
    o_ref[...] = _dot(a, w_ref[...].astype(BF16)) + b_ref[...]


def _mod_call(c, ada_w, ada_b):
    B = c.shape[0]
    n = ada_w.shape[1] // D_MODEL
    return pl.pallas_call(
        _mod_kernel,
        grid=(n,),
        in_specs=[
            pl.BlockSpec((B, D_MODEL), lambda j: (0, 0)),
            pl.BlockSpec((D_MODEL, D_MODEL), lambda j: (0, j)),
            pl.BlockSpec((1, D_MODEL), lambda j: (0, j)),
        ],
        out_specs=pl.BlockSpec((B, D_MODEL), lambda j: (0, j)),
        out_shape=jax.ShapeDtypeStruct((B, n * D_MODEL), F32),
        compiler_params=_cparams(("arbitrary",)),
        name="mod",
    )(c, ada_w, ada_b)


def _rope_tables(pos_row):
    half = MLA_ROPE // 2
    tm = pos_row.shape[1]
    idx = lax.broadcasted_iota(I32, (half, tm), 0).astype(F32)
    inv_freq = ROPE_THETA ** (-idx / half)
    ang = pos_row.astype(F32) * inv_freq
    c = jnp.cos(ang)
    s = jnp.sin(ang)
    ct = jnp.concatenate([jnp.ones((MLA_NOPE, tm), F32), c, c, jnp.zeros((MLA_ROPE, tm), F32)], axis=0)
    st = jnp.concatenate([jnp.zeros((MLA_QK, tm), F32), s, s], axis=0)
    return ct.T, st.T


def _head_finish(t, g_row, ct, st, lane_valid):
    ss = jnp.sum(jnp.where(lane_valid, t * t, 0.0), axis=-1, keepdims=True) * (1.0 / MLA_QK)
    tg = t * lax.rsqrt(ss + EPS) * g_row
    return tg * ct + pltpu.roll(tg * st, LANES - MLA_ROPE, 1)


def _proj_kernel(x_ref, mod_ref, g1_ref, win_ref, qag_ref, wq_ref, kvag_ref, wkv_ref, gq_ref, gk_ref,
                 pos_ref, q_out, k_out, vt_out, hg_out):
    x = x_ref[0]
    mod = mod_ref[0]
    h = _rms(x) * g1_ref[...]
    h = h * (1.0 + mod[1:2]) + mod[0:1]
    hb = h.astype(BF16)

    ct, st = _rope_tables(pos_ref[0])
    tm = x.shape[0]
    lane_valid = lax.broadcasted_iota(I32, (tm, LANES), 1) < MLA_QK

    c0 = MLA_Q_LORA
    c1 = c0 + MLA_KV_LORA
    c2 = c1 + LANES

    q_lat = _dot(hb, win_ref[:, 0:c0])
    qn = (_rms(q_lat) * qag_ref[...]).astype(BF16)
    qe = _dot(qn, wq_ref[...])
    for hd in range(MLA_HEADS):
        sl = slice(hd * LANES, (hd + 1) * LANES)
        q_out[0, hd] = _head_finish(qe[:, sl], gq_ref[:, sl], ct, st, lane_valid).astype(BF16)

    kv_lat = _dot(hb, win_ref[:, c0:c1])
    kvn = (_rms(kv_lat) * kvag_ref[...]).astype(BF16)
    ke = _dot(kvn, wkv_ref[:, 0:SLOT_W])
    kr = _dot(hb, win_ref[:, c1:c2])
    for hd in range(MLA_HEADS):
        sl = slice(hd * LANES, (hd + 1) * LANES)
        k_out[0, hd] = _head_finish(ke[:, sl] + kr, gk_ref[:, sl], ct, st, lane_valid).astype(BF16)

    v_t = _dot(kvn, wkv_ref[:, SLOT_W:SLOT_W + V_W]).T.astype(BF16)
    ones = jnp.ones((ONES_ROWS, tm), BF16)
    for hd in range(MLA_HEADS):
        vt_out[0, hd, 0:MLA_V, :] = v_t[hd * MLA_V:(hd + 1) * MLA_V, :]
        vt_out[0, hd, MLA_V:VT_ROWS, :] = ones

    for sg in range(HG_SEGS):
        lo = c2 + sg * HG_WIDTH
        hg_out[0, :, sg * HG_WIDTH:(sg + 1) * HG_WIDTH] = _dot(hb, win_ref[:, lo:lo + HG_WIDTH]).astype(BF16)


def _proj_call(x, mod3, g1, win_p, qag, wq_ext, kvag, wkv_ext, gq, gk, positions, tm):
    B, S, D = x.shape
    nt = S // tm
    pos3 = positions.reshape(B, 1, S)
    full = lambda shape: pl.BlockSpec(shape, lambda b, i: (0,) * len(shape))
    return pl.pallas_call(
        _proj_kernel,
        grid=(B, nt),
        in_specs=[
            pl.BlockSpec((1, tm, D), lambda b, i: (b, i, 0)),
            pl.BlockSpec((1, 6, D), lambda b, i: (b, 0, 0)),
            full((1, D)),
            full((D, W_IN_COLS)),
            full((1, MLA_Q_LORA)),
            full((MLA_Q_LORA, SLOT_W)),
            full((1, MLA_KV_LORA)),
            full((MLA_KV_LORA, SLOT_W + V_W)),
            full((1, SLOT_W)),
            full((1, SLOT_W)),
            pl.BlockSpec((1, 1, tm), lambda b, i: (b, 0, i)),
        ],
        out_specs=[
            pl.BlockSpec((1, MLA_HEADS, tm, LANES), lambda b, i: (b, 0, i, 0)),
            pl.BlockSpec((1, MLA_HEADS, tm, LANES), lambda b, i: (b, 0, i, 0)),
            pl.BlockSpec((1, MLA_HEADS, VT_ROWS, tm), lambda b, i: (b, 0, 0, i)),
            pl.BlockSpec((1, tm, HG_SEGS * HG_WIDTH), lambda b, i: (b, i, 0)),
        ],
        out_shape=[
            jax.ShapeDtypeStruct((B, MLA_HEADS, S, LANES), BF16),
            jax.ShapeDtypeStruct((B, MLA_HEADS, S, LANES), BF16),
            jax.ShapeDtypeStruct((B, MLA_HEADS, VT_ROWS, S), BF16),
            jax.ShapeDtypeStruct((B, S, HG_SEGS * HG_WIDTH), BF16),
        ],
        compiler_params=_cparams(("arbitrary", "arbitrary")),
        name="proj",
    )(x, mod3, g1, win_p, qag, wq_ext, kvag, wkv_ext, gq, gk, pos3)


def _attn_kernel(q_ref, k_ref, vt_ref, o_ref, s_scr):
    S = k_ref.shape[2]
    tq = o_ref.shape[3]
    n_items = o_ref.shape[1] * MLA_HEADS
    kb_rows = min(ATTN_KEY_BLOCK, S)
    blocks = [slice(b * kb_rows, (b + 1) * kb_rows) for b in range(S // kb_rows)]

    def split(i):
        return i // MLA_HEADS, i % MLA_HEADS

    def scores(i, rows):
        qt, hd = split(i)
        q = q_ref[0, hd, pl.ds(pl.multiple_of(qt * tq, tq), tq), :]
        s_t = _dot_nt(k_ref[0, hd, rows, :], q)
        s_scr[rows, :] = s_t
        return jnp.max(s_t, axis=0, keepdims=True)

    def step(i, m, with_next):
        qt, hd = split(i)
        acc = jnp.zeros((VT_ROWS, tq), F32)
        m_next = None
        for rows in blocks:
            p = jnp.exp2(s_scr[rows, :] - m).astype(BF16)
            acc = acc + _dot(vt_ref[0, hd, :, rows], p)
            if with_next:
                bm = scores(i + 1, rows)
                m_next = bm if m_next is None else jnp.maximum(m_next, bm)
        o = acc[:MLA_V] / acc[MLA_V:MLA_V + 1]
        o_ref[0, qt, pl.ds(pl.multiple_of(hd * MLA_V, MLA_V), MLA_V), :] = o.astype(BF16)
        return m_next

    m0 = None
    for rows in blocks:
        bm = scores(0, rows)
        m0 = bm if m0 is None else jnp.maximum(m0, bm)
    m_last = lax.fori_loop(0, n_items - 1, lambda i, m: step(i, m, True), m0)
    step(n_items - 1, m_last, False)


def _attn_call(q_hm, k_hm, vt, tq):
    B, _, S, _ = q_hm.shape
    nq = S // tq
    return pl.pallas_call(
        _attn_kernel,
        grid=(B,),
        in_specs=[
            pl.BlockSpec((1, MLA_HEADS, S, LANES), lambda b: (b, 0, 0, 0)),
            pl.BlockSpec((1, MLA_HEADS, S, LANES), lambda b: (b, 0, 0, 0)),
            pl.BlockSpec((1, MLA_HEADS, VT_ROWS, S), lambda b: (b, 0, 0, 0)),
        ],
        out_specs=pl.BlockSpec((1, nq, V_W, tq), lambda b: (b, 0, 0, 0)),
        out_shape=jax.ShapeDtypeStruct((B, nq, V_W, tq), BF16),
        scratch_shapes=[pltpu.VMEM((S, tq), F32)],
        compiler_params=_cparams(("arbitrary",)),
        name="attn",
    )(q_hm, k_hm, vt)


def _hgrn_kernel(hq_ref, ff_ref, fb_ref, hi_ref, hgate_ref, lbl_ref, ng_ref, o_ref, of_scr, ob_scr, st_scr,
                 bc_scr, k_scr, qd_scr, att_scr, kv_scr, el_scr, *, hps):
    S = hq_ref.shape[1]
    C = HG_CHUNK
    G = HG_GROUP_ROWS
    cpg = G // C
    n_groups = S // G

    def lower_bound(l0, l1):
        m = jnp.maximum(l0, l1)
        e0 = jnp.exp(l0 - m)
        return e0 / (e0 + jnp.exp(l1 - m))

    lb_f = lower_bound(lbl_ref[0:1, :], lbl_ref[1:2, :])
    lb_b = lower_bound(lbl_ref[2:3, :], lbl_ref[3:4, :])

    rr = lax.broadcasted_iota(I32, (G, G), 0)
    cc = lax.broadcasted_iota(I32, (G, G), 1)
    same_chunk = (rr // C) == (cc // C)
    mask_f = same_chunk & (cc <= rr)
    mask_b = same_chunk & (cc >= rr)
    tri_f = jnp.where(mask_f, 1.0, 0.0).astype(BF16)
    tri_b = jnp.where(mask_b, 1.0, 0.0).astype(BF16)

    st_scr[...] = jnp.zeros_like(st_scr)
    W = hps * HG_DIM
    dirs = ((0, lb_f, ff_ref, of_scr, mask_f, tri_f), (1, lb_b, fb_ref, ob_scr, mask_b, tri_b))

    def rows_of(d, g):
        grp = g if d == 0 else n_groups - 1 - g
        return pl.ds(grp * G if isinstance(grp, int) else pl.multiple_of(grp * G, G), G)

    def stage_gates(g, slot):
        for d, lb, f_ref, _, _, tri in dirs:
            f = lb + (1.0 - lb) * jax.nn.sigmoid(f_ref[0, rows_of(d, g), :].astype(F32))
            lf = jnp.log(f)
            hi = lf.astype(BF16)
            lo = (lf - hi.astype(F32)).astype(BF16)
            both = _dot(tri, jnp.concatenate([hi, lo], axis=1))
            bc_scr[slot, d] = both[:, :W] + both[:, W:]
            k_scr[slot, d] = 1.0 - f

    def stage_decay(g, slot):
        for d, _, _, _, mask, _ in dirs:
            rows = rows_of(d, g)
            bc = bc_scr[slot, d]
            eb = jnp.exp(bc)
            qd = (_silu(hq_ref[0, rows, :].astype(F32)) * eb).astype(BF16)
            kd = k_scr[slot, d] * jnp.exp(-bc)
            kdb = kd.astype(BF16)
            qd_scr[slot, d] = qd
            v = hi_ref[0, rows, :]
            for j in range(hps):
                ls = slice(j * HG_DIM, (j + 1) * HG_DIM)
                ch = d * hps + j
                att_scr[slot, ch] = jnp.where(mask, _dot_nt(qd[:, ls], kdb[:, ls]), 0.0).astype(BF16)
                lasts = []
                for c in range(cpg):
                    cs = slice(c * C, (c + 1) * C)
                    last = (c + 1) * C - 1 if d == 0 else c * C
                    e_last = eb[last:last + 1, ls]
                    lasts.append(e_last)
                    k_rem = (kd[cs, ls] * e_last).astype(BF16)
                    kv_scr[slot, ch * cpg + c] = _dot_tn(v[cs, ls], k_rem)
                el_scr[slot, ch] = jnp.concatenate(lasts + [jnp.zeros((8 - cpg, HG_DIM), F32)], axis=0)

    def stage_state(g, slot):
        for d, _, _, o_scr, _, _ in dirs:
            rows = rows_of(d, g)
            qd = qd_scr[slot, d]
            v = hi_ref[0, rows, :]
            for j in range(hps):
                ls = slice(j * HG_DIM, (j + 1) * HG_DIM)
                ch = d * hps + j
                o_intra = _dot(att_scr[slot, ch], v[:, ls])
                el = el_scr[slot, ch]
                st_t = st_scr[ch]
                o_inter = [None] * cpg
                for c in (range(cpg) if d == 0 else reversed(range(cpg))):
                    o_inter[c] = _dot_nt(qd[c * C:(c + 1) * C, ls], st_t.astype(BF16))
                    st_t = st_t * el[c:c + 1, :] + kv_scr[slot, ch * cpg + c]
                st_scr[ch] = st_t
                o_scr[rows, ls] = o_intra + jnp.concatenate(o_inter, axis=0)

    def step(t, parity):
        if not isinstance(t, int) or t < n_groups:
            stage_gates(t, parity)
        if not isinstance(t, int) or 1 <= t <= n_groups:
            stage_decay(t - 1, 1 - parity)
        if not isinstance(t, int) or t >= 2:
            stage_state(t - 2, parity)

    n_steps = n_groups + 2
    head = min(2, n_steps)
    tail_start = max(head, n_groups)
    for t in range(head):
        step(t, t % 2)
    n_mid = tail_start - head
    if n_mid % 2 == 0 and n_mid > 4:
        def body(i, carry):
            t = 2 + 2 * i
            step(t, 0)
            step(t + 1, 1)
            return carry
        lax.fori_loop(0, n_mid // 2, body, 0)
    else:
        for t in range(head, tail_start):
            step(t, t % 2)
    for t in range(tail_start, n_steps):
        step(t, t % 2)

    for j in range(hps):
        ls = slice(j * HG_DIM, (j + 1) * HG_DIM)
        o = of_scr[:, ls] + ob_scr[:, ls]
        gate = _silu(hgate_ref[0, :, ls].astype(F32))
        o_ref[0, :, ls] = (_rms(o) * ng_ref[...] * gate).astype(BF16)


def _hgrn_call(hg, lb_logits, norm_g, hps):
    B, S, _ = hg.shape
    W = hps * HG_DIM
    nh = HG_HEADS // hps
    G = HG_GROUP_ROWS
    cpg = G // HG_CHUNK
    seg = lambda sg: pl.BlockSpec((1, S, W), lambda b, j, sg=sg: (b, 0, sg * nh + j))
    return pl.pallas_call(
        functools.partial(_hgrn_kernel, hps=hps),
        grid=(B, nh),
        in_specs=[
            seg(0), seg(1), seg(2), seg(3), seg(4),
            pl.BlockSpec((4, W), lambda b, j: (0, j)),
            pl.BlockSpec((1, HG_DIM), lambda b, j: (0, 0)),
        ],
        out_specs=pl.BlockSpec((1, S, W), lambda b, j: (b, 0, j)),
        out_shape=jax.ShapeDtypeStruct((B, S, HG_WIDTH), BF16),
        scratch_shapes=[
            pltpu.VMEM((S, W), F32),
            pltpu.VMEM((S, W), F32),
            pltpu.VMEM((2 * hps, HG_DIM, HG_DIM), F32),
            pltpu.VMEM((2, 2, G, W), F32),
            pltpu.VMEM((2, 2, G, W), F32),
            pltpu.VMEM((2, 2, G, W), BF16),
            pltpu.VMEM((2, 2 * hps, G, G), BF16),
            pltpu.VMEM((2, 2 * hps * cpg, HG_DIM, HG_DIM), F32),
            pltpu.VMEM((2, 2 * hps, 8, HG_DIM), F32),
        ],
        compiler_params=_cparams(("arbitrary", "arbitrary")),
        name="hgrn",
    )(hg, hg, hg, hg, hg, lb_logits, norm_g)


def _router_kernel(attn_ref, rec_ref, x_ref, mod_ref, woa_ref, wor_ref, g2_ref, wr_ref, br_ref,
                   x1_ref, h2_ref, info_ref, wts_ref, lch_ref, lg_scr):
    @pl.when(pl.program_id(0) == 0)
    def _():
        lg_scr[...] = jnp.zeros_like(lg_scr)

    _route(lg_scr[...], info_ref, wts_ref, lch_ref)

    mod = mod_ref[0]
    tm = x_ref.shape[0]
    sub = min(ROUTER_SUB_ROWS, tm)
    for r0 in range(0, tm, sub):
        rs = slice(r0, r0 + sub)
        mixed = _dot_tn(attn_ref[0, 0, :, rs], woa_ref[...]) + _dot(rec_ref[rs, :], wor_ref[...])
        x1 = x_ref[rs, :] + mod[2:3] * mixed
        x1_ref[rs, :] = x1
        h2 = _rms(x1) * g2_ref[...]
        h2 = (h2 * (1.0 + mod[4:5]) + mod[3:4]).astype(BF16)
        h2_ref[rs, :] = h2
        lg_scr[rs, :] = _dot(h2, wr_ref[...]) + br_ref[...]


def _route(logits, info_ref, wts_ref, lch_ref):
    lt = logits.T[:ROUTE_USED]
    tm = lt.shape[1]
    r = lax.broadcasted_iota(I32, (ROUTE_USED, tm), 0)
    neg = -jnp.inf
    big = ROUTE_ROWS

    isg = r < N_GROUPS
    gmax = jnp.max(jnp.where(isg, lt, neg), axis=0, keepdims=True)
    gidx = jnp.min(jnp.where(isg & (lt == gmax), r, big), axis=0, keepdims=True)
    gp = 1.0 / jnp.sum(jnp.where(isg, jnp.exp(lt - gmax), 0.0), axis=0, keepdims=True)

    ing = (r >= N_GROUPS) & (r < N_GROUPS + N_EXPERTS) & (((r - N_GROUPS) >> 3) == gidx)
    v1 = jnp.max(jnp.where(ing, lt, neg), axis=0, keepdims=True)
    i1 = jnp.min(jnp.where(ing & (lt == v1), r, big), axis=0, keepdims=True)
    ing2 = ing & (r != i1)
    v2 = jnp.max(jnp.where(ing2, lt, neg), axis=0, keepdims=True)
    i2 = jnp.min(jnp.where(ing2 & (lt == v2), r, big), axis=0, keepdims=True)
    e21 = jnp.exp(v2 - v1)
    w1 = gp / (1.0 + e21)
    w2 = gp * e21 / (1.0 + e21)

    oh1 = r == i1
    oh2 = r == i2
    oh = jnp.where(oh1 | oh2, 1.0, 0.0)
    ohb = oh.astype(BF16)
    tt = lax.broadcasted_iota(I32, (tm, tm), 0)
    tc = lax.broadcasted_iota(I32, (tm, tm), 1)
    upper = jnp.where(tt < tc, 1.0, 0.0).astype(BF16)
    before = _dot(ohb, upper)
    cnt_col = jnp.sum(oh, axis=1, keepdims=True)
    pad_col = jnp.ceil(cnt_col * (1.0 / ROW_CHUNK)) * ROW_CHUNK
    er = lax.broadcasted_iota(I32, (ROUTE_USED, ROUTE_USED), 0)
    ec = lax.broadcasted_iota(I32, (ROUTE_USED, ROUTE_USED), 1)
    lower = jnp.where(ec < er, 1.0, 0.0).astype(BF16)
    start_col = _dot(lower, jnp.broadcast_to(pad_col, (ROUTE_USED, LANES)).astype(BF16))[:, 0:1]
    pos = before + start_col
    slot1 = jnp.sum(jnp.where(oh1, pos, 0.0), axis=0, keepdims=True)
    slot2 = jnp.sum(jnp.where(oh2, pos, 0.0), axis=0, keepdims=True)

    zero = jnp.zeros((1, tm), I32)
    info_ref[0] = jnp.concatenate(
        [slot1.astype(I32), slot2.astype(I32), i1 - N_GROUPS, i2 - N_GROUPS, zero, zero, zero, zero], axis=0)
    rows = [jnp.broadcast_to(v, (COL_REP, tm)) for v in (w1, w2, slot1, slot2)]
    cols = jnp.concatenate(rows, axis=0).T
    wts_ref[...] = cols
    ohb_all = jnp.concatenate([ohb, jnp.zeros((ROUTE_ROWS - ROUTE_USED, tm), BF16)], axis=0)
    cnt_row = _dot_nt(jnp.ones((8, tm), BF16), ohb_all)
    lch_ref[0] = jnp.ceil(cnt_row * (1.0 / ROW_CHUNK)).astype(I32)


def _router_call(attn_t, rec, x2, mod3, wo_a, wo_r, g2, wr, br, tm, tiles_per_batch):
    T, D = x2.shape
    nt = T // tm
    tiles_per_q = attn_t.shape[3] // tm
    full = lambda shape: pl.BlockSpec(shape, lambda i: (0,) * len(shape))
    cur = lambda i: jnp.minimum(i, nt - 1)
    prev = lambda i: jnp.maximum(i - 1, 0)
    return pl.pallas_call(
        _router_kernel,
        grid=(nt + 1,),
        in_specs=[
            pl.BlockSpec((1, 1, V_W, tm), lambda i: (cur(i) // tiles_per_batch,
                                                     (cur(i) % tiles_per_batch) // tiles_per_q,
                                                     0, (cur(i) % tiles_per_batch) % tiles_per_q)),
            pl.BlockSpec((tm, HG_WIDTH), lambda i: (cur(i), 0)),
            pl.BlockSpec((tm, D), lambda i: (cur(i), 0)),
            pl.BlockSpec((1, 6, D), lambda i: (cur(i) // tiles_per_batch, 0, 0)),
            full((V_W, D)),
            full((HG_WIDTH, D)),
            full((1, D)),
            full((D, ROUTE_ROWS)),
            full((1, ROUTE_ROWS)),
        ],
        out_specs=[
            pl.BlockSpec((tm, D), lambda i: (cur(i), 0)),
            pl.BlockSpec((tm, D), lambda i: (cur(i), 0)),
            pl.BlockSpec((1, 8, tm), lambda i: (prev(i), 0, 0)),
            pl.BlockSpec((tm, LANES), lambda i: (prev(i), 0)),
            pl.BlockSpec((1, 8, ROUTE_ROWS), lambda i: (prev(i), 0, 0)),
        ],
        out_shape=[
            jax.ShapeDtypeStruct((T, D), F32),
            jax.ShapeDtypeStruct((T, D), BF16),
            jax.ShapeDtypeStruct((nt, 8, tm), I32),
            jax.ShapeDtypeStruct((T, LANES), F32),
            jax.ShapeDtypeStruct((nt, 8, ROUTE_ROWS), I32),
        ],
        scratch_shapes=[pltpu.VMEM((tm, ROUTE_ROWS), F32)],
        compiler_params=_cparams(("arbitrary",)),
        name="router",
    )(attn_t, rec, x2, mod3, wo_a, wo_r, g2, wr, br)


def _local_rows(tm):
    need = TOP_K * tm + N_EXPERTS * (ROW_CHUNK - 1)
    return -(-need // MXU_DIM) * MXU_DIM


def _dispatch_kernel(dst_ref, info_ref, h2_ref, xb_ref, xs_scr, sem, *, n_real):
    i = pl.program_id(0)
    n = pl.num_programs(0)
    slot = i % 2
    LF = xs_scr.shape[1]
    L = LF // 2
    tm = h2_ref.shape[0]

    def whole(sl):
        return pltpu.make_async_copy(xs_scr.at[sl], xb_ref.at[pl.ds(0, LF)], sem.at[sl])

    @pl.when(i >= 2)
    def _():
        whole(slot).wait()

    @pl.when(i < n_real)
    def _():
        s = lax.broadcasted_iota(I32, (L, tm), 0)
        hit = (s == info_ref[0, 0:1, :]) | (s == info_ref[0, 1:2, :])
        xs_scr[slot] = _fold_rows(_dot(jnp.where(hit, 1.0, 0.0).astype(BF16), h2_ref[...]))

    @pl.when(i >= n_real)
    def _():
        xs_scr[slot] = jnp.zeros((LF, xs_scr.shape[2]), BF16)

    for c in range(LF // FOLD_CHUNK):
        d = pl.multiple_of(dst_ref[i, c] * FOLD_CHUNK, FOLD_CHUNK)
        pltpu.make_async_copy(xs_scr.at[slot, pl.ds(c * FOLD_CHUNK, FOLD_CHUNK)],
                              xb_ref.at[pl.ds(d, FOLD_CHUNK)], sem.at[slot]).start()

    @pl.when(i == n - 1)
    def _():
        whole(slot).wait()
        whole(1 - slot).wait()


def _dispatch_call(dst_table, info, h2, cap_rows, tm):
    T, D = h2.shape
    nt = T // tm
    n_steps = dst_table.shape[0]
    L = _local_rows(tm)
    last = nt - 1
    grid_spec = pltpu.PrefetchScalarGridSpec(
        num_scalar_prefetch=1,
        grid=(n_steps,),
        in_specs=[
            pl.BlockSpec((1, 8, tm), lambda i, dst: (jnp.minimum(i, last), 0, 0)),
            pl.BlockSpec((tm, D), lambda i, dst: (jnp.minimum(i, last), 0)),
        ],
        out_specs=pl.BlockSpec(memory_space=pl.ANY),
        scratch_shapes=[
            pltpu.VMEM((2, 2 * L, D // 2), BF16),
            pltpu.SemaphoreType.DMA((2,)),
        ],
    )
    return pl.pallas_call(
        functools.partial(_dispatch_kernel, n_real=nt),
        grid_spec=grid_spec,
        out_shape=jax.ShapeDtypeStruct((2 * cap_rows, D // 2), BF16),
        compiler_params=_cparams(("arbitrary",)),
        name="dispatch",
    )(dst_table, info, h2)


def _expert_kernel(te_ref, tr_ref, xb_ref, wg_ref, wu_ref, wd_ref, y_ref, wg_b, wu_b, wd_b):
    j = pl.program_id(0)
    prev = te_ref[jnp.maximum(j - 1, 0)]
    rows_used = tr_ref[j]

    @pl.when((j == 0) | (te_ref[j] != prev))
    def _():
        wg_b[...] = wg_ref[0].astype(BF16)
        wu_b[...] = wu_ref[0].astype(BF16)
        wd_b[...] = wd_ref[0].astype(BF16)

    @pl.when(rows_used > 0)
    def _():
        x = jnp.concatenate(_unfold_rows(xb_ref[...]), axis=1)
        hid = (_silu(_dot(x, wg_b[...])) * _dot(x, wu_b[...])).astype(BF16)
        y_ref[...] = _fold_rows(_dot(hid, wd_b[...]))

    @pl.when(rows_used == 0)
    def _():
        y_ref[...] = jnp.zeros_like(y_ref)


def _expert_call(tile_expert, tile_rows, xb, w_gate, w_up, w_down):
    cap, DP = xb.shape
    D = 2 * DP
    nt = tile_expert.shape[0]
    grid_spec = pltpu.PrefetchScalarGridSpec(
        num_scalar_prefetch=2,
        grid=(nt,),
        in_specs=[
            pl.BlockSpec((2 * EXPERT_TILE, DP), lambda j, te, tr: (jnp.where(tr[j] > 0, j, 0), 0)),
            pl.BlockSpec((1, D, D_EXPERT), lambda j, te, tr: (te[j], 0, 0)),
            pl.BlockSpec((1, D, D_EXPERT), lambda j, te, tr: (te[j], 0, 0)),
            pl.BlockSpec((1, D_EXPERT, D), lambda j, te, tr: (te[j], 0, 0)),
        ],
        out_specs=pl.BlockSpec((2 * EXPERT_TILE, DP), lambda j, te, tr: (j, 0)),
        scratch_shapes=[
            pltpu.VMEM((D, D_EXPERT), BF16),
            pltpu.VMEM((D, D_EXPERT), BF16),
            pltpu.VMEM((D_EXPERT, D), BF16),
        ],
    )
    return pl.pallas_call(
        _expert_kernel,
        grid_spec=grid_spec,
        out_shape=jax.ShapeDtypeStruct((cap, DP), BF16),
        compiler_params=_cparams(("arbitrary",)),
        name="expert",
    )(tile_expert, tile_rows, xb, w_gate, w_up, w_down)


def _combine_kernel(src_ref, x1_ref, mod_ref, wts_ref, yb_ref, o_ref, ys_scr, sem):
    i = pl.program_id(0)
    n = pl.num_programs(0)
    slot = i % 2
    LF = ys_scr.shape[1]
    L = LF // 2
    tm = x1_ref.shape[0]

    def fetch(step, sl):
        for c in range(LF // FOLD_CHUNK):
            d = pl.multiple_of(src_ref[step, c] * FOLD_CHUNK, FOLD_CHUNK)
            pltpu.make_async_copy(yb_ref.at[pl.ds(d, FOLD_CHUNK)],
                                  ys_scr.at[sl, pl.ds(c * FOLD_CHUNK, FOLD_CHUNK)], sem.at[sl]).start()

    @pl.when(i == 0)
    def _():
        fetch(0, 0)

    @pl.when(i + 1 < n)
    def _():
        fetch(i + 1, 1 - slot)

    pltpu.make_async_copy(yb_ref.at[pl.ds(0, LF)], ys_scr.at[slot], sem.at[slot]).wait()

    s = lax.broadcasted_iota(I32, (tm, L), 1).astype(F32)
    w1, w2, slot1, slot2 = (wts_ref[:, j * COL_REP:j * COL_REP + 1] for j in range(4))
    pick = (jnp.where(s == slot1, w1, 0.0) + jnp.where(s == slot2, w2, 0.0)).astype(BF16)
    y_left, y_right = _unfold_rows(ys_scr[slot])
    moe = jnp.concatenate([_dot(pick, y_left), _dot(pick, y_right)], axis=1)
    o_ref[...] = x1_ref[...] + mod_ref[0][5:6] * moe


def _combine_call(src_table, x1, mod3, wts, yb, tm, tiles_per_batch):
    T, D = x1.shape
    nt = T // tm
    L = _local_rows(tm)
    grid_spec = pltpu.PrefetchScalarGridSpec(
        num_scalar_prefetch=1,
        grid=(nt,),
        in_specs=[
            pl.BlockSpec((tm, D), lambda i, src: (i, 0)),
            pl.BlockSpec((1, 6, D), lambda i, src: (i // tiles_per_batch, 0, 0)),
            pl.BlockSpec((tm, LANES), lambda i, src: (i, 0)),
            pl.BlockSpec(memory_space=pl.ANY),
        ],
        out_specs=pl.BlockSpec((tm, D), lambda i, src: (i, 0)),
        scratch_shapes=[
            pltpu.VMEM((2, 2 * L, D // 2), BF16),
            pltpu.SemaphoreType.DMA((2,)),
        ],
    )
    return pl.pallas_call(
        _combine_kernel,
        grid_spec=grid_spec,
        out_shape=jax.ShapeDtypeStruct((T, D), F32),
        compiler_params=_cparams(("arbitrary",)),
        name="combine",
    )(src_table, x1, mod3, wts, yb)


def _rotate_half_matrix():
    half = MLA_ROPE // 2
    p = np.zeros((MLA_ROPE, MLA_ROPE), np.float32)
    for j in range(half):
        p[j + half, j] = -1.0
        p[j, j + half] = 1.0
    return jnp.asarray(p)


def _slot_gain(g, factor):
    half = MLA_ROPE // 2
    g_n, g_r = g[:MLA_NOPE], g[MLA_NOPE:]
    slot = jnp.concatenate([g_n, g_r, g_r[half:], g_r[:half]]) * factor
    return jnp.tile(slot, MLA_HEADS)[None, :].astype(F32)


def _prepare_weights(w_in, wq_up, wkv_up, qn_g, kn_g):
    rot = _rotate_half_matrix()
    s0 = MLA_Q_LORA
    s1 = s0 + MLA_KV_LORA
    s2 = s1 + MLA_ROPE
    w_kr = w_in[:, s1:s2]
    kr_slot = jnp.concatenate([jnp.zeros((D_MODEL, MLA_NOPE), F32), w_kr, w_kr @ rot], axis=1)
    win_p = jnp.concatenate([w_in[:, :s1], kr_slot, w_in[:, s2:]], axis=1).astype(BF16)

    wq = wq_up.reshape(MLA_Q_LORA, MLA_HEADS, MLA_QK)
    wq_r = wq[:, :, MLA_NOPE:]
    wq_ext = jnp.concatenate([wq, jnp.einsum('lhr,rs->lhs', wq_r, rot)], axis=2)
    wq_ext = wq_ext.reshape(MLA_Q_LORA, SLOT_W).astype(BF16)

    wkv = wkv_up.reshape(MLA_KV_LORA, MLA_HEADS, MLA_NOPE + MLA_V)
    wk_slot = jnp.concatenate([wkv[:, :, :MLA_NOPE], jnp.zeros((MLA_KV_LORA, MLA_HEADS, LANES - MLA_NOPE), F32)],
                              axis=2).reshape(MLA_KV_LORA, SLOT_W)
    wv = wkv[:, :, MLA_NOPE:].reshape(MLA_KV_LORA, V_W)
    wkv_ext = jnp.concatenate([wk_slot, wv], axis=1).astype(BF16)

    gq = _slot_gain(qn_g, MLA_QK ** -0.5 * math.log2(math.e))
    gk = _slot_gain(kn_g, 1.0)
    return win_p, wq_ext, wkv_ext, gq, gk


def _chunk_tables(lch, tm):
    nt = lch.shape[0]
    lchunks = _local_rows(tm) // ROW_CHUNK
    tch = EXPERT_TILE // ROW_CHUNK
    n_pad_steps = -(-(N_EXPERTS * (tch - 1)) // lchunks)
    while ((nt + n_pad_steps) * lchunks) % tch:
        n_pad_steps += 1
    n_steps = nt + n_pad_steps
    total = n_steps * lchunks
    n_tiles = total // tch

    nch = lch[:, 0, N_GROUPS:N_GROUPS + N_EXPERTS]
    lend = jnp.cumsum(nch, axis=1)
    lstart = lend - nch
    ltot = lend[:, -1]
    tot = jnp.sum(nch, axis=0)
    ptot = (tot + tch - 1) // tch * tch
    gend = jnp.cumsum(ptot)
    gstart = gend - ptot
    toff = jnp.cumsum(nch, axis=0) - nch
    c = jnp.arange(lchunks, dtype=I32)

    def lookup(idx, table):
        hit = idx[..., None] == jnp.arange(table.shape[-1], dtype=I32)
        return jnp.sum(jnp.where(hit, table, 0), axis=-1)

    e_of = jnp.minimum(jnp.sum(c[None, :, None] >= lend[:, None, :], axis=2), N_EXPERTS - 1)
    dest = lookup(e_of, (gstart[None, :] + toff - lstart)[:, None, :]) + c[None, :]
    valid = c[None, :] < ltot[:, None]
    src_table = jnp.where(valid, dest, 0).astype(I32)

    gap = jnp.concatenate([ptot - tot, total - gend[-1:]])
    gap_first = jnp.concatenate([gstart + tot, gend[-1:]])
    gap_cum = jnp.cumsum(gap)
    unused = lchunks - ltot
    rank = jnp.concatenate([
        (jnp.cumsum(unused) - unused)[:, None] + c[None, :] - ltot[:, None],
        jnp.sum(unused) + jnp.arange(n_pad_steps, dtype=I32)[:, None] * lchunks + c[None, :]])
    g_of = jnp.minimum(jnp.sum(rank[:, :, None] >= gap_cum[None, None, :], axis=2), N_EXPERTS)
    filler = lookup(g_of, gap_first - (gap_cum - gap)) + rank
    routed = jnp.concatenate([valid, jnp.zeros((n_pad_steps, lchunks), bool)])
    dst_table = jnp.where(routed, jnp.concatenate([dest, jnp.zeros((n_pad_steps, lchunks), I32)]), filler).astype(I32)

    ts = jnp.arange(n_tiles, dtype=I32) * tch
    tile_expert = jnp.minimum(jnp.sum(ts[:, None] >= gend[None, :], axis=1), N_EXPERTS - 1).astype(I32)
    seg_end = lookup(tile_expert, gstart + tot)
    tile_rows = (jnp.clip(seg_end - ts, 0, tch) * (ts < gend[-1]) * ROW_CHUNK).astype(I32)
    return dst_table, src_table, tile_expert, tile_rows, total * ROW_CHUNK


def _pick(n, pref):
    return pref if n % pref == 0 else n


def kernel(x, c, positions, ada_w, ada_b, norm1_g, w_in, mla_qa_g, mla_wq_up, mla_kva_g, mla_wkv_up, mla_qn_g, mla_kn_g, hg_lb_logits, hg_norm_g, w_out, norm2_g, router_group_w, router_group_b, router_expert_w, router_expert_b, w_gate, w_up, w_down):
    B, S, D = x.shape
    assert D == D_MODEL and ada_w.shape[0] == 1, "one layer of width 1024"
    T = B * S
    tm = _pick(S, 512)
    tiles_per_batch = S // tm

    mod3 = _mod_call(c, ada_w[0], ada_b).reshape(B, 6, D)

    win_p, wq_ext, wkv_ext, gq, gk = _prepare_weights(w_in[0], mla_wq_up[0], mla_wkv_up[0], mla_qn_g[0], mla_kn_g[0])
    q_hm, k_hm, vt, hg = _proj_call(x, mod3, norm1_g, win_p, mla_qa_g, wq_ext, mla_kva_g, wkv_ext, gq, gk,
                                    positions, tm)
    tq = _pick(S, ATTN_Q_TILE)
    attn_t = _attn_call(q_hm, k_hm, vt, tq)
    rec = _hgrn_call(hg, hg_lb_logits.reshape(2 * hg_lb_logits.shape[1], HG_WIDTH), hg_norm_g, hps=4)

    wo = w_out[0].astype(BF16)
    wr = jnp.concatenate([router_group_w[0], router_expert_w[0],
                          jnp.zeros((D, ROUTE_ROWS - N_GROUPS - N_EXPERTS), F32)], axis=1).astype(BF16)
    br = jnp.concatenate([router_group_b[0], router_expert_b[0],
                          jnp.zeros((ROUTE_ROWS - N_GROUPS - N_EXPERTS,), F32)])[None, :]
    x1, h2, info, wts, lch = _router_call(attn_t, rec.reshape(T, HG_WIDTH), x.reshape(T, D), mod3,
                                          wo[:V_W], wo[V_W:], norm2_g, wr, br, tm, tiles_per_batch)

    dst_table, src_table, tile_expert, tile_rows, cap_rows = _chunk_tables(lch, tm)
    xb = _dispatch_call(dst_table, info, h2, cap_rows, tm)
    yb = _expert_call(tile_expert, tile_rows, xb, w_gate[0], w_up[0], w_down[0])
    out = _combine_call(src_table, x1, mod3, wts, yb, tm, tiles_per_batch)
    return out.reshape(B, S, D)
```

```python
import functools
import math

import numpy as np
import jax
import jax.numpy as jnp
from jax import lax
from jax.experimental import pallas as pl
from jax.experimental.pallas import tpu as pltpu

F32 = jnp.float32
BF16 = jnp.bfloat16
I32 = jnp.int32

D_MODEL = 1024
MLA_HEADS = 8
MLA_NOPE = 64
MLA_ROPE = 32
MLA_QK = MLA_NOPE + MLA_ROPE
MLA_V = 64
MLA_Q_LORA = 384
MLA_KV_LORA = 256
ROPE_THETA = 10000.0
HG_HEADS = 4
HG_DIM = 128
HG_WIDTH = HG_HEADS * HG_DIM
HG_CHUNK = 64
N_GROUPS = 4
EPG = 8
N_EXPERTS = N_GROUPS * EPG
TOP_K = 2
D_EXPERT = 512
EPS = 1e-6

LANES = 128
VMEM_LIMIT = 56 * 1024 * 1024

SLOT_W = MLA_HEADS * LANES
V_W = MLA_HEADS * MLA_V
HG_SEGS = 5
W_IN_COLS = MLA_Q_LORA + MLA_KV_LORA + LANES + HG_SEGS * HG_WIDTH
HG_GROUP_ROWS = 4 * HG_CHUNK
ONES_ROWS = 16
VT_ROWS = MLA_V + ONES_ROWS
ATTN_KEY_BLOCK = 256
ATTN_Q_TILE = 1024
ROUTE_ROWS = 128
ROUTER_SUB_ROWS = 256
ROUTE_USED = 48
EXPERT_TILE = 1024
EXPERT_RING = 3
MXU_DIM = 256
ROW_CHUNK = 8
FOLD_CHUNK = 2 * ROW_CHUNK
COL_REP = LANES // 4


def _cparams(sem, vmem=VMEM_LIMIT):
    return pltpu.CompilerParams(dimension_semantics=sem, vmem_limit_bytes=vmem)


def _rms(x):
    return x * lax.rsqrt(jnp.mean(x * x, axis=-1, keepdims=True) + EPS)


def _silu(x):
    return x * jax.nn.sigmoid(x)


def _dot(a, b):
    return jnp.dot(a, b, preferred_element_type=F32)


def _dot_nt(a, b):
    return lax.dot_general(a, b, (((1,), (1,)), ((), ())), preferred_element_type=F32)


def _dot_tn(a, b):
    return lax.dot_general(a, b, (((0,), (0,)), ((), ())), preferred_element_type=F32)


def _fold_rows(v):
    r, c2 = v.shape
    c = c2 // 2
    left = v[:, :c].reshape(r // ROW_CHUNK, 1, ROW_CHUNK, c)
    right = v[:, c:].reshape(r // ROW_CHUNK, 1, ROW_CHUNK, c)
    return jnp.concatenate([left, right], axis=1).reshape(2 * r, c).astype(BF16)


def _unfold_rows(p):
    r2, c = p.shape
    r = r2 // 2
    q = p.astype(F32).reshape(r // ROW_CHUNK, 2, ROW_CHUNK, c)
    return q[:, 0].reshape(r, c).astype(BF16), q[:, 1].reshape(r, c).astype(BF16)


def _mod_kernel(c_ref, w_ref, b_ref, o_ref):
    a = _silu(c_ref[...]).astype(BF16)
    o_ref[...] = _dot(a, w_ref[...].astype(BF16)) + b_ref[...]


def _mod_call(c, ada_w, ada_b):
    B = c.shape[0]
    n = ada_w.shape[1] // D_MODEL
    return pl.pallas_call(
        _mod_kernel,
        grid=(n,),
        in_specs=[
            pl.BlockSpec((B, D_MODEL), lambda j: (0, 0)),
            pl.BlockSpec((D_MODEL, D_MODEL), lambda j: (0, j)),
            pl.BlockSpec((1, D_MODEL), lambda j: (0, j)),
        ],
        out_specs=pl.BlockSpec((B, D_MODEL), lambda j: (0, j)),
        out_shape=jax.ShapeDtypeStruct((B, n * D_MODEL), F32),
        compiler_params=_cparams(("arbitrary",)),
        name="mod",
    )(c, ada_w, ada_b)


def _rope_tables(pos_row):
    half = MLA_ROPE // 2
    tm = pos_row.shape[1]
    idx = lax.broadcasted_iota(I32, (half, tm), 0).astype(F32)
    inv_freq = ROPE_THETA ** (-idx / half)
    ang = pos_row.astype(F32) * inv_freq
    c = jnp.cos(ang)
    s = jnp.sin(ang)
    ct = jnp.concatenate([jnp.ones((MLA_NOPE, tm), F32), c, c, jnp.zeros((MLA_ROPE, tm), F32)], axis=0)
    st = jnp.concatenate([jnp.zeros((MLA_QK, tm), F32), s, s], axis=0)
    return ct.T, st.T


def _head_finish(t, g_row, ct, st, lane_valid):
    ss = jnp.sum(jnp.where(lane_valid, t * t, 0.0), axis=-1, keepdims=True) * (1.0 / MLA_QK)
    tg = t * lax.rsqrt(ss + EPS) * g_row
    return tg * ct + pltpu.roll(tg * st, LANES - MLA_ROPE, 1)


def _proj_kernel(x_ref, mod_ref, g1_ref, win_ref, qag_ref, wq_ref, kvag_ref, wkv_ref, gq_ref, gk_ref,
                 pos_ref, q_out, k_out, vt_out, hg_out):
    x = x_ref[0]
    mod = mod_ref[0]
    h = _rms(x) * g1_ref[...]
    h = h * (1.0 + mod[1:2]) + mod[0:1]
    hb = h.astype(BF16)

    ct, st = _rope_tables(pos_ref[0])
    tm = x.shape[0]
    lane_valid = lax.broadcasted_iota(I32, (tm, LANES), 1) < MLA_QK

    c0 = MLA_Q_LORA
    c1 = c0 + MLA_KV_LORA
    c2 = c1 + LANES

    q_lat = _dot(hb, win_ref[:, 0:c0])
    qn = (_rms(q_lat) * qag_ref[...]).astype(BF16)
    qe = _dot(qn, wq_ref[...])
    for hd in range(MLA_HEADS):
        sl = slice(hd * LANES, (hd + 1) * LANES)
        q_out[0, hd] = _head_finish(qe[:, sl], gq_ref[:, sl], ct, st, lane_valid).astype(BF16)

    kv_lat = _dot(hb, win_ref[:, c0:c1])
    kvn = (_rms(kv_lat) * kvag_ref[...]).astype(BF16)
    ke = _dot(kvn, wkv_ref[:, 0:SLOT_W])
    kr = _dot(hb, win_ref[:, c1:c2])
    for hd in range(MLA_HEADS):
        sl = slice(hd * LANES, (hd + 1) * LANES)
        k_out[0, hd] = _head_finish(ke[:, sl] + kr, gk_ref[:, sl], ct, st, lane_valid).astype(BF16)

    v_t = _dot(kvn, wkv_ref[:, SLOT_W:SLOT_W + V_W]).T.astype(BF16)
    ones = jnp.ones((ONES_ROWS, tm), BF16)
    for hd in range(MLA_HEADS):
        vt_out[0, hd, 0:MLA_V, :] = v_t[hd * MLA_V:(hd + 1) * MLA_V, :]
        vt_out[0, hd, MLA_V:VT_ROWS, :] = ones

    for sg in range(HG_SEGS):
        lo = c2 + sg * HG_WIDTH
        hg_out[0, :, sg * HG_WIDTH:(sg + 1) * HG_WIDTH] = _dot(hb, win_ref[:, lo:lo + HG_WIDTH]).astype(BF16)


def _proj_call(x, mod3, g1, win_p, qag, wq_ext, kvag, wkv_ext, gq, gk, positions, tm):
    B, S, D = x.shape
    nt = S // tm
    pos3 = positions.reshape(B, 1, S)
    full = lambda shape: pl.BlockSpec(shape, lambda b, i: (0,) * len(shape))
    return pl.pallas_call(
        _proj_kernel,
        grid=(B, nt),
        in_specs=[
            pl.BlockSpec((1, tm, D), lambda b, i: (b, i, 0)),
            pl.BlockSpec((1, 6, D), lambda b, i: (b, 0, 0)),
            full((1, D)),
            full((D, W_IN_COLS)),
            full((1, MLA_Q_LORA)),
            full((MLA_Q_LORA, SLOT_W)),
            full((1, MLA_KV_LORA)),
            full((MLA_KV_LORA, SLOT_W + V_W)),
            full((1, SLOT_W)),
            full((1, SLOT_W)),
            pl.BlockSpec((1, 1, tm), lambda b, i: (b, 0, i)),
        ],
        out_specs=[
            pl.BlockSpec((1, MLA_HEADS, tm, LANES), lambda b, i: (b, 0, i, 0)),
            pl.BlockSpec((1, MLA_HEADS, tm, LANES), lambda b, i: (b, 0, i, 0)),
            pl.BlockSpec((1, MLA_HEADS, VT_ROWS, tm), lambda b, i: (b, 0, 0, i)),
            pl.BlockSpec((1, tm, HG_SEGS * HG_WIDTH), lambda b, i: (b, i, 0)),
        ],
        out_shape=[
            jax.ShapeDtypeStruct((B, MLA_HEADS, S, LANES), BF16),
            jax.ShapeDtypeStruct((B, MLA_HEADS, S, LANES), BF16),
            jax.ShapeDtypeStruct((B, MLA_HEADS, VT_ROWS, S), BF16),
            jax.ShapeDtypeStruct((B, S, HG_SEGS * HG_WIDTH), BF16),
        ],
        compiler_params=_cparams(("arbitrary", "arbitrary")),
        name="proj",
    )(x, mod3, g1, win_p, qag, wq_ext, kvag, wkv_ext, gq, gk, pos3)


def _attn_kernel(q_ref, k_ref, vt_ref, o_ref, s_scr):
    S = k_ref.shape[2]
    tq = o_ref.shape[3]
    n_items = o_ref.shape[1] * MLA_HEADS
    kb_rows = min(ATTN_KEY_BLOCK, S)
    blocks = [slice(b * kb_rows, (b + 1) * kb_rows) for b in range(S // kb_rows)]

    def split(i):
        return i // MLA_HEADS, i % MLA_HEADS

    def scores(i, rows):
        qt, hd = split(i)
        q = q_ref[0, hd, pl.ds(pl.multiple_of(qt * tq, tq), tq), :]
        s_t = _dot_nt(k_ref[0, hd, rows, :], q)
        s_scr[rows, :] = s_t
        return jnp.max(s_t, axis=0, keepdims=True)

    def step(i, m, with_next):
        qt, hd = split(i)
        acc = jnp.zeros((VT_ROWS, tq), F32)
        m_next = None
        for rows in blocks:
            p = jnp.exp2(s_scr[rows, :] - m).astype(BF16)
            acc = acc + _dot(vt_ref[0, hd, :, rows], p)
            if with_next:
                bm = scores(i + 1, rows)
                m_next = bm if m_next is None else jnp.maximum(m_next, bm)
        o = acc[:MLA_V] / acc[MLA_V:MLA_V + 1]
        o_ref[0, qt, pl.ds(pl.multiple_of(hd * MLA_V, MLA_V), MLA_V), :] = o.astype(BF16)
        return m_next

    m0 = None
    for rows in blocks:
        bm = scores(0, rows)
        m0 = bm if m0 is None else jnp.maximum(m0, bm)
    m_last = lax.fori_loop(0, n_items - 1, lambda i, m: step(i, m, True), m0)
    step(n_items - 1, m_last, False)


def _attn_call(q_hm, k_hm, vt, tq):
    B, _, S, _ = q_hm.shape
    nq = S // tq
    return pl.pallas_call(
        _attn_kernel,
        grid=(B,),
        in_specs=[
            pl.BlockSpec((1, MLA_HEADS, S, LANES), lambda b: (b, 0, 0, 0)),
            pl.BlockSpec((1, MLA_HEADS, S, LANES), lambda b: (b, 0, 0, 0)),
            pl.BlockSpec((1, MLA_HEADS, VT_ROWS, S), lambda b: (b, 0, 0, 0)),
        ],
        out_specs=pl.BlockSpec((1, nq, V_W, tq), lambda b: (b, 0, 0, 0)),
        out_shape=jax.ShapeDtypeStruct((B, nq, V_W, tq), BF16),
        scratch_shapes=[pltpu.VMEM((S, tq), F32)],
        compiler_params=_cparams(("arbitrary",)),
        name="attn",
    )(q_hm, k_hm, vt)


def _hgrn_kernel(hq_ref, ff_ref, fb_ref, hi_ref, hgate_ref, lbl_ref, ng_ref, o_ref, of_scr, ob_scr, st_scr,
                 bc_scr, k_scr, qd_scr, att_scr, kv_scr, el_scr, *, hps):
    S = hq_ref.shape[1]
    C = HG_CHUNK
    G = HG_GROUP_ROWS
    cpg = G // C
    n_groups = S // G

    def lower_bound(l0, l1):
        m = jnp.maximum(l0, l1)
        e0 = jnp.exp(l0 - m)
        return e0 / (e0 + jnp.exp(l1 - m))

    lb_f = lower_bound(lbl_ref[0:1, :], lbl_ref[1:2, :])
    lb_b = lower_bound(lbl_ref[2:3, :], lbl_ref[3:4, :])

    rr = lax.broadcasted_iota(I32, (G, G), 0)
    cc = lax.broadcasted_iota(I32, (G, G), 1)
    same_chunk = (rr // C) == (cc // C)
    mask_f = same_chunk & (cc <= rr)
    mask_b = same_chunk & (cc >= rr)
    tri_f = jnp.where(mask_f, 1.0, 0.0).astype(BF16)
    tri_b = jnp.where(mask_b, 1.0, 0.0).astype(BF16)

    st_scr[...] = jnp.zeros_like(st_scr)
    W = hps * HG_DIM
    dirs = ((0, lb_f, ff_ref, of_scr, mask_f, tri_f), (1, lb_b, fb_ref, ob_scr, mask_b, tri_b))

    def rows_of(d, g):
        grp = g if d == 0 else n_groups - 1 - g
        return pl.ds(grp * G if isinstance(grp, int) else pl.multiple_of(grp * G, G), G)

    def stage_gates(g, slot):
        for d, lb, f_ref, _, _, tri in dirs:
            f = lb + (1.0 - lb) * jax.nn.sigmoid(f_ref[0, rows_of(d, g), :].astype(F32))
            lf = jnp.log(f)
            hi = lf.astype(BF16)
            lo = (lf - hi.astype(F32)).astype(BF16)
            both = _dot(tri, jnp.concatenate([hi, lo], axis=1))
            bc_scr[slot, d] = both[:, :W] + both[:, W:]
            k_scr[slot, d] = 1.0 - f

    def stage_decay(g, slot):
        for d, _, _, _, mask, _ in dirs:
            rows = rows_of(d, g)
            bc = bc_scr[slot, d]
            eb = jnp.exp(bc)
            qd = (_silu(hq_ref[0, rows, :].astype(F32)) * eb).astype(BF16)
            kd = k_scr[slot, d] * jnp.exp(-bc)
            kdb = kd.astype(BF16)
            qd_scr[slot, d] = qd
            v = hi_ref[0, rows, :]
            for j in range(hps):
                ls = slice(j * HG_DIM, (j + 1) * HG_DIM)
                ch = d * hps + j
                att_scr[slot, ch] = jnp.where(mask, _dot_nt(qd[:, ls], kdb[:, ls]), 0.0).astype(BF16)
                lasts = []
                for c in range(cpg):
                    cs = slice(c * C, (c + 1) * C)
                    last = (c + 1) * C - 1 if d == 0 else c * C
                    e_last = eb[last:last + 1, ls]
                    lasts.append(e_last)
                    k_rem = (kd[cs, ls] * e_last).astype(BF16)
                    kv_scr[slot, ch * cpg + c] = _dot_tn(v[cs, ls], k_rem)
                el_scr[slot, ch] = jnp.concatenate(lasts + [jnp.zeros((8 - cpg, HG_DIM), F32)], axis=0)

    def stage_state(g, slot):
        for d, _, _, o_scr, _, _ in dirs:
            rows = rows_of(d, g)
            qd = qd_scr[slot, d]
            v = hi_ref[0, rows, :]
            for j in range(hps):
                ls = slice(j * HG_DIM, (j + 1) * HG_DIM)
                ch = d * hps + j
                o_intra = _dot(att_scr[slot, ch], v[:, ls])
                el = el_scr[slot, ch]
                st_t = st_scr[ch]
                o_inter = [None] * cpg
                for c in (range(cpg) if d == 0 else reversed(range(cpg))):
                    o_inter[c] = _dot_nt(qd[c * C:(c + 1) * C, ls], st_t.astype(BF16))
                    st_t = st_t * el[c:c + 1, :] + kv_scr[slot, ch * cpg + c]
                st_scr[ch] = st_t
                o_scr[rows, ls] = o_intra + jnp.concatenate(o_inter, axis=0)

    def step(t, parity):
        if not isinstance(t, int) or t < n_groups:
            stage_gates(t, parity)
        if not isinstance(t, int) or 1 <= t <= n_groups:
            stage_decay(t - 1, 1 - parity)
        if not isinstance(t, int) or t >= 2:
            stage_state(t - 2, parity)

    n_steps = n_groups + 2
    head = min(2, n_steps)
    tail_start = max(head, n_groups)
    for t in range(head):
        step(t, t % 2)
    n_mid = tail_start - head
    if n_mid % 2 == 0 and n_mid > 4:
        def body(i, carry):
            t = 2 + 2 * i
            step(t, 0)
            step(t + 1, 1)
            return carry
        lax.fori_loop(0, n_mid // 2, body, 0)
    else:
        for t in range(head, tail_start):
            step(t, t % 2)
    for t in range(tail_start, n_steps):
        step(t, t % 2)

    for j in range(hps):
        ls = slice(j * HG_DIM, (j + 1) * HG_DIM)
        o = of_scr[:, ls] + ob_scr[:, ls]
        gate = _silu(hgate_ref[0, :, ls].astype(F32))
        o_ref[0, :, ls] = (_rms(o) * ng_ref[...] * gate).astype(BF16)


def _hgrn_call(hg, lb_logits, norm_g, hps):
    B, S, _ = hg.shape
    W = hps * HG_DIM
    nh = HG_HEADS // hps
    G = HG_GROUP_ROWS
    cpg = G // HG_CHUNK
    seg = lambda sg: pl.BlockSpec((1, S, W), lambda b, j, sg=sg: (b, 0, sg * nh + j))
    return pl.pallas_call(
        functools.partial(_hgrn_kernel, hps=hps),
        grid=(B, nh),
        in_specs=[
            seg(0), seg(1), seg(2), seg(3), seg(4),
            pl.BlockSpec((4, W), lambda b, j: (0, j)),
            pl.BlockSpec((1, HG_DIM), lambda b, j: (0, 0)),
        ],
        out_specs=pl.BlockSpec((1, S, W), lambda b, j: (b, 0, j)),
        out_shape=jax.ShapeDtypeStruct((B, S, HG_WIDTH), BF16),
        scratch_shapes=[
            pltpu.VMEM((S, W), F32),
            pltpu.VMEM((S, W), F32),
            pltpu.VMEM((2 * hps, HG_DIM, HG_DIM), F32),
            pltpu.VMEM((2, 2, G, W), F32),
            pltpu.VMEM((2, 2, G, W), F32),
            pltpu.VMEM((2, 2, G, W), BF16),
            pltpu.VMEM((2, 2 * hps, G, G), BF16),
            pltpu.VMEM((2, 2 * hps * cpg, HG_DIM, HG_DIM), F32),
            pltpu.VMEM((2, 2 * hps, 8, HG_DIM), F32),
        ],
        compiler_params=_cparams(("arbitrary", "arbitrary")),
        name="hgrn",
    )(hg, hg, hg, hg, hg, lb_logits, norm_g)


def _router_kernel(attn_ref, rec_ref, x_ref, mod_ref, woa_ref, wor_ref, g2_ref, wr_ref, br_ref,
                   x1_ref, h2_ref, info_ref, wts_ref, lch_ref, lg_scr):
    @pl.when(pl.program_id(0) == 0)
    def _():
        lg_scr[...] = jnp.zeros_like(lg_scr)

    _route(lg_scr[...], info_ref, wts_ref, lch_ref)

    mod = mod_ref[0]
    tm = x_ref.shape[0]
    sub = min(ROUTER_SUB_ROWS, tm)
    for r0 in range(0, tm, sub):
        rs = slice(r0, r0 + sub)
        mixed = _dot_tn(attn_ref[0, 0, :, rs], woa_ref[...]) + _dot(rec_ref[rs, :], wor_ref[...])
        x1 = x_ref[rs, :] + mod[2:3] * mixed
        x1_ref[rs, :] = x1
        h2 = _rms(x1) * g2_ref[...]
        h2 = (h2 * (1.0 + mod[4:5]) + mod[3:4]).astype(BF16)
        h2_ref[rs, :] = h2
        lg_scr[rs, :] = _dot(h2, wr_ref[...]) + br_ref[...]


def _route(logits, info_ref, wts_ref, lch_ref):
    lt = logits.T[:ROUTE_USED]
    tm = lt.shape[1]
    r = lax.broadcasted_iota(I32, (ROUTE_USED, tm), 0)
    neg = -jnp.inf
    big = ROUTE_ROWS

    isg = r < N_GROUPS
    gmax = jnp.max(jnp.where(isg, lt, neg), axis=0, keepdims=True)
    gidx = jnp.min(jnp.where(isg & (lt == gmax), r, big), axis=0, keepdims=True)
    gp = 1.0 / jnp.sum(jnp.where(isg, jnp.exp(lt - gmax), 0.0), axis=0, keepdims=True)

    ing = (r >= N_GROUPS) & (r < N_GROUPS + N_EXPERTS) & (((r - N_GROUPS) >> 3) == gidx)
    v1 = jnp.max(jnp.where(ing, lt, neg), axis=0, keepdims=True)
    i1 = jnp.min(jnp.where(ing & (lt == v1), r, big), axis=0, keepdims=True)
    ing2 = ing & (r != i1)
    v2 = jnp.max(jnp.where(ing2, lt, neg), axis=0, keepdims=True)
    i2 = jnp.min(jnp.where(ing2 & (lt == v2), r, big), axis=0, keepdims=True)
    e21 = jnp.exp(v2 - v1)
    w1 = gp / (1.0 + e21)
    w2 = gp * e21 / (1.0 + e21)

    oh1 = r == i1
    oh2 = r == i2
    oh = jnp.where(oh1 | oh2, 1.0, 0.0)
    ohb = oh.astype(BF16)
    tt = lax.broadcasted_iota(I32, (tm, tm), 0)
    tc = lax.broadcasted_iota(I32, (tm, tm), 1)
    upper = jnp.where(tt < tc, 1.0, 0.0).astype(BF16)
    before = _dot(ohb, upper)
    cnt_col = jnp.sum(oh, axis=1, keepdims=True)
    pad_col = jnp.ceil(cnt_col * (1.0 / ROW_CHUNK)) * ROW_CHUNK
    er = lax.broadcasted_iota(I32, (ROUTE_USED, ROUTE_USED), 0)
    ec = lax.broadcasted_iota(I32, (ROUTE_USED, ROUTE_USED), 1)
    lower = jnp.where(ec < er, 1.0, 0.0).astype(BF16)
    start_col = _dot(lower, jnp.broadcast_to(pad_col, (ROUTE_USED, LANES)).astype(BF16))[:, 0:1]
    pos = before + start_col
    slot1 = jnp.sum(jnp.where(oh1, pos, 0.0), axis=0, keepdims=True)
    slot2 = jnp.sum(jnp.where(oh2, pos, 0.0), axis=0, keepdims=True)

    zero = jnp.zeros((1, tm), I32)
    info_ref[0] = jnp.concatenate(
        [slot1.astype(I32), slot2.astype(I32), i1 - N_GROUPS, i2 - N_GROUPS, zero, zero, zero, zero], axis=0)
    rows = [jnp.broadcast_to(v, (COL_REP, tm)) for v in (w1, w2, slot1, slot2)]
    cols = jnp.concatenate(rows, axis=0).T
    wts_ref[...] = cols
    ohb_all = jnp.concatenate([ohb, jnp.zeros((ROUTE_ROWS - ROUTE_USED, tm), BF16)], axis=0)
    cnt_row = _dot_nt(jnp.ones((8, tm), BF16), ohb_all)
    lch_ref[0] = jnp.ceil(cnt_row * (1.0 / ROW_CHUNK)).astype(I32)


def _router_call(attn_t, rec, x2, mod3, wo_a, wo_r, g2, wr, br, tm, tiles_per_batch):
    T, D = x2.shape
    nt = T // tm
    tiles_per_q = attn_t.shape[3] // tm
    full = lambda shape: pl.BlockSpec(shape, lambda i: (0,) * len(shape))
    cur = lambda i: jnp.minimum(i, nt - 1)
    prev = lambda i: jnp.maximum(i - 1, 0)
    return pl.pallas_call(
        _router_kernel,
        grid=(nt + 1,),
        in_specs=[
            pl.BlockSpec((1, 1, V_W, tm), lambda i: (cur(i) // tiles_per_batch,
                                                     (cur(i) % tiles_per_batch) // tiles_per_q,
                                                     0, (cur(i) % tiles_per_batch) % tiles_per_q)),
            pl.BlockSpec((tm, HG_WIDTH), lambda i: (cur(i), 0)),
            pl.BlockSpec((tm, D), lambda i: (cur(i), 0)),
            pl.BlockSpec((1, 6, D), lambda i: (cur(i) // tiles_per_batch, 0, 0)),
            full((V_W, D)),
            full((HG_WIDTH, D)),
            full((1, D)),
            full((D, ROUTE_ROWS)),
            full((1, ROUTE_ROWS)),
        ],
        out_specs=[
            pl.BlockSpec((tm, D), lambda i: (cur(i), 0)),
            pl.BlockSpec((tm, D), lambda i: (cur(i), 0)),
            pl.BlockSpec((1, 8, tm), lambda i: (prev(i), 0, 0)),
            pl.BlockSpec((tm, LANES), lambda i: (prev(i), 0)),
            pl.BlockSpec((1, 8, ROUTE_ROWS), lambda i: (prev(i), 0, 0)),
        ],
        out_shape=[
            jax.ShapeDtypeStruct((T, D), F32),
            jax.ShapeDtypeStruct((T, D), BF16),
            jax.ShapeDtypeStruct((nt, 8, tm), I32),
            jax.ShapeDtypeStruct((T, LANES), F32),
            jax.ShapeDtypeStruct((nt, 8, ROUTE_ROWS), I32),
        ],
        scratch_shapes=[pltpu.VMEM((tm, ROUTE_ROWS), F32)],
        compiler_params=_cparams(("arbitrary",)),
        name="router",
    )(attn_t, rec, x2, mod3, wo_a, wo_r, g2, wr, br)


def _local_rows(tm):
    need = TOP_K * tm + N_EXPERTS * (ROW_CHUNK - 1)
    return -(-need // MXU_DIM) * MXU_DIM


def _dispatch_kernel(dst_ref, info_ref, h2_ref, xb_ref, xs_scr, sem, *, n_real):
    i = pl.program_id(0)
    n = pl.num_programs(0)
    slot = i % 2
    LF = xs_scr.shape[1]
    L = LF // 2
    tm = h2_ref.shape[0]

    def whole(sl):
        return pltpu.make_async_copy(xs_scr.at[sl], xb_ref.at[pl.ds(0, LF)], sem.at[sl])

    @pl.when(i >= 2)
    def _():
        whole(slot).wait()

    @pl.when(i < n_real)
    def _():
        s = lax.broadcasted_iota(I32, (L, tm), 0)
        hit = (s == info_ref[0, 0:1, :]) | (s == info_ref[0, 1:2, :])
        xs_scr[slot] = _fold_rows(_dot(jnp.where(hit, 1.0, 0.0).astype(BF16), h2_ref[...]))

    @pl.when(i >= n_real)
    def _():
        xs_scr[slot] = jnp.zeros((LF, xs_scr.shape[2]), BF16)

    for c in range(LF // FOLD_CHUNK):
        d = pl.multiple_of(dst_ref[i, c] * FOLD_CHUNK, FOLD_CHUNK)
        pltpu.make_async_copy(xs_scr.at[slot, pl.ds(c * FOLD_CHUNK, FOLD_CHUNK)],
                              xb_ref.at[pl.ds(d, FOLD_CHUNK)], sem.at[slot]).start()

    @pl.when(i == n - 1)
    def _():
        whole(slot).wait()
        whole(1 - slot).wait()


def _dispatch_call(dst_table, info, h2, cap_rows, tm):
    T, D = h2.shape
    nt = T // tm
    n_steps = dst_table.shape[0]
    L = _local_rows(tm)
    last = nt - 1
    grid_spec = pltpu.PrefetchScalarGridSpec(
        num_scalar_prefetch=1,
        grid=(n_steps,),
        in_specs=[
            pl.BlockSpec((1, 8, tm), lambda i, dst: (jnp.minimum(i, last), 0, 0)),
            pl.BlockSpec((tm, D), lambda i, dst: (jnp.minimum(i, last), 0)),
        ],
        out_specs=pl.BlockSpec(memory_space=pl.ANY),
        scratch_shapes=[
            pltpu.VMEM((2, 2 * L, D // 2), BF16),
            pltpu.SemaphoreType.DMA((2,)),
        ],
    )
    return pl.pallas_call(
        functools.partial(_dispatch_kernel, n_real=nt),
        grid_spec=grid_spec,
        out_shape=jax.ShapeDtypeStruct((2 * cap_rows, D // 2), BF16),
        compiler_params=_cparams(("arbitrary",)),
        name="dispatch",
    )(dst_table, info, h2)


def _expert_kernel(te_ref, tr_ref, xb_ref, wg_ref, wu_ref, wd_ref, y_ref, wg_b, wu_b, wd_b, x_ring, x_sem):
    j = pl.program_id(0)
    n = pl.num_programs(0)
    prev = te_ref[jnp.maximum(j - 1, 0)]
    rows_used = tr_ref[j]
    rows = x_ring.shape[1]

    def tile_copy(t):
        sl = t % EXPERT_RING
        return pltpu.make_async_copy(xb_ref.at[pl.ds(pl.multiple_of(t * rows, rows), rows)], x_ring.at[sl],
                                     x_sem.at[sl])

    def fetch(t):
        @pl.when((t < n) & (tr_ref[jnp.minimum(t, n - 1)] > 0))
        def _():
            tile_copy(t).start()

    @pl.when(j == 0)
    def _():
        for t in range(EXPERT_RING - 1):
            fetch(jnp.int32(t))

    fetch(j + EXPERT_RING - 1)

    @pl.when((j == 0) | (te_ref[j] != prev))
    def _():
        wg_b[...] = wg_ref[0].astype(BF16)
        wu_b[...] = wu_ref[0].astype(BF16)
        wd_b[...] = wd_ref[0].astype(BF16)

    @pl.when(rows_used > 0)
    def _():
        tile_copy(j).wait()
        x = jnp.concatenate(_unfold_rows(x_ring[j % EXPERT_RING]), axis=1)
        hid = (_silu(_dot(x, wg_b[...])) * _dot(x, wu_b[...])).astype(BF16)
        y_ref[...] = _fold_rows(_dot(hid, wd_b[...]))

    @pl.when(rows_used == 0)
    def _():
        y_ref[...] = jnp.zeros_like(y_ref)


def _expert_call(tile_expert, tile_rows, xb, w_gate, w_up, w_down):
    cap, DP = xb.shape
    D = 2 * DP
    nt = tile_expert.shape[0]
    grid_spec = pltpu.PrefetchScalarGridSpec(
        num_scalar_prefetch=2,
        grid=(nt,),
        in_specs=[
            pl.BlockSpec(memory_space=pl.ANY),
            pl.BlockSpec((1, D, D_EXPERT), lambda j, te, tr: (te[j], 0, 0)),
            pl.BlockSpec((1, D, D_EXPERT), lambda j, te, tr: (te[j], 0, 0)),
            pl.BlockSpec((1, D_EXPERT, D), lambda j, te, tr: (te[j], 0, 0)),
        ],
        out_specs=pl.BlockSpec((2 * EXPERT_TILE, DP), lambda j, te, tr: (j, 0)),
        scratch_shapes=[
            pltpu.VMEM((D, D_EXPERT), BF16),
            pltpu.VMEM((D, D_EXPERT), BF16),
            pltpu.VMEM((D_EXPERT, D), BF16),
            pltpu.VMEM((EXPERT_RING, 2 * EXPERT_TILE, DP), BF16),
            pltpu.SemaphoreType.DMA((EXPERT_RING,)),
        ],
    )
    return pl.pallas_call(
        _expert_kernel,
        grid_spec=grid_spec,
        out_shape=jax.ShapeDtypeStruct((cap, DP), BF16),
        compiler_params=_cparams(("arbitrary",)),
        name="expert",
    )(tile_expert, tile_rows, xb, w_gate, w_up, w_down)


def _combine_kernel(src_ref, x1_ref, mod_ref, wts_ref, yb_ref, o_ref, ys_scr, sem):
    i = pl.program_id(0)
    n = pl.num_programs(0)
    slot = i % 2
    LF = ys_scr.shape[1]
    L = LF // 2
    tm = x1_ref.shape[0]

    def fetch(step, sl):
        for c in range(LF // FOLD_CHUNK):
            d = pl.multiple_of(src_ref[step, c] * FOLD_CHUNK, FOLD_CHUNK)
            pltpu.make_async_copy(yb_ref.at[pl.ds(d, FOLD_CHUNK)],
                                  ys_scr.at[sl, pl.ds(c * FOLD_CHUNK, FOLD_CHUNK)], sem.at[sl]).start()

    @pl.when(i == 0)
    def _():
        fetch(0, 0)

    @pl.when(i + 1 < n)
    def _():
        fetch(i + 1, 1 - slot)

    pltpu.make_async_copy(yb_ref.at[pl.ds(0, LF)], ys_scr.at[slot], sem.at[slot]).wait()

    s = lax.broadcasted_iota(I32, (tm, L), 1).astype(F32)
    w1, w2, slot1, slot2 = (wts_ref[:, j * COL_REP:j * COL_REP + 1] for j in range(4))
    pick = (jnp.where(s == slot1, w1, 0.0) + jnp.where(s == slot2, w2, 0.0)).astype(BF16)
    y_left, y_right = _unfold_rows(ys_scr[slot])
    moe = jnp.concatenate([_dot(pick, y_left), _dot(pick, y_right)], axis=1)
    o_ref[...] = x1_ref[...] + mod_ref[0][5:6] * moe


def _combine_call(src_table, x1, mod3, wts, yb, tm, tiles_per_batch):
    T, D = x1.shape
    nt = T // tm
    L = _local_rows(tm)
    grid_spec = pltpu.PrefetchScalarGridSpec(
        num_scalar_prefetch=1,
        grid=(nt,),
        in_specs=[
            pl.BlockSpec((tm, D), lambda i, src: (i, 0)),
            pl.BlockSpec((1, 6, D), lambda i, src: (i // tiles_per_batch, 0, 0)),
            pl.BlockSpec((tm, LANES), lambda i, src: (i, 0)),
            pl.BlockSpec(memory_space=pl.ANY),
        ],
        out_specs=pl.BlockSpec((tm, D), lambda i, src: (i, 0)),
        scratch_shapes=[
            pltpu.VMEM((2, 2 * L, D // 2), BF16),
            pltpu.SemaphoreType.DMA((2,)),
        ],
    )
    return pl.pallas_call(
        _combine_kernel,
        grid_spec=grid_spec,
        out_shape=jax.ShapeDtypeStruct((T, D), F32),
        compiler_params=_cparams(("arbitrary",)),
        name="combine",
    )(src_table, x1, mod3, wts, yb)


def _rotate_half_matrix():
    half = MLA_ROPE // 2
    p = np.zeros((MLA_ROPE, MLA_ROPE), np.float32)
    for j in range(half):
        p[j + half, j] = -1.0
        p[j, j + half] = 1.0
    return jnp.asarray(p)


def _slot_gain(g, factor):
    half = MLA_ROPE // 2
    g_n, g_r = g[:MLA_NOPE], g[MLA_NOPE:]
    slot = jnp.concatenate([g_n, g_r, g_r[half:], g_r[:half]]) * factor
    return jnp.tile(slot, MLA_HEADS)[None, :].astype(F32)


def _prepare_weights(w_in, wq_up, wkv_up, qn_g, kn_g):
    rot = _rotate_half_matrix()
    s0 = MLA_Q_LORA
    s1 = s0 + MLA_KV_LORA
    s2 = s1 + MLA_ROPE
    w_kr = w_in[:, s1:s2]
    kr_slot = jnp.concatenate([jnp.zeros((D_MODEL, MLA_NOPE), F32), w_kr, w_kr @ rot], axis=1)
    win_p = jnp.concatenate([w_in[:, :s1], kr_slot, w_in[:, s2:]], axis=1).astype(BF16)

    wq = wq_up.reshape(MLA_Q_LORA, MLA_HEADS, MLA_QK)
    wq_r = wq[:, :, MLA_NOPE:]
    wq_ext = jnp.concatenate([wq, jnp.einsum('lhr,rs->lhs', wq_r, rot)], axis=2)
    wq_ext = wq_ext.reshape(MLA_Q_LORA, SLOT_W).astype(BF16)

    wkv = wkv_up.reshape(MLA_KV_LORA, MLA_HEADS, MLA_NOPE + MLA_V)
    wk_slot = jnp.concatenate([wkv[:, :, :MLA_NOPE], jnp.zeros((MLA_KV_LORA, MLA_HEADS, LANES - MLA_NOPE), F32)],
                              axis=2).reshape(MLA_KV_LORA, SLOT_W)
    wv = wkv[:, :, MLA_NOPE:].reshape(MLA_KV_LORA, V_W)
    wkv_ext = jnp.concatenate([wk_slot, wv], axis=1).astype(BF16)

    gq = _slot_gain(qn_g, MLA_QK ** -0.5 * math.log2(math.e))
    gk = _slot_gain(kn_g, 1.0)
    return win_p, wq_ext, wkv_ext, gq, gk


def _chunk_tables(lch, tm):
    nt = lch.shape[0]
    lchunks = _local_rows(tm) // ROW_CHUNK
    tch = EXPERT_TILE // ROW_CHUNK
    n_pad_steps = -(-(N_EXPERTS * (tch - 1)) // lchunks)
    while ((nt + n_pad_steps) * lchunks) % tch:
        n_pad_steps += 1
    n_steps = nt + n_pad_steps
    total = n_steps * lchunks
    n_tiles = total // tch

    nch = lch[:, 0, N_GROUPS:N_GROUPS + N_EXPERTS]
    lend = jnp.cumsum(nch, axis=1)
    lstart = lend - nch
    ltot = lend[:, -1]
    tot = jnp.sum(nch, axis=0)
    ptot = (tot + tch - 1) // tch * tch
    gend = jnp.cumsum(ptot)
    gstart = gend - ptot
    toff = jnp.cumsum(nch, axis=0) - nch
    c = jnp.arange(lchunks, dtype=I32)

    def lookup(idx, table):
        hit = idx[..., None] == jnp.arange(table.shape[-1], dtype=I32)
        return jnp.sum(jnp.where(hit, table, 0), axis=-1)

    e_of = jnp.minimum(jnp.sum(c[None, :, None] >= lend[:, None, :], axis=2), N_EXPERTS - 1)
    dest = lookup(e_of, (gstart[None, :] + toff - lstart)[:, None, :]) + c[None, :]
    valid = c[None, :] < ltot[:, None]
    src_table = jnp.where(valid, dest, 0).astype(I32)

    gap = jnp.concatenate([ptot - tot, total - gend[-1:]])
    gap_first = jnp.concatenate([gstart + tot, gend[-1:]])
    gap_cum = jnp.cumsum(gap)
    unused = lchunks - ltot
    rank = jnp.concatenate([
        (jnp.cumsum(unused) - unused)[:, None] + c[None, :] - ltot[:, None],
        jnp.sum(unused) + jnp.arange(n_pad_steps, dtype=I32)[:, None] * lchunks + c[None, :]])
    g_of = jnp.minimum(jnp.sum(rank[:, :, None] >= gap_cum[None, None, :], axis=2), N_EXPERTS)
    filler = lookup(g_of, gap_first - (gap_cum - gap)) + rank
    routed = jnp.concatenate([valid, jnp.zeros((n_pad_steps, lchunks), bool)])
    dst_table = jnp.where(routed, jnp.concatenate([dest, jnp.zeros((n_pad_steps, lchunks), I32)]), filler).astype(I32)

    ts = jnp.arange(n_tiles, dtype=I32) * tch
    tile_expert = jnp.minimum(jnp.sum(ts[:, None] >= gend[None, :], axis=1), N_EXPERTS - 1).astype(I32)
    seg_end = lookup(tile_expert, gstart + tot)
    tile_rows = (jnp.clip(seg_end - ts, 0, tch) * (ts < gend[-1]) * ROW_CHUNK).astype(I32)
    return dst_table, src_table, tile_expert, tile_rows, total * ROW_CHUNK


def _pick(n, pref):
    return pref if n % pref == 0 else n


def kernel(x, c, positions, ada_w, ada_b, norm1_g, w_in, mla_qa_g, mla_wq_up, mla_kva_g, mla_wkv_up, mla_qn_g, mla_kn_g, hg_lb_logits, hg_norm_g, w_out, norm2_g, router_group_w, router_group_b, router_expert_w, router_expert_b, w_gate, w_up, w_down):
    B, S, D = x.shape
    assert D == D_MODEL and ada_w.shape[0] == 1, "one layer of width 1024"
    T = B * S
    tm = _pick(S, 512)
    tiles_per_batch = S // tm

    mod3 = _mod_call(c, ada_w[0], ada_b).reshape(B, 6, D)

    win_p, wq_ext, wkv_ext, gq, gk = _prepare_weights(w_in[0], mla_wq_up[0], mla_wkv_up[0], mla_qn_g[0], mla_kn_g[0])
    q_hm, k_hm, vt, hg = _proj_call(x, mod3, norm1_g, win_p, mla_qa_g, wq_ext, mla_kva_g, wkv_ext, gq, gk,
                                    positions, tm)
    tq = _pick(S, ATTN_Q_TILE)
    attn_t = _attn_call(q_hm, k_hm, vt, tq)
    rec = _hgrn_call(hg, hg_lb_logits.reshape(2 * hg_lb_logits.shape[1], HG_WIDTH), hg_norm_g, hps=4)

    wo = w_out[0].astype(BF16)
    wr = jnp.concatenate([router_group_w[0], router_expert_w[0],
                          jnp.zeros((D, ROUTE_ROWS - N_GROUPS - N_EXPERTS), F32)], axis=1).astype(BF16)
    br = jnp.concatenate([router_group_b[0], router_expert_b[0],
                          jnp.zeros((ROUTE_ROWS - N_GROUPS - N_EXPERTS,), F32)])[None, :]
    x1, h2, info, wts, lch = _router_call(attn_t, rec.reshape(T, HG_WIDTH), x.reshape(T, D), mod3,
                                          wo[:V_W], wo[V_W:], norm2_g, wr, br, tm, tiles_per_batch)

    dst_table, src_table, tile_expert, tile_rows, cap_rows = _chunk_tables(lch, tm)
    xb = _dispatch_call(dst_table, info, h2, cap_rows, tm)
    yb = _expert_call(tile_expert, tile_rows, xb, w_gate[0], w_up[0], w_down[0])
    out = _combine_call(src_table, x1, mod3, wts, yb, tm, tiles_per_batch)
    return out.reshape(B, S, D)
```
